```python
import math
import jax
import jax.numpy as jnp
from jax import lax
import numpy as np

D_MODEL = 2048
BATCH = 4
SEQ = 2048
DEPTH = 4
DEC_BATCH = 8
DEC_SEQ = 4
PAST_LEN = 16384
PAGE_SIZE = 128

H_A = 8
DK_A = 128
DV_A = 128
D_A = H_A * DK_A
GLA_CHUNK = 64
H_B = 8
DH_B = 128
D_B = H_B * DH_B
MOBA_BLOCK = 256
MOBA_TOPK = 3
MOBA_Q_BLOCK = 16
N_MEM = 256
H_C = 4
DH_C = 256
D_C = H_C * DH_C
N_BRANCH = 3
D_FF = -(-8 * D_MODEL // (3 * 256)) * 256
D_IN = 4 * D_A + 3 * D_B + D_C + N_BRANCH * D_MODEL
EPS = 1e-6

kernel_name = 'hgrn2_moba_memory_gated_hybrid_step'


def _rmsnorm(x, g):
    xf = x.astype(jnp.float32)
    xf = xf * lax.rsqrt(jnp.mean(xf * xf, axis=-1, keepdims=True) + EPS)
    return xf.astype(x.dtype) * g


def _alibi_slopes(n):
    return 2.0 ** (-8.0 * jnp.arange(1, n + 1, dtype=jnp.float32) / n)


def _hgrn2_lower_bounds(lb_logits):
    p = jax.nn.softmax(lb_logits.astype(jnp.float32), axis=0)
    c = jnp.cumsum(p, axis=0)
    return c - c[0:1]


def _gla_chunked(q, k, v, log_f, s0):
    b, h, L, _ = q.shape
    c = math.gcd(L, GLA_CHUNK)
    n = L // c

    def to_chunks(t):
        return jnp.moveaxis(t.astype(jnp.float32).reshape(b, h, n, c, t.shape[-1]), 2, 0)

    tri = jnp.tril(jnp.ones((c, c), dtype=bool))

    def step(S, inp):
        qc, kc, vc, lfc = inp
        G = jnp.cumsum(lfc, axis=2)
        o_inter = jnp.einsum('bhtk,bhkv->bhtv', qc * jnp.exp(G), S)
        dG = jnp.where(tri[:, :, None], G[:, :, :, None, :] - G[:, :, None, :, :], -jnp.inf)
        A = jnp.einsum('bhtk,bhsk,bhtsk->bhts', qc, kc, jnp.exp(dG))
        o = o_inter + jnp.einsum('bhts,bhsv->bhtv', A, vc)
        G_last = G[:, :, -1:, :]
        S_new = jnp.exp(G_last[:, :, 0, :])[..., None] * S + jnp.einsum(
            'bhsk,bhsv->bhkv', kc * jnp.exp(G_last - G), vc)
        return S_new, o

    S, o = lax.scan(step, s0.astype(jnp.float32),
                    (to_chunks(q), to_chunks(k), to_chunks(v), to_chunks(log_f)))
    return jnp.moveaxis(o, 0, 2).reshape(b, h, L, -1), S


def _hgrn2_branch(q_raw, f_raw, i_raw, g_raw, lb, g_onorm, s0):
    bsz, L, _ = q_raw.shape

    def heads(t):
        return t.reshape(bsz, L, H_A, -1).transpose(0, 2, 1, 3)

    zf = f_raw.astype(jnp.float32)
    lbf = lb.astype(jnp.float32)
    log_f = jnp.logaddexp(jnp.log(lbf), jnp.log1p(-lbf) + jax.nn.log_sigmoid(zf))
    k = (1.0 - lbf) * jax.nn.sigmoid(-zf)
    q = jax.nn.silu(q_raw.astype(jnp.float32)) * (DK_A ** -0.5)
    o, S = _gla_chunked(heads(q), heads(k), heads(i_raw), heads(log_f), s0)
    o = _rmsnorm(o, g_onorm.astype(jnp.float32)).transpose(0, 2, 1, 3)
    o = o * jax.nn.silu(g_raw.astype(jnp.float32)).reshape(bsz, L, H_A, DV_A)
    return o.reshape(bsz, L, D_A).astype(q_raw.dtype), S


def _moba_attend(q, k_all, v_all, q_pos0, slopes):
    b, h, L, dh = q.shape
    T = k_all.shape[2]
    n_blk = -(-T // MOBA_BLOCK)
    pad = n_blk * MOBA_BLOCK - T
    kb = jnp.pad(k_all, ((0, 0), (0, 0), (0, pad), (0, 0))).reshape(b, h, n_blk, MOBA_BLOCK, dh)
    vb = jnp.pad(v_all, ((0, 0), (0, 0), (0, pad), (0, 0))).reshape(b, h, n_blk, MOBA_BLOCK, dh)
    k_mean = jnp.mean(kb.astype(jnp.float32), axis=3)
    q_pos = q_pos0 + jnp.arange(L, dtype=jnp.int32)
    q_blk = q_pos // MOBA_BLOCK
    gate = jnp.einsum('bhld,bhnd->bhln', q.astype(jnp.float32), k_mean)
    fully_past = jnp.arange(n_blk, dtype=jnp.int32)[None, :] < q_blk[:, None]
    gate = jnp.where(fully_past, gate, -jnp.inf)
    n_sel = min(MOBA_TOPK, n_blk)
    _, sel = lax.top_k(gate, n_sel)
    sel = sel.astype(jnp.int32)
    sel_ok = sel < q_blk[:, None]
    own = jnp.broadcast_to(q_blk[:, None], (b, h, L, 1))
    blk_idx = jnp.concatenate([sel, own], axis=-1)
    slot_ok = jnp.concatenate([sel_ok, jnp.ones((b, h, L, 1), dtype=bool)], axis=-1)
    qc = math.gcd(L, MOBA_Q_BLOCK)
    nq = L // qc

    def chunks(t):
        return jnp.moveaxis(t.reshape(b, h, nq, qc, *t.shape[3:]), 2, 0)

    gather = jax.vmap(jax.vmap(lambda tbl, idx: tbl[idx]))
    offs = jnp.arange(MOBA_BLOCK, dtype=jnp.int32)
    scale = DH_B ** -0.5

    def attend_block(args):
        qq, bi, ok, tp = args
        kg = gather(kb, bi)
        vg = gather(vb, bi)
        kpos = bi[..., None] * MOBA_BLOCK + offs
        s = jnp.einsum('bhqd,bhqnkd->bhqnk', qq, kg).astype(jnp.float32) * scale
        dist = (tp[:, None, None] - kpos).astype(jnp.float32)
        s = s - slopes[:, None, None, None] * dist
        mask = ok[..., None] & (kpos <= tp[:, None, None])
        s = jnp.where(mask, s, -jnp.inf)
        p = jax.nn.softmax(s.reshape(b, h, qc, -1), axis=-1).reshape(s.shape)
        return jnp.einsum('bhqnk,bhqnkd->bhqd', p.astype(vg.dtype), vg)

    out = lax.map(attend_block, (chunks(q), chunks(blk_idx), chunks(slot_ok), q_pos.reshape(nq, qc)))
    return jnp.moveaxis(out, 0, 2).reshape(b, h, L, dh)


def _memory_kv(mem, g_mem, w_kv, g_k):
    bsz, m, _ = mem.shape
    kv = _rmsnorm(mem, g_mem) @ w_kv
    k, v = jnp.split(kv, [D_C], axis=-1)
    k = _rmsnorm(k.reshape(bsz, m, H_C, DH_C), g_k)
    return k, v.reshape(bsz, m, H_C, DH_C)


def _memory_attend(q, k, v):
    bsz, L = q.shape[0], q.shape[1]
    s = jnp.einsum('blhd,bmhd->bhlm', q, k.astype(q.dtype)).astype(jnp.float32) * (DH_C ** -0.5)
    p = jax.nn.softmax(s, axis=-1)
    o = jnp.einsum('bhlm,bmhd->blhd', p.astype(q.dtype), v.astype(q.dtype))
    return o.reshape(bsz, L, D_C)


def _mixer_sublayer(x, lw, slopes, s0, k_past, v_past, mem_k, mem_v):
    (g_mix, w_in_l, lb_l, g_oa, g_qb, g_kb, g_qc, w_pa, w_pb, w_pc, w_o) = lw
    bsz, L, _ = x.shape
    h = _rmsnorm(x, g_mix)
    z = h @ w_in_l
    cuts = np.cumsum([D_A, D_A, D_A, D_A, D_B, D_B, D_B, D_C]).tolist()
    q_a, f_a, i_a, g_a, q_b, k_b, v_b, q_c, gates = jnp.split(z, cuts, axis=-1)
    o_a, s_new = _hgrn2_branch(q_a, f_a, i_a, g_a, lb_l, g_oa, s0)
    q_b = _rmsnorm(q_b.reshape(bsz, L, H_B, DH_B), g_qb)
    k_b = _rmsnorm(k_b.reshape(bsz, L, H_B, DH_B), g_kb)
    v_b = v_b.reshape(bsz, L, H_B, DH_B)
    if k_past is None:
        k_all, v_all, pos0 = k_b, v_b, 0
    else:
        k_all = jnp.concatenate([k_past.astype(k_b.dtype), k_b], axis=1)
        v_all = jnp.concatenate([v_past.astype(v_b.dtype), v_b], axis=1)
        pos0 = k_past.shape[1]
    o_b = _moba_attend(q_b.transpose(0, 2, 1, 3), k_all.transpose(0, 2, 1, 3),
                       v_all.transpose(0, 2, 1, 3), pos0, slopes)
    o_b = o_b.transpose(0, 2, 1, 3).reshape(bsz, L, D_B)
    q_c = _rmsnorm(q_c.reshape(bsz, L, H_C, DH_C), g_qc)
    o_c = _memory_attend(q_c, mem_k, mem_v)
    gt = jax.nn.sigmoid(gates.astype(jnp.float32)).astype(x.dtype).reshape(bsz, L, N_BRANCH, D_MODEL)
    merged = gt[:, :, 0] * (o_a @ w_pa) + gt[:, :, 1] * (o_b @ w_pb) + gt[:, :, 2] * (o_c @ w_pc)
    return x + merged @ w_o, s_new, k_b, v_b


def _ffn_sublayer(x, g, w_gate, w_up, w_down):
    h = _rmsnorm(x, g)
    return x + (jax.nn.silu(h @ w_gate) * (h @ w_up)) @ w_down


def setup_inputs(seed: int = 0) -> dict:
    key = jax.random.key(seed)
    ks = jax.random.split(key, 32)
    f32 = jnp.float32
    n_pages = PAST_LEN // PAGE_SIZE
    n_used = DEC_BATCH * n_pages
    n_pool = n_used + (n_used + 3) // 4
    res_scale = (2.0 * DEPTH) ** -0.5

    def nrm(k, shape, scale=1.0):
        return jax.random.normal(k, shape, f32) * scale

    def gain(k, shape):
        return 1.0 + 0.02 * jax.random.normal(k, shape, f32)

    page_table = jax.random.permutation(ks[9], n_pool)[:n_used].reshape(DEC_BATCH, n_pages).astype(jnp.int32)
    return {
        'x_prompt': nrm(ks[0], (BATCH, SEQ, D_MODEL)),
        'x_sample': nrm(ks[1], (DEC_BATCH, DEC_SEQ, D_MODEL)),
        'state_hgrn': nrm(ks[2], (DEPTH, DEC_BATCH, H_A, DK_A, DV_A), 0.5),
        'cache_k': nrm(ks[3], (DEPTH, n_pool, PAGE_SIZE, H_B, DH_B)),
        'cache_v': nrm(ks[4], (DEPTH, n_pool, PAGE_SIZE, H_B, DH_B)),
        'cache_mem_k': nrm(ks[5], (DEPTH, DEC_BATCH, N_MEM, H_C, DH_C)),
        'cache_mem_v': nrm(ks[6], (DEPTH, DEC_BATCH, N_MEM, H_C, DH_C)),
        'page_table': page_table,
        'mem_prompt': nrm(ks[7], (BATCH, N_MEM, D_MODEL)),
        'lb_logits': nrm(ks[8], (DEPTH, D_A), 0.1),
        'norm_mix': gain(ks[10], (DEPTH, D_MODEL)),
        'w_in': nrm(ks[11], (DEPTH, D_MODEL, D_IN), D_MODEL ** -0.5),
        'norm_o_a': gain(ks[12], (DEPTH, DV_A)),
        'norm_q_b': gain(ks[13], (DEPTH, DH_B)),
        'norm_k_b': gain(ks[14], (DEPTH, DH_B)),
        'norm_q_c': gain(ks[15], (DEPTH, DH_C)),
        'norm_k_c': gain(ks[16], (DEPTH, DH_C)),
        'norm_mem': gain(ks[17], (DEPTH, D_MODEL)),
        'w_mem_kv': nrm(ks[18], (DEPTH, D_MODEL, 2 * D_C), D_MODEL ** -0.5),
        'w_br_a': nrm(ks[19], (DEPTH, D_A, D_MODEL), D_A ** -0.5),
        'w_br_b': nrm(ks[20], (DEPTH, D_B, D_MODEL), D_B ** -0.5),
        'w_br_c': nrm(ks[21], (DEPTH, D_C, D_MODEL), D_C ** -0.5),
        'w_out': nrm(ks[22], (DEPTH, D_MODEL, D_MODEL), res_scale * D_MODEL ** -0.5),
        'norm_ffn': gain(ks[23], (DEPTH, D_MODEL)),
        'w_gate': nrm(ks[24], (DEPTH, D_MODEL, D_FF), D_MODEL ** -0.5),
        'w_up': nrm(ks[25], (DEPTH, D_MODEL, D_FF), D_MODEL ** -0.5),
        'w_down': nrm(ks[26], (DEPTH, D_FF, D_MODEL), res_scale * D_FF ** -0.5),
    }


def reference(x_prompt, x_sample, state_hgrn, cache_k, cache_v, cache_mem_k, cache_mem_v, page_table,
              mem_prompt, lb_logits, norm_mix, w_in, norm_o_a, norm_q_b, norm_k_b, norm_q_c, norm_k_c,
              norm_mem, w_mem_kv, w_br_a, w_br_b, w_br_c, w_out, norm_ffn, w_gate, w_up, w_down):
    lower = _hgrn2_lower_bounds(lb_logits)
    slopes = _alibi_slopes(H_B)
    n_seq_s = page_table.shape[0]
    xp, xs = x_prompt, x_sample
    hp, kp, vp, mkp, mvp, hs, ksm, vsm = [], [], [], [], [], [], [], []
    for l in range(DEPTH):
        mem_k_p, mem_v_p = _memory_kv(mem_prompt, norm_mem[l], w_mem_kv[l], norm_k_c[l])
        k_past = cache_k[l][page_table].reshape(n_seq_s, -1, H_B, DH_B)
        v_past = cache_v[l][page_table].reshape(n_seq_s, -1, H_B, DH_B)
        lw = (norm_mix[l], w_in[l], lower[l], norm_o_a[l], norm_q_b[l], norm_k_b[l], norm_q_c[l],
              w_br_a[l], w_br_b[l], w_br_c[l], w_out[l])
        s0_p = jnp.zeros((xp.shape[0], H_A, DK_A, DV_A), jnp.float32)
        xp, s_p, k_p, v_p = _mixer_sublayer(xp, lw, slopes, s0_p, None, None, mem_k_p, mem_v_p)
        xs, s_s, k_s, v_s = _mixer_sublayer(xs, lw, slopes, state_hgrn[l], k_past, v_past,
                                            cache_mem_k[l], cache_mem_v[l])
        xp = _ffn_sublayer(xp, norm_ffn[l], w_gate[l], w_up[l], w_down[l])
        xs = _ffn_sublayer(xs, norm_ffn[l], w_gate[l], w_up[l], w_down[l])
        hp.append(s_p); kp.append(k_p); vp.append(v_p); mkp.append(mem_k_p); mvp.append(mem_v_p)
        hs.append(s_s); ksm.append(k_s); vsm.append(v_s)
    return (xp, xs, jnp.stack(hp), jnp.stack(kp), jnp.stack(vp), jnp.stack(mkp), jnp.stack(mvp),
            jnp.stack(hs), jnp.stack(ksm), jnp.stack(vsm))
```

```python
import functools

import jax
import jax.numpy as jnp
from jax import lax
from jax.experimental import pallas as pl
from jax.experimental.pallas import tpu as pltpu

D_MODEL = 2048
H_A, DK_A, DV_A = 8, 128, 128
D_A = H_A * DK_A
H_B, DH_B = 8, 128
D_B = H_B * DH_B
MOBA_BLOCK = 256
MOBA_TOPK = 3
PAGE_SIZE = 128
PAGES_PER_BLOCK = MOBA_BLOCK // PAGE_SIZE
N_MEM = 256
H_C, DH_C = 4, 256
D_C = H_C * DH_C
N_BRANCH = 3
D_IN = 4 * D_A + 3 * D_B + D_C + N_BRANCH * D_MODEL
EPS = 1e-6

OFF_QA, OFF_FA, OFF_IA, OFF_GA = 0, D_A, 2 * D_A, 3 * D_A
OFF_QB = 4 * D_A
OFF_KB = OFF_QB + D_B
OFF_VB = OFF_KB + D_B
OFF_QC = OFF_VB + D_B
OFF_GATE = OFF_QC + D_C

GLA_SUB = 16
SAMPLE_ROWS = 16
V7X_VMEM_LIMIT = 56 * 1024 * 1024

F32 = jnp.float32
BF16 = jnp.bfloat16
_NT = (((1,), (1,)), ((), ()))
_TN = (((0,), (0,)), ((), ()))


def _params(*semantics):
    return pltpu.CompilerParams(dimension_semantics=semantics, vmem_limit_bytes=V7X_VMEM_LIMIT)


def _tile(n, pref):
    t = min(n, pref)
    assert n % t == 0, (n, pref)
    return t


def _rms(x, gain):
    return x * lax.rsqrt(jnp.mean(x * x, axis=-1, keepdims=True) + EPS) * gain


def _rmsnorm_kernel(x_ref, g_ref, o_ref):
    o_ref[...] = _rms(x_ref[...], g_ref[...]).astype(o_ref.dtype)


def _rmsnorm(x, gains, layer):
    m, d = x.shape
    tm = _tile(m, 512)
    return pl.pallas_call(
        _rmsnorm_kernel,
        out_shape=jax.ShapeDtypeStruct((m, d), BF16),
        grid=(m // tm,),
        in_specs=[pl.BlockSpec((tm, d), lambda i: (i, 0)),
                  pl.BlockSpec((None, 1, d), lambda i: (layer, 0, 0))],
        out_specs=pl.BlockSpec((tm, d), lambda i: (i, 0)),
        compiler_params=_params("parallel"),
        name="rmsnorm",
    )(x, gains)


def _dot(a, b):
    return jnp.dot(a, b, preferred_element_type=F32)


def _mm_kernel(a_ref, w_ref, o_ref):
    o_ref[...] = _dot(a_ref[...], w_ref[...]).astype(o_ref.dtype)


def _mm(a, w, layer, *, tm, tn, out_dtype=F32):
    m, k = a.shape
    n = w.shape[-1]
    tm, tn = _tile(m, tm), _tile(n, tn)
    return pl.pallas_call(
        _mm_kernel,
        out_shape=jax.ShapeDtypeStruct((m, n), out_dtype),
        grid=(m // tm, n // tn),
        in_specs=[pl.BlockSpec((tm, k), lambda i, j: (i, 0)),
                  pl.BlockSpec((None, k, tn), lambda i, j: (layer, 0, j))],
        out_specs=pl.BlockSpec((tm, tn), lambda i, j: (i, j)),
        compiler_params=_params("parallel", "arbitrary"),
        name="mm",
    )(a, w)


def _mm_residual_kernel(a_ref, w_ref, r_ref, o_ref):
    o_ref[...] = r_ref[...] + _dot(a_ref[...], w_ref[...])


def _mm_residual(a, w, layer, res, *, tm, tn):
    m, k = a.shape
    n = w.shape[-1]
    tm, tn = _tile(m, tm), _tile(n, tn)
    return pl.pallas_call(
        _mm_residual_kernel,
        out_shape=jax.ShapeDtypeStruct((m, n), F32),
        grid=(m // tm, n // tn),
        in_specs=[pl.BlockSpec((tm, k), lambda i, j: (i, 0)),
                  pl.BlockSpec((None, k, tn), lambda i, j: (layer, 0, j)),
                  pl.BlockSpec((tm, tn), lambda i, j: (i, j))],
        out_specs=pl.BlockSpec((tm, tn), lambda i, j: (i, j)),
        compiler_params=_params("parallel", "arbitrary"),
        name="mm_residual",
    )(a, w, res)


def _mm_swiglu_kernel(a_ref, wg_ref, wu_ref, o_ref):
    a = a_ref[...]
    g = _dot(a, wg_ref[...])
    u = _dot(a, wu_ref[...])
    o_ref[...] = (g * jax.nn.sigmoid(g) * u).astype(o_ref.dtype)


def _mm_swiglu(a, wg, wu, layer, *, tm, tn):
    m, k = a.shape
    n = wg.shape[-1]
    tm, tn = _tile(m, tm), _tile(n, tn)
    wspec = pl.BlockSpec((None, k, tn), lambda i, j: (layer, 0, j))
    return pl.pallas_call(
        _mm_swiglu_kernel,
        out_shape=jax.ShapeDtypeStruct((m, n), BF16),
        grid=(m // tm, n // tn),
        in_specs=[pl.BlockSpec((tm, k), lambda i, j: (i, 0)), wspec, wspec],
        out_specs=pl.BlockSpec((tm, tn), lambda i, j: (i, j)),
        compiler_params=_params("parallel", "arbitrary"),
        name="mm_swiglu",
    )(a, wg, wu)


def _mm_merge_kernel(oa_ref, ob_ref, oc_ref, g0_ref, g1_ref, g2_ref, wa_ref, wb_ref, wc_ref, o_ref):
    merged = (jax.nn.sigmoid(g0_ref[...]) * _dot(oa_ref[...], wa_ref[...])
              + jax.nn.sigmoid(g1_ref[...]) * _dot(ob_ref[...], wb_ref[...])
              + jax.nn.sigmoid(g2_ref[...]) * _dot(oc_ref[...], wc_ref[...]))
    o_ref[...] = merged.astype(o_ref.dtype)


def _mm_merge(o_a, o_b, o_c, z, w_a, w_b, w_c, layer, *, tm, tn):
    m, k = o_a.shape
    n = w_a.shape[-1]
    tm, tn = _tile(m, tm), _tile(n, tn)
    aspec = pl.BlockSpec((tm, k), lambda i, j: (i, 0))
    wspec = pl.BlockSpec((None, k, tn), lambda i, j: (layer, 0, j))

    def gate_spec(branch):
        first = (OFF_GATE + branch * D_MODEL) // tn
        return pl.BlockSpec((tm, tn), lambda i, j: (i, first + j))

    return pl.pallas_call(
        _mm_merge_kernel,
        out_shape=jax.ShapeDtypeStruct((m, n), BF16),
        grid=(m // tm, n // tn),
        in_specs=[aspec, aspec, aspec, gate_spec(0), gate_spec(1), gate_spec(2), wspec, wspec, wspec],
        out_specs=pl.BlockSpec((tm, tn), lambda i, j: (i, j)),
        compiler_params=_params("parallel", "arbitrary"),
        name="mm_merge",
    )(o_a, o_b, o_c, z, z, z, w_a, w_b, w_c)


def _lower_bounds_kernel(x_ref, o_ref):
    x = x_ref[...]
    e = jnp.exp(x - jnp.max(x, axis=0, keepdims=True))
    p = e / jnp.sum(e, axis=0, keepdims=True)
    rows = [jnp.zeros_like(p[0:1])]
    for r in range(1, x.shape[0]):
        rows.append(rows[-1] + p[r:r + 1])
    o_ref[...] = jnp.concatenate(rows, axis=0)


def _lower_bounds(lb_logits):
    return pl.pallas_call(
        _lower_bounds_kernel,
        out_shape=jax.ShapeDtypeStruct(lb_logits.shape, F32),
        name="hgrn2_lower_bounds",
    )(lb_logits)


def _hgrn2_kernel(q_ref, f_ref, i_ref, g_ref, lb_ref, gn_ref, s0_ref, o_ref, sout_ref, st_ref,
                  *, chunk, valid_len):
    c = pl.program_id(2)

    @pl.when(c == 0)
    def _():
        st_ref[...] = s0_ref[...].T

    zf = f_ref[...]
    lb = lb_ref[...]
    log_sig = jnp.minimum(zf, 0.0) - jnp.log1p(jnp.exp(-jnp.abs(zf)))
    la = jnp.log(lb)
    lc = jnp.log1p(-lb) + log_sig
    log_f = jnp.maximum(la, lc) + jnp.log1p(jnp.exp(-jnp.abs(la - lc)))
    kk = (1.0 - lb) * jax.nn.sigmoid(-zf)
    qr = q_ref[...]
    q = qr * jax.nn.sigmoid(qr) * (DK_A ** -0.5)
    v = i_ref[...]

    row = lax.broadcasted_iota(jnp.int32, (chunk, 1), 0)
    if valid_len is not None:
        live = (c * chunk + row) < valid_len
        log_f = jnp.where(live, log_f, 0.0)
        kk = jnp.where(live, kk, 0.0)

    tri = (lax.broadcasted_iota(jnp.int32, (chunk, chunk), 0)
           >= lax.broadcasted_iota(jnp.int32, (chunk, chunk), 1)).astype(F32)
    G = jnp.dot(tri, log_f, precision=lax.Precision.HIGHEST, preferred_element_type=F32)

    st = st_ref[...]
    o = lax.dot_general((q * jnp.exp(G)).astype(BF16), st.astype(BF16), _NT, preferred_element_type=F32)

    pieces = [jnp.zeros((GLA_SUB, DV_A), F32)]
    for i in range(1, chunk // GLA_SUB):
        lo = i * GLA_SUB
        r = G[lo - 1:lo]
        qi = (q[lo:lo + GLA_SUB] * jnp.exp(G[lo:lo + GLA_SUB] - r)).astype(BF16)
        kj = (kk[:lo] * jnp.exp(r - G[:lo])).astype(BF16)
        a = lax.dot_general(qi, kj, _NT, preferred_element_type=F32)
        pieces.append(_dot(a.astype(BF16), v[:lo].astype(BF16)))
    o = o + jnp.concatenate(pieces, axis=0)

    rmod = row % GLA_SUB
    for d in range(GLA_SUB):
        k_d = kk if d == 0 else pltpu.roll(kk, d, axis=0)
        g_d = G if d == 0 else pltpu.roll(G, d, axis=0)
        v_d = v if d == 0 else pltpu.roll(v, d, axis=0)
        decay = jnp.exp(jnp.where(rmod >= d, G - g_d, -jnp.inf))
        a_d = jnp.sum(q * k_d * decay, axis=-1, keepdims=True)
        o = o + a_d * v_d

    g_last = G[chunk - 1:chunk]
    kd = (kk * jnp.exp(g_last - G)).astype(BF16)
    st_new = st * jnp.exp(g_last) + lax.dot_general(v.astype(BF16), kd, _TN, preferred_element_type=F32)
    st_ref[...] = st_new

    @pl.when(c == pl.num_programs(2) - 1)
    def _():
        sout_ref[...] = st_new.T

    gr = g_ref[...]
    o_ref[...] = (_rms(o, gn_ref[...]) * (gr * jax.nn.sigmoid(gr))).astype(o_ref.dtype)


def _hgrn2(z, lower, g_onorm, s0, layer, s0_layer, *, chunk, valid_len=None):
    b, l, _ = z.shape
    chunk = _tile(l, chunk)

    def col_spec(off):
        first = off // DK_A
        return pl.BlockSpec((None, chunk, DK_A), lambda bi, h, c: (bi, c, first + h))

    state_spec = pl.BlockSpec((None, None, DK_A, DV_A), lambda bi, h, c: (bi, h, 0, 0))
    return pl.pallas_call(
        functools.partial(_hgrn2_kernel, chunk=chunk, valid_len=valid_len),
        out_shape=(jax.ShapeDtypeStruct((b, l, D_A), BF16),
                   jax.ShapeDtypeStruct((b, H_A, DK_A, DV_A), F32)),
        grid=(b, H_A, l // chunk),
        in_specs=[col_spec(OFF_QA), col_spec(OFF_FA), col_spec(OFF_IA), col_spec(OFF_GA),
                  pl.BlockSpec((None, 1, DK_A), lambda bi, h, c: (layer, 0, h)),
                  pl.BlockSpec((None, 1, DV_A), lambda bi, h, c: (layer, 0, 0)),
                  pl.BlockSpec((None, None, None, DK_A, DV_A), lambda bi, h, c: (s0_layer, bi, h, 0, 0))],
        out_specs=(pl.BlockSpec((None, chunk, DV_A), lambda bi, h, c: (bi, c, h)), state_spec),
        scratch_shapes=[pltpu.VMEM((DV_A, DK_A), F32)],
        compiler_params=_params("parallel", "parallel", "arbitrary"),
        name="hgrn2",
    )(z, z, z, z, lower, g_onorm, s0)


def _moba_kv_kernel(k_ref, v_ref, g_ref, kn_ref, kb_ref, vb_ref, km_ref):
    g = g_ref[...]
    for h in range(H_B):
        sl = slice(h * DH_B, (h + 1) * DH_B)
        kn = _rms(k_ref[:, sl], g)
        kn_ref[:, sl] = kn
        kb_ref[:, sl] = kn.astype(BF16)
        km_ref[:, sl] = jnp.mean(kn, axis=0, keepdims=True)
    vb_ref[...] = v_ref[...].astype(BF16)


def _moba_kv(z, g_kb, layer, *, rows):
    b, l, _ = z.shape
    rows = _tile(l, rows)
    nblk = l // rows
    act = lambda dtype: jax.ShapeDtypeStruct((b, l, D_B), dtype)
    act_spec = pl.BlockSpec((None, rows, D_B), lambda bi, n: (bi, n, 0))
    return pl.pallas_call(
        _moba_kv_kernel,
        out_shape=(act(F32), act(BF16), act(BF16), jax.ShapeDtypeStruct((b, nblk, 1, D_B), F32)),
        grid=(b, nblk),
        in_specs=[pl.BlockSpec((None, rows, D_B), lambda bi, n: (bi, n, OFF_KB // D_B)),
                  pl.BlockSpec((None, rows, D_B), lambda bi, n: (bi, n, OFF_VB // D_B)),
                  pl.BlockSpec((None, 1, DH_B), lambda bi, n: (layer, 0, 0))],
        out_specs=(act_spec, act_spec, act_spec,
                   pl.BlockSpec((None, None, 1, D_B), lambda bi, n: (bi, n, 0, 0))),
        compiler_params=_params("parallel", "parallel"),
        name="moba_kv",
    )(z, z, g_kb)


def _topk_mask(gate, n_valid, topk):
    rows, n = gate.shape
    col = lax.broadcasted_iota(jnp.int32, (rows, n), 1)
    gm = jnp.where(col < n_valid, gate, -jnp.inf)
    sel = jnp.zeros((rows, n), F32)
    for j in range(n):
        gj = gm[:, j:j + 1]
        beats = jnp.where(gm > gj, 1.0, jnp.where((gm == gj) & (col < j), 1.0, 0.0))
        rank = jnp.sum(beats, axis=-1, keepdims=True)
        sel = jnp.where((col == j) & (rank < topk) & (col < n_valid), 1.0, sel)
    return sel


def _moba_prompt_kernel(slopes_ref, q_ref, gq_ref, k_ref, v_ref, km_ref, o_ref):
    h = pl.program_id(1)
    i = pl.program_id(2)
    blk = MOBA_BLOCK
    slope = slopes_ref[h]
    scale = DH_B ** -0.5

    qn = _rms(q_ref[...], gq_ref[...])
    gate = lax.dot_general(qn, km_ref[...], _NT, precision=lax.Precision.HIGHEST,
                           preferred_element_type=F32)
    sel = _topk_mask(gate, i, MOBA_TOPK)
    sel_col = lax.broadcasted_iota(jnp.int32, sel.shape, 1)

    qb = qn.astype(BF16)
    t_pos = i * blk + lax.broadcasted_iota(jnp.int32, (blk, 1), 0)
    k_off = lax.broadcasted_iota(jnp.int32, (1, blk), 1)

    def scores(j):
        start = pl.multiple_of(j * blk, blk)
        s = lax.dot_general(qb, k_ref[pl.ds(start, blk), :], _NT, preferred_element_type=F32) * scale
        k_pos = j * blk + k_off
        return s - slope * (t_pos - k_pos).astype(F32), k_pos, start

    s, k_pos, start = scores(i)
    s = jnp.where(k_pos <= t_pos, s, -jnp.inf)
    m0 = jnp.max(s, axis=-1, keepdims=True)
    p = jnp.exp(s - m0)
    l0 = jnp.sum(p, axis=-1, keepdims=True)
    acc0 = _dot(p.astype(BF16), v_ref[pl.ds(start, blk), :])

    def past_block(j, carry):
        m, l, acc = carry
        s, _, start = scores(j)
        chosen = jnp.sum(jnp.where(sel_col == j, sel, 0.0), axis=-1, keepdims=True)
        s = jnp.where(chosen > 0.0, s, -jnp.inf)
        m_new = jnp.maximum(m, jnp.max(s, axis=-1, keepdims=True))
        alpha = jnp.exp(m - m_new)
        p = jnp.exp(s - m_new)
        l = alpha * l + jnp.sum(p, axis=-1, keepdims=True)
        acc = alpha * acc + _dot(p.astype(BF16), v_ref[pl.ds(start, blk), :])
        return m_new, l, acc

    _, l, acc = lax.fori_loop(0, i, past_block, (m0, l0, acc0))
    o_ref[...] = (acc / l).astype(o_ref.dtype)


def _moba_prompt(z, g_qb, kb, vb, kmean, slopes, layer):
    b, l, _ = z.shape
    nblk = l // MOBA_BLOCK
    seq_spec = pl.BlockSpec((None, l, DH_B), lambda bi, h, i: (bi, 0, h))
    return pl.pallas_call(
        _moba_prompt_kernel,
        out_shape=jax.ShapeDtypeStruct((b, l, D_B), BF16),
        grid=(b, H_B, nblk),
        in_specs=[pl.BlockSpec(memory_space=pltpu.SMEM),
                  pl.BlockSpec((None, MOBA_BLOCK, DH_B), lambda bi, h, i: (bi, i, OFF_QB // DH_B + h)),
                  pl.BlockSpec((None, 1, DH_B), lambda bi, h, i: (layer, 0, 0)),
                  seq_spec, seq_spec,
                  pl.BlockSpec((None, nblk, DH_B), lambda bi, h, i: (bi, 0, h))],
        out_specs=pl.BlockSpec((None, MOBA_BLOCK, DH_B), lambda bi, h, i: (bi, i, h)),
        compiler_params=_params("parallel", "parallel", "arbitrary"),
        name="moba_prompt",
    )(slopes, z, g_qb, kb, vb, kmean)


def _page_mean_kernel(pt_ref, *refs):
    del pt_ref
    page_refs, o_ref = refs[:-1], refs[-1]
    total = jnp.sum(page_refs[0][...], axis=0, keepdims=True)
    for r in page_refs[1:]:
        total = total + jnp.sum(r[...], axis=0, keepdims=True)
    o_ref[...] = total * (1.0 / MOBA_BLOCK)


def _page_means(cache_k, page_table_flat, layer, n_seq, n_pages):
    n_blocks = n_pages // PAGES_PER_BLOCK

    def page_spec(p):
        return pl.BlockSpec(
            (None, None, PAGE_SIZE, D_B),
            lambda bi, n, pt: (layer, pt[bi * n_pages + n * PAGES_PER_BLOCK + p], 0, 0))

    out = pl.pallas_call(
        _page_mean_kernel,
        out_shape=jax.ShapeDtypeStruct((n_seq, n_blocks, 1, D_B), F32),
        grid_spec=pltpu.PrefetchScalarGridSpec(
            num_scalar_prefetch=1,
            grid=(n_seq, n_blocks),
            in_specs=[page_spec(p) for p in range(PAGES_PER_BLOCK)],
            out_specs=pl.BlockSpec((None, None, 1, D_B), lambda bi, n, pt: (bi, n, 0, 0))),
        compiler_params=_params("parallel", "parallel"),
        name="moba_page_means",
    )(page_table_flat, *([cache_k] * PAGES_PER_BLOCK))
    return out.reshape(n_seq, n_blocks, D_B)


def _moba_select_kernel(q_ref, gq_ref, km_ref, qn_ref, sel_ref):
    rows = q_ref.shape[0]
    n_blocks = km_ref.shape[0]
    g = gq_ref[...]
    col = lax.broadcasted_iota(jnp.int32, (rows, n_blocks), 1)
    lane = lax.broadcasted_iota(jnp.int32, (rows, sel_ref.shape[-1]), 1)
    for h in range(H_B):
        sl = slice(h * DH_B, (h + 1) * DH_B)
        qn = _rms(q_ref[:, sl], g)
        qn_ref[:, sl] = qn
        gate = lax.dot_general(qn, km_ref[:, sl], _NT, precision=lax.Precision.HIGHEST,
                               preferred_element_type=F32)
        picks = jnp.zeros(lane.shape, jnp.int32)
        for r in range(MOBA_TOPK):
            best = jnp.max(gate, axis=-1, keepdims=True)
            idx = jnp.min(jnp.where(gate == best, col, n_blocks), axis=-1, keepdims=True)
            picks = jnp.where(lane == r, idx, picks)
            gate = jnp.where(col == idx, -jnp.inf, gate)
        sel_ref[h] = picks


def _moba_select(z, g_qb, kmean, layer):
    b, rows, _ = z.shape
    n_blocks = kmean.shape[1]
    return pl.pallas_call(
        _moba_select_kernel,
        out_shape=(jax.ShapeDtypeStruct((b, rows, D_B), F32),
                   jax.ShapeDtypeStruct((b, H_B, rows, 128), jnp.int32)),
        grid=(b,),
        in_specs=[pl.BlockSpec((None, rows, D_B), lambda bi: (bi, 0, OFF_QB // D_B)),
                  pl.BlockSpec((None, 1, DH_B), lambda bi: (layer, 0, 0)),
                  pl.BlockSpec((None, n_blocks, D_B), lambda bi: (bi, 0, 0))],
        out_specs=(pl.BlockSpec((None, rows, D_B), lambda bi: (bi, 0, 0)),
                   pl.BlockSpec((None, H_B, rows, 128), lambda bi: (bi, 0, 0, 0))),
        compiler_params=_params("parallel"),
        name="moba_select",
    )(z, g_qb, kmean)


def _moba_sample_kernel(sel_ref, pt_ref, slopes_ref, q_ref, kown_ref, vown_ref, *refs, past_len, n_tok):
    del pt_ref
    n_pg = PAGES_PER_BLOCK
    k_pages, v_pages = refs[:n_pg], refs[n_pg:2 * n_pg]
    o_ref, m_ref, l_ref, acc_ref = refs[2 * n_pg:]
    bi, h, t, slot = (pl.program_id(a) for a in range(4))
    slope = slopes_ref[h]
    scale = DH_B ** -0.5
    q = q_ref[pl.ds(t, 1), :]
    t_pos = past_len + t

    def fold(s, vals, m, l, acc):
        m_new = jnp.maximum(m, jnp.max(s, axis=0, keepdims=True))
        alpha = jnp.exp(m - m_new)
        p = jnp.exp(s - m_new)
        l = alpha * l + jnp.sum(p, axis=0, keepdims=True)
        acc = alpha * acc + jnp.sum(p * vals, axis=0, keepdims=True)
        return m_new, l, acc

    @pl.when(slot == 0)
    def _():
        rows = kown_ref.shape[0]
        k_pos = past_len + lax.broadcasted_iota(jnp.int32, (rows, 1), 0)
        s = jnp.sum(kown_ref[...] * q, axis=-1, keepdims=True) * scale
        s = s - slope * (t_pos - k_pos).astype(F32)
        s = jnp.where(k_pos <= t_pos, s, -jnp.inf)
        m, l, acc = fold(s, vown_ref[...], jnp.full((1, 1), -jnp.inf, F32), jnp.zeros((1, 1), F32),
                         jnp.zeros((1, DH_B), F32))
        m_ref[...] = m
        l_ref[...] = l
        acc_ref[...] = acc

    block = sel_ref[((bi * H_B + h) * n_tok + t) * MOBA_TOPK + slot]
    m, l, acc = m_ref[...], l_ref[...], acc_ref[...]
    for p in range(n_pg):
        k_pos = block * MOBA_BLOCK + p * PAGE_SIZE + lax.broadcasted_iota(jnp.int32, (PAGE_SIZE, 1), 0)
        s = jnp.sum(k_pages[p][...] * q, axis=-1, keepdims=True) * scale
        s = s - slope * (t_pos - k_pos).astype(F32)
        m, l, acc = fold(s, v_pages[p][...], m, l, acc)
    m_ref[...] = m
    l_ref[...] = l
    acc_ref[...] = acc

    @pl.when(slot == MOBA_TOPK - 1)
    def _():
        o_ref[...] = acc / l


def _moba_sample(qn, kn, z, cache_k, cache_v, picks_flat, page_table_flat, slopes, layer, *,
                 n_tok, n_pages):
    b, rows, _ = qn.shape
    past_len = n_pages * PAGE_SIZE
    own_spec = pl.BlockSpec((None, rows, DH_B), lambda bi, h, t, s, sel, pt: (bi, 0, h))

    def page_spec(p):
        def index(bi, h, t, s, sel, pt):
            block = sel[((bi * H_B + h) * n_tok + t) * MOBA_TOPK + s]
            return layer, pt[bi * n_pages + block * PAGES_PER_BLOCK + p], 0, h
        return pl.BlockSpec((None, None, PAGE_SIZE, DH_B), index)

    pages = [page_spec(p) for p in range(PAGES_PER_BLOCK)]
    out = pl.pallas_call(
        functools.partial(_moba_sample_kernel, past_len=past_len, n_tok=n_tok),
        out_shape=jax.ShapeDtypeStruct((b, n_tok, H_B, 1, DH_B), F32),
        grid_spec=pltpu.PrefetchScalarGridSpec(
            num_scalar_prefetch=2,
            grid=(b, H_B, n_tok, MOBA_TOPK),
            in_specs=[pl.BlockSpec(memory_space=pltpu.SMEM), own_spec, own_spec,
                      pl.BlockSpec((None, rows, DH_B), lambda bi, h, t, s, sel, pt: (bi, 0, OFF_VB // DH_B + h)),
                      *pages, *pages],
            out_specs=pl.BlockSpec((None, None, None, 1, DH_B), lambda bi, h, t, s, sel, pt: (bi, t, h, 0, 0)),
            scratch_shapes=[pltpu.VMEM((1, 1), F32), pltpu.VMEM((1, 1), F32), pltpu.VMEM((1, DH_B), F32)]),
        compiler_params=_params("parallel", "parallel", "parallel", "arbitrary"),
        name="moba_sample",
    )(picks_flat, page_table_flat, slopes, qn, kn, z,
      *([cache_k] * PAGES_PER_BLOCK), *([cache_v] * PAGES_PER_BLOCK))
    return out.reshape(b, n_tok, D_B)


def _mem_knorm_kernel(x_ref, g_ref, o_ref):
    g = g_ref[...]
    for h in range(H_C):
        sl = slice(h * DH_C, (h + 1) * DH_C)
        o_ref[:, sl] = _rms(x_ref[:, sl], g)


def _mem_knorm(kv, g_kc, layer):
    b, n, _ = kv.shape
    return pl.pallas_call(
        _mem_knorm_kernel,
        out_shape=jax.ShapeDtypeStruct((b, n, D_C), F32),
        grid=(b,),
        in_specs=[pl.BlockSpec((None, n, D_C), lambda bi: (bi, 0, 0)),
                  pl.BlockSpec((None, 1, DH_C), lambda bi: (layer, 0, 0))],
        out_specs=pl.BlockSpec((None, n, D_C), lambda bi: (bi, 0, 0)),
        compiler_params=_params("parallel"),
        name="mem_knorm",
    )(kv, g_kc)


def _mem_attn_kernel(q_ref, gq_ref, k_ref, v_ref, o_ref):
    qn = _rms(q_ref[...], gq_ref[...])
    s = lax.dot_general(qn.astype(BF16), k_ref[...].astype(BF16), _NT, preferred_element_type=F32)
    s = s * (DH_C ** -0.5)
    e = jnp.exp(s - jnp.max(s, axis=-1, keepdims=True))
    p = e / jnp.sum(e, axis=-1, keepdims=True)
    o_ref[...] = _dot(p.astype(BF16), v_ref[...].astype(BF16)).astype(o_ref.dtype)


def _mem_attn(z, g_qc, mem_k, mem_v, layer, *, kv_index, v_col0, tq):
    b, l, _ = z.shape
    tq = _tile(l, tq)
    k_block = (None,) * (mem_k.ndim - 2) + (N_MEM, DH_C)
    v_block = (None,) * (mem_v.ndim - 2) + (N_MEM, DH_C)
    return pl.pallas_call(
        _mem_attn_kernel,
        out_shape=jax.ShapeDtypeStruct((b, l, D_C), BF16),
        grid=(b, H_C, l // tq),
        in_specs=[pl.BlockSpec((None, tq, DH_C), lambda bi, h, i: (bi, i, OFF_QC // DH_C + h)),
                  pl.BlockSpec((None, 1, DH_C), lambda bi, h, i: (layer, 0, 0)),
                  pl.BlockSpec(k_block, lambda bi, h, i: kv_index(bi, h, 0)),
                  pl.BlockSpec(v_block, lambda bi, h, i: kv_index(bi, h, v_col0))],
        out_specs=pl.BlockSpec((None, tq, DH_C), lambda bi, h, i: (bi, i, h)),
        compiler_params=_params("parallel", "parallel", "arbitrary"),
        name="mem_attn",
    )(z, g_qc, mem_k, mem_v)


def _mixer_and_ffn(x, layer, w, branches):
    b, l, d = x.shape
    m = b * l
    x2 = x.reshape(m, d)
    h = _rmsnorm(x2, w["norm_mix"], layer)
    z2 = _mm(h, w["w_in"], layer, tm=1024, tn=1024)
    o_a, o_b, o_c, extras = branches(z2.reshape(b, l, D_IN))
    merged = _mm_merge(o_a, o_b, o_c, z2, w["w_br_a"], w["w_br_b"], w["w_br_c"], layer, tm=1024, tn=512)
    x2 = _mm_residual(merged, w["w_out"], layer, x2, tm=1024, tn=512)
    h = _rmsnorm(x2, w["norm_ffn"], layer)
    act = _mm_swiglu(h, w["w_gate"], w["w_up"], layer, tm=1024, tn=512)
    x2 = _mm_residual(act, w["w_down"], layer, x2, tm=512, tn=512)
    return x2.reshape(b, l, d), z2.reshape(b, l, D_IN), extras


def kernel(x_prompt, x_sample, state_hgrn, cache_k, cache_v, cache_mem_k, cache_mem_v, page_table,
           mem_prompt, lb_logits, norm_mix, w_in, norm_o_a, norm_q_b, norm_k_b, norm_q_c, norm_k_c,
           norm_mem, w_mem_kv, w_br_a, w_br_b, w_br_c, w_out, norm_ffn, w_gate, w_up, w_down):
    depth = w_in.shape[0]
    n_prompt, seq, _ = x_prompt.shape
    n_seq, n_tok, _ = x_sample.shape
    n_pages = page_table.shape[1]
    n_pool = cache_k.shape[1]
    assert seq % MOBA_BLOCK == 0 and n_pages % PAGES_PER_BLOCK == 0
    assert n_pages // PAGES_PER_BLOCK >= MOBA_TOPK and n_tok <= SAMPLE_ROWS <= MOBA_BLOCK

    row = lambda g: g.reshape(depth, 1, g.shape[-1])
    w = {"norm_mix": row(norm_mix), "norm_ffn": row(norm_ffn),
         "w_in": w_in.astype(BF16), "w_br_a": w_br_a.astype(BF16), "w_br_b": w_br_b.astype(BF16),
         "w_br_c": w_br_c.astype(BF16), "w_out": w_out.astype(BF16), "w_gate": w_gate.astype(BF16),
         "w_up": w_up.astype(BF16), "w_down": w_down.astype(BF16)}
    w_mem_kv_b = w_mem_kv.astype(BF16)
    g_oa, g_qb, g_kb, g_qc, g_kc, g_mem = (row(g) for g in (norm_o_a, norm_q_b, norm_k_b, norm_q_c,
                                                           norm_k_c, norm_mem))
    lower = _lower_bounds(lb_logits).reshape(depth, 1, D_A)
    slopes = 2.0 ** (-8.0 * jnp.arange(1, H_B + 1, dtype=F32) / H_B)

    ck = cache_k.reshape(depth, n_pool, PAGE_SIZE, D_B)
    cv = cache_v.reshape(depth, n_pool, PAGE_SIZE, D_B)
    cmk = cache_mem_k.reshape(depth, n_seq, N_MEM, D_C)
    cmv = cache_mem_v.reshape(depth, n_seq, N_MEM, D_C)
    pt_flat = page_table.reshape(-1)
    s0_prompt = jnp.zeros((1, n_prompt, H_A, DK_A, DV_A), F32)
    mem2 = mem_prompt.reshape(n_prompt * N_MEM, D_MODEL)

    xp = x_prompt
    xs = jnp.pad(x_sample, ((0, 0), (0, SAMPLE_ROWS - n_tok), (0, 0)))
    outs = {k: [] for k in ("hp", "kp", "vp", "mkp", "mvp", "hs", "ks", "vs")}

    for layer in range(depth):
        kv = _mm(_rmsnorm(mem2, g_mem, layer), w_mem_kv_b, layer, tm=1024, tn=1024)
        kv = kv.reshape(n_prompt, N_MEM, 2 * D_C)
        mem_k = _mem_knorm(kv, g_kc, layer)
        mem_v = kv[:, :, D_C:]

        def prompt_branches(z):
            o_a, s_new = _hgrn2(z, lower, g_oa, s0_prompt, layer, 0, chunk=64)
            kn, kb, vb, kmean = _moba_kv(z, g_kb, layer, rows=MOBA_BLOCK)
            o_b = _moba_prompt(z, g_qb, kb, vb, kmean.reshape(n_prompt, -1, D_B), slopes, layer)
            o_c = _mem_attn(z, g_qc, mem_k, kv, layer, tq=512, v_col0=D_C // DH_C,
                            kv_index=lambda bi, h, c0: (bi, 0, c0 + h))
            m = n_prompt * seq
            return o_a.reshape(m, D_A), o_b.reshape(m, D_B), o_c.reshape(m, D_C), (s_new, kn)

        def sample_branches(z):
            o_a, s_new = _hgrn2(z, lower, g_oa, state_hgrn, layer, layer, chunk=SAMPLE_ROWS, valid_len=n_tok)
            kn, _, _, _ = _moba_kv(z, g_kb, layer, rows=SAMPLE_ROWS)
            kmean = _page_means(ck, pt_flat, layer, n_seq, n_pages)
            qn, picks = _moba_select(z, g_qb, kmean, layer)
            picks_flat = picks[:, :, :n_tok, :MOBA_TOPK].reshape(-1)
            o_b = _moba_sample(qn, kn, z, ck, cv, picks_flat, pt_flat, slopes, layer,
                               n_tok=n_tok, n_pages=n_pages)
            o_b = jnp.pad(o_b, ((0, 0), (0, SAMPLE_ROWS - n_tok), (0, 0))).astype(BF16)
            o_c = _mem_attn(z, g_qc, cmk, cmv, layer, tq=SAMPLE_ROWS, v_col0=0,
                            kv_index=lambda bi, h, c0: (layer, bi, 0, c0 + h))
            m = n_seq * SAMPLE_ROWS
            return o_a.reshape(m, D_A), o_b.reshape(m, D_B), o_c.reshape(m, D_C), (s_new, kn)

        xp, zp, (s_p, k_p) = _mixer_and_ffn(xp, layer, w, prompt_branches)
        xs, zs, (s_s, k_s) = _mixer_and_ffn(xs, layer, w, sample_branches)

        outs["hp"].append(s_p)
        outs["kp"].append(k_p.reshape(n_prompt, seq, H_B, DH_B))
        outs["vp"].append(zp[:, :, OFF_VB:OFF_VB + D_B].reshape(n_prompt, seq, H_B, DH_B))
        outs["mkp"].append(mem_k.reshape(n_prompt, N_MEM, H_C, DH_C))
        outs["mvp"].append(mem_v.reshape(n_prompt, N_MEM, H_C, DH_C))
        outs["hs"].append(s_s)
        outs["ks"].append(k_s[:, :n_tok].reshape(n_seq, n_tok, H_B, DH_B))
        outs["vs"].append(zs[:, :n_tok, OFF_VB:OFF_VB + D_B].reshape(n_seq, n_tok, H_B, DH_B))

    stack = lambda k: jnp.stack(outs[k])
    return (xp, xs[:, :n_tok], stack("hp"), stack("kp"), stack("vp"), stack("mkp"), stack("mvp"),
            stack("hs"), stack("ks"), stack("vs"))
```

```python
import functools

import jax
import jax.numpy as jnp
from jax import lax
from jax.experimental import pallas as pl
from jax.experimental.pallas import tpu as pltpu

D_MODEL = 2048
H_A, DK_A, DV_A = 8, 128, 128
D_A = H_A * DK_A
H_B, DH_B = 8, 128
D_B = H_B * DH_B
MOBA_BLOCK = 256
MOBA_TOPK = 3
PAGE_SIZE = 128
PAGES_PER_BLOCK = MOBA_BLOCK // PAGE_SIZE
N_MEM = 256
H_C, DH_C = 4, 256
D_C = H_C * DH_C
N_BRANCH = 3
D_IN = 4 * D_A + 3 * D_B + D_C + N_BRANCH * D_MODEL
EPS = 1e-6

OFF_QA, OFF_FA, OFF_IA, OFF_GA = 0, D_A, 2 * D_A, 3 * D_A
OFF_QB = 4 * D_A
OFF_KB = OFF_QB + D_B
OFF_VB = OFF_KB + D_B
OFF_QC = OFF_VB + D_B
OFF_GATE = OFF_QC + D_C

GLA_SUB = 16
GLA_SAFE_DROP = 80.0
SAMPLE_ROWS = 16
PAGE_MEAN_BLOCKS = 4
V7X_VMEM_LIMIT = 56 * 1024 * 1024

F32 = jnp.float32
BF16 = jnp.bfloat16
_NT = (((1,), (1,)), ((), ()))
_TN = (((0,), (0,)), ((), ()))


def _params(*semantics):
    return pltpu.CompilerParams(dimension_semantics=semantics, vmem_limit_bytes=V7X_VMEM_LIMIT)


def _tile(n, pref):
    t = min(n, pref)
    assert n % t == 0, (n, pref)
    return t


def _rms(x, gain):
    return x * lax.rsqrt(jnp.mean(x * x, axis=-1, keepdims=True) + EPS) * gain


def _rmsnorm_kernel(x_ref, g_ref, o_ref):
    o_ref[...] = _rms(x_ref[...], g_ref[...]).astype(o_ref.dtype)


def _rmsnorm(x, gains, layer):
    m, d = x.shape
    tm = _tile(m, 512)
    return pl.pallas_call(
        _rmsnorm_kernel,
        out_shape=jax.ShapeDtypeStruct((m, d), BF16),
        grid=(m // tm,),
        in_specs=[pl.BlockSpec((tm, d), lambda i: (i, 0)),
                  pl.BlockSpec((None, 1, d), lambda i: (layer, 0, 0))],
        out_specs=pl.BlockSpec((tm, d), lambda i: (i, 0)),
        compiler_params=_params("parallel"),
        name="rmsnorm",
    )(x, gains)


def _dot(a, b):
    return jnp.dot(a, b, preferred_element_type=F32)


def _mm_kernel(a_ref, w_ref, o_ref):
    o_ref[...] = _dot(a_ref[...], w_ref[...]).astype(o_ref.dtype)


def _mm(a, w, layer, *, tm, tn, out_dtype=F32):
    m, k = a.shape
    n = w.shape[-1]
    tm, tn = _tile(m, tm), _tile(n, tn)
    return pl.pallas_call(
        _mm_kernel,
        out_shape=jax.ShapeDtypeStruct((m, n), out_dtype),
        grid=(m // tm, n // tn),
        in_specs=[pl.BlockSpec((tm, k), lambda i, j: (i, 0)),
                  pl.BlockSpec((None, k, tn), lambda i, j: (layer, 0, j))],
        out_specs=pl.BlockSpec((tm, tn), lambda i, j: (i, j)),
        compiler_params=_params("parallel", "arbitrary"),
        name="mm",
    )(a, w)


def _mm_residual_kernel(a_ref, w_ref, r_ref, o_ref):
    o_ref[...] = r_ref[...] + _dot(a_ref[...], w_ref[...])


def _mm_residual(a, w, layer, res, *, tm, tn):
    m, k = a.shape
    n = w.shape[-1]
    tm, tn = _tile(m, tm), _tile(n, tn)
    return pl.pallas_call(
        _mm_residual_kernel,
        out_shape=jax.ShapeDtypeStruct((m, n), F32),
        grid=(m // tm, n // tn),
        in_specs=[pl.BlockSpec((tm, k), lambda i, j: (i, 0)),
                  pl.BlockSpec((None, k, tn), lambda i, j: (layer, 0, j)),
                  pl.BlockSpec((tm, tn), lambda i, j: (i, j))],
        out_specs=pl.BlockSpec((tm, tn), lambda i, j: (i, j)),
        compiler_params=_params("parallel", "arbitrary"),
        name="mm_residual",
    )(a, w, res)


def _mm_swiglu_kernel(a_ref, wg_ref, wu_ref, o_ref):
    a = a_ref[...]
    g = _dot(a, wg_ref[...])
    u = _dot(a, wu_ref[...])
    o_ref[...] = (g * jax.nn.sigmoid(g) * u).astype(o_ref.dtype)


def _mm_swiglu(a, wg, wu, layer, *, tm, tn):
    m, k = a.shape
    n = wg.shape[-1]
    tm, tn = _tile(m, tm), _tile(n, tn)
    wspec = pl.BlockSpec((None, k, tn), lambda i, j: (layer, 0, j))
    return pl.pallas_call(
        _mm_swiglu_kernel,
        out_shape=jax.ShapeDtypeStruct((m, n), BF16),
        grid=(m // tm, n // tn),
        in_specs=[pl.BlockSpec((tm, k), lambda i, j: (i, 0)), wspec, wspec],
        out_specs=pl.BlockSpec((tm, tn), lambda i, j: (i, j)),
        compiler_params=_params("parallel", "arbitrary"),
        name="mm_swiglu",
    )(a, wg, wu)


def _mm_merge_kernel(oa_ref, ob_ref, oc_ref, g0_ref, g1_ref, g2_ref, wa_ref, wb_ref, wc_ref, o_ref):
    merged = (jax.nn.sigmoid(g0_ref[...]) * _dot(oa_ref[...], wa_ref[...])
              + jax.nn.sigmoid(g1_ref[...]) * _dot(ob_ref[...], wb_ref[...])
              + jax.nn.sigmoid(g2_ref[...]) * _dot(oc_ref[...], wc_ref[...]))
    o_ref[...] = merged.astype(o_ref.dtype)


def _mm_merge(o_a, o_b, o_c, z, w_a, w_b, w_c, layer, *, tm, tn):
    m, k = o_a.shape
    n = w_a.shape[-1]
    tm, tn = _tile(m, tm), _tile(n, tn)
    aspec = pl.BlockSpec((tm, k), lambda i, j: (i, 0))
    wspec = pl.BlockSpec((None, k, tn), lambda i, j: (layer, 0, j))

    def gate_spec(branch):
        first = (OFF_GATE + branch * D_MODEL) // tn
        return pl.BlockSpec((tm, tn), lambda i, j: (i, first + j))

    return pl.pallas_call(
        _mm_merge_kernel,
        out_shape=jax.ShapeDtypeStruct((m, n), BF16),
        grid=(m // tm, n // tn),
        in_specs=[aspec, aspec, aspec, gate_spec(0), gate_spec(1), gate_spec(2), wspec, wspec, wspec],
        out_specs=pl.BlockSpec((tm, tn), lambda i, j: (i, j)),
        compiler_params=_params("parallel", "arbitrary"),
        name="mm_merge",
    )(o_a, o_b, o_c, z, z, z, w_a, w_b, w_c)


def _lower_bounds_kernel(x_ref, o_ref):
    x = x_ref[...]
    e = jnp.exp(x - jnp.max(x, axis=0, keepdims=True))
    p = e / jnp.sum(e, axis=0, keepdims=True)
    rows = [jnp.zeros_like(p[0:1])]
    for r in range(1, x.shape[0]):
        rows.append(rows[-1] + p[r:r + 1])
    o_ref[...] = jnp.concatenate(rows, axis=0)


def _lower_bounds(lb_logits):
    return pl.pallas_call(
        _lower_bounds_kernel,
        out_shape=jax.ShapeDtypeStruct(lb_logits.shape, F32),
        name="hgrn2_lower_bounds",
    )(lb_logits)


def _hgrn2_intra_factored(q, kk, v, G, g_mid):
    qt = (q * jnp.exp(G - g_mid)).astype(BF16)
    kt = (kk * jnp.exp(g_mid - G)).astype(BF16)
    a = lax.dot_general(qt, kt, _NT, preferred_element_type=F32)
    causal = lax.broadcasted_iota(jnp.int32, a.shape, 1) <= lax.broadcasted_iota(jnp.int32, a.shape, 0)
    return _dot(jnp.where(causal, a, 0.0).astype(BF16), v.astype(BF16))


def _hgrn2_intra_exact(q, kk, v, G, rmod, chunk):
    pieces = [jnp.zeros((GLA_SUB, DV_A), F32)]
    for i in range(1, chunk // GLA_SUB):
        lo = i * GLA_SUB
        r = G[lo - 1:lo]
        qi = (q[lo:lo + GLA_SUB] * jnp.exp(G[lo:lo + GLA_SUB] - r)).astype(BF16)
        kj = (kk[:lo] * jnp.exp(r - G[:lo])).astype(BF16)
        a = lax.dot_general(qi, kj, _NT, preferred_element_type=F32)
        pieces.append(_dot(a.astype(BF16), v[:lo].astype(BF16)))
    o = jnp.concatenate(pieces, axis=0)
    for d in range(GLA_SUB):
        k_d = kk if d == 0 else pltpu.roll(kk, d, axis=0)
        g_d = G if d == 0 else pltpu.roll(G, d, axis=0)
        v_d = v if d == 0 else pltpu.roll(v, d, axis=0)
        decay = jnp.exp(jnp.where(rmod >= d, G - g_d, -jnp.inf))
        a_d = jnp.sum(q * k_d * decay, axis=-1, keepdims=True)
        o = o + a_d * v_d
    return o


def _hgrn2_kernel(q_ref, f_ref, i_ref, g_ref, lb_ref, gn_ref, s0_ref, o_ref, sout_ref, st_ref,
                  *, chunk, valid_len):
    c = pl.program_id(1)

    @pl.when(c == 0)
    def _():
        for h in range(H_A):
            st_ref[h] = s0_ref[h].T

    zf = f_ref[...]
    lb = lb_ref[...]
    log_sig = jnp.minimum(zf, 0.0) - jnp.log1p(jnp.exp(-jnp.abs(zf)))
    la = jnp.log(lb)
    lc = jnp.log1p(-lb) + log_sig
    log_f = jnp.maximum(la, lc) + jnp.log1p(jnp.exp(-jnp.abs(la - lc)))
    kk = (1.0 - lb) * jax.nn.sigmoid(-zf)
    qr = q_ref[...]
    q = qr * jax.nn.sigmoid(qr) * (DK_A ** -0.5)

    row = lax.broadcasted_iota(jnp.int32, (chunk, 1), 0)
    if valid_len is not None:
        live = (c * chunk + row) < valid_len
        log_f = jnp.where(live, log_f, 0.0)
        kk = jnp.where(live, kk, 0.0)

    tri = (lax.broadcasted_iota(jnp.int32, (chunk, chunk), 0)
           >= lax.broadcasted_iota(jnp.int32, (chunk, chunk), 1)).astype(F32)
    G = jnp.dot(tri, log_f, precision=lax.Precision.HIGHEST, preferred_element_type=F32)

    v = i_ref[...]
    rmod = row % GLA_SUB
    heads = [slice(h * DK_A, (h + 1) * DK_A) for h in range(H_A)]

    g_mid = G[chunk // 2 - 1:chunk // 2]
    g_last = G[chunk - 1:chunk]

    def intra_factored():
        return jnp.concatenate(
            [_hgrn2_intra_factored(q[:, sl], kk[:, sl], v[:, sl], G[:, sl], g_mid[:, sl]) for sl in heads], axis=1)

    def intra_exact():
        return jnp.concatenate(
            [_hgrn2_intra_exact(q[:, sl], kk[:, sl], v[:, sl], G[:, sl], rmod, chunk) for sl in heads], axis=1)

    worst_drop = jnp.max(jnp.maximum(-g_mid, g_mid - g_last))
    o_intra = lax.cond(worst_drop < GLA_SAFE_DROP, intra_factored, intra_exact)

    gr = g_ref[...]
    out_gate = gr * jax.nn.sigmoid(gr)
    gn = gn_ref[...]
    q_in = (q * jnp.exp(G)).astype(BF16)
    k_out = (kk * jnp.exp(g_last - G)).astype(BF16)
    st_decay = jnp.exp(g_last)
    last = c == pl.num_programs(1) - 1
    for h, sl in enumerate(heads):
        st = st_ref[h]
        o = o_intra[:, sl] + lax.dot_general(q_in[:, sl], st.astype(BF16), _NT, preferred_element_type=F32)
        st_new = st * st_decay[:, sl] + lax.dot_general(v[:, sl].astype(BF16), k_out[:, sl], _TN,
                                                         preferred_element_type=F32)
        st_ref[h] = st_new

        @pl.when(last)
        def _():
            sout_ref[h] = st_new.T

        o_ref[:, sl] = (_rms(o, gn) * out_gate[:, sl]).astype(o_ref.dtype)


def _hgrn2(z, lower, g_onorm, s0, layer, s0_layer, *, chunk, valid_len=None):
    b, l, _ = z.shape
    chunk = _tile(l, chunk)

    def col_spec(off):
        return pl.BlockSpec((None, chunk, D_A), lambda bi, c: (bi, c, off // D_A))

    return pl.pallas_call(
        functools.partial(_hgrn2_kernel, chunk=chunk, valid_len=valid_len),
        out_shape=(jax.ShapeDtypeStruct((b, l, D_A), BF16),
                   jax.ShapeDtypeStruct((b, H_A, DK_A, DV_A), F32)),
        grid=(b, l // chunk),
        in_specs=[col_spec(OFF_QA), col_spec(OFF_FA), col_spec(OFF_IA), col_spec(OFF_GA),
                  pl.BlockSpec((None, 1, D_A), lambda bi, c: (layer, 0, 0)),
                  pl.BlockSpec((None, 1, DV_A), lambda bi, c: (layer, 0, 0)),
                  pl.BlockSpec((None, None, H_A, DK_A, DV_A), lambda bi, c: (s0_layer, bi, 0, 0, 0))],
        out_specs=(pl.BlockSpec((None, chunk, D_A), lambda bi, c: (bi, c, 0)),
                   pl.BlockSpec((None, H_A, DK_A, DV_A), lambda bi, c: (bi, 0, 0, 0))),
        scratch_shapes=[pltpu.VMEM((H_A, DV_A, DK_A), F32)],
        compiler_params=_params("parallel", "arbitrary"),
        name="hgrn2",
    )(z, z, z, z, lower, g_onorm, s0)


def _head_norm_kernel(x_ref, g_ref, o_ref, *, heads, hd):
    g = g_ref[...]
    for h in range(heads):
        sl = slice(h * hd, (h + 1) * hd)
        o_ref[:, sl] = _rms(x_ref[:, sl], g)


def _head_norm(x, gains, layer, *, col_block, heads, hd):
    b, r, _ = x.shape
    width = heads * hd
    return pl.pallas_call(
        functools.partial(_head_norm_kernel, heads=heads, hd=hd),
        out_shape=jax.ShapeDtypeStruct((b, r, width), F32),
        grid=(b,),
        in_specs=[pl.BlockSpec((None, r, width), lambda bi: (bi, 0, col_block)),
                  pl.BlockSpec((None, 1, hd), lambda bi: (layer, 0, 0))],
        out_specs=pl.BlockSpec((None, r, width), lambda bi: (bi, 0, 0)),
        compiler_params=_params("parallel"),
        name="head_norm",
    )(x, gains)


def _moba_kv_kernel(k_ref, v_ref, g_ref, kn_ref, kb_ref, vt_ref, km_ref):
    g = g_ref[...]
    for h in range(H_B):
        sl = slice(h * DH_B, (h + 1) * DH_B)
        kn = _rms(k_ref[:, sl], g)
        kn_ref[:, sl] = kn
        kb_ref[:, sl] = kn.astype(BF16)
        km_ref[:, sl] = jnp.mean(kn, axis=0, keepdims=True)
        vt_ref[sl, :] = v_ref[:, sl].T.astype(BF16)


def _moba_kv(z, g_kb, layer):
    b, l, _ = z.shape
    nblk = l // MOBA_BLOCK
    act = lambda dtype: jax.ShapeDtypeStruct((b, l, D_B), dtype)
    act_spec = pl.BlockSpec((None, MOBA_BLOCK, D_B), lambda bi, n: (bi, n, 0))
    kn, kb, vt, km = pl.pallas_call(
        _moba_kv_kernel,
        out_shape=(act(F32), act(BF16), jax.ShapeDtypeStruct((b, D_B, l), BF16),
                   jax.ShapeDtypeStruct((b, nblk, 1, D_B), F32)),
        grid=(b, nblk),
        in_specs=[pl.BlockSpec((None, MOBA_BLOCK, D_B), lambda bi, n: (bi, n, OFF_KB // D_B)),
                  pl.BlockSpec((None, MOBA_BLOCK, D_B), lambda bi, n: (bi, n, OFF_VB // D_B)),
                  pl.BlockSpec((None, 1, DH_B), lambda bi, n: (layer, 0, 0))],
        out_specs=(act_spec, act_spec,
                   pl.BlockSpec((None, D_B, MOBA_BLOCK), lambda bi, n: (bi, 0, n)),
                   pl.BlockSpec((None, None, 1, D_B), lambda bi, n: (bi, n, 0, 0))),
        compiler_params=_params("parallel", "parallel"),
        name="moba_kv",
    )(z, z, g_kb)
    return kn, kb, vt, km.reshape(b, nblk, D_B)


def _topk_mask_t(gate_t, n_valid, topk):
    n, r = gate_t.shape
    rowi = lax.broadcasted_iota(jnp.int32, (n, r), 0)
    gm = jnp.where(rowi < n_valid, gate_t, -jnp.inf)
    sel = jnp.zeros((n, r), F32)
    for j in range(n):
        gj = gm[j:j + 1]
        beats = jnp.where(gm > gj, 1.0, jnp.where((gm == gj) & (rowi < j), 1.0, 0.0))
        rank = jnp.sum(beats, axis=0, keepdims=True)
        sel = jnp.where((rowi == j) & (rank < topk) & (rowi < n_valid), 1.0, sel)
    return sel


def _moba_prompt_kernel(slopes_ref, q_ref, gq_ref, k_ref, vt_ref, km_ref, o_ref, s_ref):
    h = pl.program_id(1)
    i = pl.program_id(2)
    blk = MOBA_BLOCK
    slope = slopes_ref[h]
    scale = DH_B ** -0.5

    qn = _rms(q_ref[...], gq_ref[...])
    gate_t = lax.dot_general(km_ref[...], qn, _NT, precision=lax.Precision.HIGHEST,
                             preferred_element_type=F32)
    sel = _topk_mask_t(gate_t, i, MOBA_TOPK)
    qb = qn.astype(BF16)
    d0 = (lax.broadcasted_iota(jnp.int32, (blk, blk), 1)
          - lax.broadcasted_iota(jnp.int32, (blk, blk), 0)).astype(F32)

    def attend(own):
        m = None
        for j in range(own + 1):
            rows = slice(j * blk, (j + 1) * blk)
            s = lax.dot_general(k_ref[rows, :], qb, _NT, preferred_element_type=F32) * scale
            s = s - slope * (d0 + float((own - j) * blk))
            keep = (d0 >= 0.0) if j == own else (sel[j:j + 1] > 0.0)
            s = jnp.where(keep, s, -jnp.inf)
            s_ref[rows, :] = s
            m_j = jnp.max(s, axis=0, keepdims=True)
            m = m_j if m is None else jnp.maximum(m, m_j)
        l = jnp.zeros((1, blk), F32)
        acc = jnp.zeros((DH_B, blk), F32)
        for j in range(own + 1):
            rows = slice(j * blk, (j + 1) * blk)
            p = jnp.exp(s_ref[rows, :] - m)
            l = l + jnp.sum(p, axis=0, keepdims=True)
            acc = acc + _dot(vt_ref[:, rows], p.astype(BF16))
        o_ref[...] = (acc / l).T.astype(o_ref.dtype)

    for own in range(km_ref.shape[0]):
        pl.when(i == own)(functools.partial(attend, own))


def _moba_prompt(z, g_qb, kb, vt, kmean, slopes, layer):
    b, l, _ = z.shape
    nblk = l // MOBA_BLOCK
    return pl.pallas_call(
        _moba_prompt_kernel,
        out_shape=jax.ShapeDtypeStruct((b, l, D_B), BF16),
        grid=(b, H_B, nblk),
        in_specs=[pl.BlockSpec(memory_space=pltpu.SMEM),
                  pl.BlockSpec((None, MOBA_BLOCK, DH_B), lambda bi, h, i: (bi, i, OFF_QB // DH_B + h)),
                  pl.BlockSpec((None, 1, DH_B), lambda bi, h, i: (layer, 0, 0)),
                  pl.BlockSpec((None, l, DH_B), lambda bi, h, i: (bi, 0, h)),
                  pl.BlockSpec((None, DH_B, l), lambda bi, h, i: (bi, h, 0)),
                  pl.BlockSpec((None, nblk, DH_B), lambda bi, h, i: (bi, 0, h))],
        out_specs=pl.BlockSpec((None, MOBA_BLOCK, DH_B), lambda bi, h, i: (bi, i, h)),
        scratch_shapes=[pltpu.VMEM((l, MOBA_BLOCK), F32)],
        compiler_params=_params("parallel", "parallel", "arbitrary"),
        name="moba_prompt",
    )(slopes, z, g_qb, kb, vt, kmean)


def _page_mean_kernel(pt_ref, *refs):
    del pt_ref
    page_refs, o_ref = refs[:-1], refs[-1]
    for n in range(PAGE_MEAN_BLOCKS):
        pages = page_refs[n * PAGES_PER_BLOCK:(n + 1) * PAGES_PER_BLOCK]
        total = jnp.sum(pages[0][...], axis=0)
        for r in pages[1:]:
            total = total + jnp.sum(r[...], axis=0)
        o_ref[n] = total * (1.0 / MOBA_BLOCK)


def _page_means(cache_k, page_table_flat, layer, n_seq, n_pages):
    n_blocks = n_pages // PAGES_PER_BLOCK
    assert n_blocks % PAGE_MEAN_BLOCKS == 0
    pages_per_step = PAGE_MEAN_BLOCKS * PAGES_PER_BLOCK

    def page_spec(p):
        return pl.BlockSpec(
            (None, None, PAGE_SIZE, H_B, DH_B),
            lambda bi, n, pt: (layer, pt[bi * n_pages + n * pages_per_step + p], 0, 0, 0))

    return pl.pallas_call(
        _page_mean_kernel,
        out_shape=jax.ShapeDtypeStruct((n_seq, n_blocks, H_B, DH_B), F32),
        grid_spec=pltpu.PrefetchScalarGridSpec(
            num_scalar_prefetch=1,
            grid=(n_seq, n_blocks // PAGE_MEAN_BLOCKS),
            in_specs=[page_spec(p) for p in range(pages_per_step)],
            out_specs=pl.BlockSpec((None, PAGE_MEAN_BLOCKS, H_B, DH_B), lambda bi, n, pt: (bi, n, 0, 0))),
        compiler_params=_params("parallel", "parallel"),
        name="moba_page_means",
    )(page_table_flat, *([cache_k] * pages_per_step))


def _moba_select_kernel(q_ref, gq_ref, km_ref, qn_ref, sel_ref):
    rows = q_ref.shape[0]
    n_blocks = km_ref.shape[0]
    g = gq_ref[...]
    col = lax.broadcasted_iota(jnp.int32, (rows, n_blocks), 1)
    lane = lax.broadcasted_iota(jnp.int32, (rows, sel_ref.shape[-1]), 1)
    for h in range(H_B):
        sl = slice(h * DH_B, (h + 1) * DH_B)
        qn = _rms(q_ref[:, sl], g)
        qn_ref[:, sl] = qn
        gate = lax.dot_general(qn, km_ref[:, sl], _NT, precision=lax.Precision.HIGHEST,
                               preferred_element_type=F32)
        picks = jnp.zeros(lane.shape, jnp.int32)
        for r in range(MOBA_TOPK):
            best = jnp.max(gate, axis=-1, keepdims=True)
            idx = jnp.min(jnp.where(gate == best, col, n_blocks), axis=-1, keepdims=True)
            picks = jnp.where(lane == r, idx, picks)
            gate = jnp.where(col == idx, -jnp.inf, gate)
        sel_ref[h] = picks


def _moba_select(z, g_qb, kmean, layer):
    b, rows, _ = z.shape
    n_blocks = kmean.shape[1]
    return pl.pallas_call(
        _moba_select_kernel,
        out_shape=(jax.ShapeDtypeStruct((b, rows, D_B), F32),
                   jax.ShapeDtypeStruct((b, H_B, rows, 128), jnp.int32)),
        grid=(b,),
        in_specs=[pl.BlockSpec((None, rows, D_B), lambda bi: (bi, 0, OFF_QB // D_B)),
                  pl.BlockSpec((None, 1, DH_B), lambda bi: (layer, 0, 0)),
                  pl.BlockSpec((None, n_blocks, D_B), lambda bi: (bi, 0, 0))],
        out_specs=(pl.BlockSpec((None, rows, D_B), lambda bi: (bi, 0, 0)),
                   pl.BlockSpec((None, H_B, rows, 128), lambda bi: (bi, 0, 0, 0))),
        compiler_params=_params("parallel"),
        name="moba_select",
    )(z, g_qb, kmean)


def _moba_sample_kernel(sel_ref, pt_ref, slopes_ref, q_ref, kown_ref, vown_ref, ck_ref, cv_ref, o_ref,
                        kbuf, vbuf, sem, *, layer, n_tok, n_pages):
    step = pl.program_id(0)
    n_steps = pl.num_programs(0)
    tiles_per_tok = MOBA_TOPK * PAGES_PER_BLOCK
    past_len = n_pages * PAGE_SIZE

    def picked_block(st, t, s):
        return sel_ref[(st * n_tok + t) * MOBA_TOPK + s]

    def tile_copies(st, slot):
        bi, h = st // H_B, st % H_B
        copies = []
        for t in range(n_tok):
            for s in range(MOBA_TOPK):
                block = picked_block(st, t, s)
                for p in range(PAGES_PER_BLOCK):
                    page = pt_ref[bi * n_pages + block * PAGES_PER_BLOCK + p]
                    j = t * tiles_per_tok + s * PAGES_PER_BLOCK + p
                    copies.append(pltpu.make_async_copy(
                        ck_ref.at[layer, page, :, h, :], kbuf.at[slot, j], sem.at[slot]))
                    copies.append(pltpu.make_async_copy(
                        cv_ref.at[layer, page, :, h, :], vbuf.at[slot, j], sem.at[slot]))
        return copies

    slot = step % 2

    @pl.when(step == 0)
    def _():
        for c in tile_copies(step, slot):
            c.start()

    @pl.when(step + 1 < n_steps)
    def _():
        for c in tile_copies(step + 1, 1 - slot):
            c.start()

    for c in tile_copies(step, slot):
        c.wait()

    slope = slopes_ref[step % H_B]
    scale = DH_B ** -0.5
    rows = kown_ref.shape[0]
    k_own, v_own = kown_ref[...], vown_ref[...]
    own_pos = past_len + lax.broadcasted_iota(jnp.int32, (rows, 1), 0)
    page_row = lax.broadcasted_iota(jnp.int32, (PAGE_SIZE, 1), 0)
    out_row = lax.broadcasted_iota(jnp.int32, (rows, 1), 0)
    out = jnp.zeros((rows, DH_B), F32)
    for t in range(n_tok):
        q = q_ref[t:t + 1, :]
        t_pos = past_len + t

        def score(keys, k_pos):
            return (jnp.sum(keys * q, axis=-1, keepdims=True) * scale
                    - slope * (t_pos - k_pos).astype(F32))

        scores = [jnp.where(own_pos <= t_pos, score(k_own, own_pos), -jnp.inf)]
        values = [v_own]
        for s in range(MOBA_TOPK):
            block = picked_block(step, t, s)
            for p in range(PAGES_PER_BLOCK):
                j = t * tiles_per_tok + s * PAGES_PER_BLOCK + p
                scores.append(score(kbuf[slot, j], block * MOBA_BLOCK + p * PAGE_SIZE + page_row))
                values.append(vbuf[slot, j])
        m = functools.reduce(jnp.maximum, [jnp.max(s, axis=0, keepdims=True) for s in scores])
        probs = [jnp.exp(s - m) for s in scores]
        l = functools.reduce(jnp.add, [jnp.sum(p, axis=0, keepdims=True) for p in probs])
        acc = functools.reduce(jnp.add, [jnp.sum(p * v, axis=0, keepdims=True) for p, v in zip(probs, values)])
        out = jnp.where(out_row == t, acc / l, out)
    o_ref[...] = out.astype(o_ref.dtype)


def _moba_sample(qn, kn, z, cache_k, cache_v, picks_flat, page_table_flat, slopes, layer, *, n_tok, n_pages):
    b, rows, _ = qn.shape
    n_tiles = n_tok * MOBA_TOPK * PAGES_PER_BLOCK
    own_spec = pl.BlockSpec((None, rows, DH_B), lambda st, sel, pt: (st // H_B, 0, st % H_B))
    return pl.pallas_call(
        functools.partial(_moba_sample_kernel, layer=layer, n_tok=n_tok, n_pages=n_pages),
        out_shape=jax.ShapeDtypeStruct((b, rows, D_B), BF16),
        grid_spec=pltpu.PrefetchScalarGridSpec(
            num_scalar_prefetch=2,
            grid=(b * H_B,),
            in_specs=[pl.BlockSpec(memory_space=pltpu.SMEM), own_spec, own_spec,
                      pl.BlockSpec((None, rows, DH_B),
                                   lambda st, sel, pt: (st // H_B, 0, OFF_VB // DH_B + st % H_B)),
                      pl.BlockSpec(memory_space=pl.ANY), pl.BlockSpec(memory_space=pl.ANY)],
            out_specs=pl.BlockSpec((None, rows, DH_B), lambda st, sel, pt: (st // H_B, 0, st % H_B)),
            scratch_shapes=[pltpu.VMEM((2, n_tiles, PAGE_SIZE, DH_B), F32),
                            pltpu.VMEM((2, n_tiles, PAGE_SIZE, DH_B), F32),
                            pltpu.SemaphoreType.DMA((2,))]),
        compiler_params=_params("arbitrary"),
        name="moba_sample",
    )(picks_flat, page_table_flat, slopes, qn, kn, z, cache_k, cache_v)


def _mem_attn_kernel(q_ref, gq_ref, k_ref, v_ref, o_ref):
    qn = _rms(q_ref[...], gq_ref[...])
    s = lax.dot_general(qn.astype(BF16), k_ref[...].astype(BF16), _NT, preferred_element_type=F32)
    s = s * (DH_C ** -0.5)
    e = jnp.exp(s - jnp.max(s, axis=-1, keepdims=True))
    p = e / jnp.sum(e, axis=-1, keepdims=True)
    o_ref[...] = _dot(p.astype(BF16), v_ref[...].astype(BF16)).astype(o_ref.dtype)


def _mem_attn(z, g_qc, mem_k, mem_v, layer, *, kv_index, v_col0, tq):
    b, l, _ = z.shape
    tq = _tile(l, tq)
    k_block = (None,) * (mem_k.ndim - 2) + (N_MEM, DH_C)
    v_block = (None,) * (mem_v.ndim - 2) + (N_MEM, DH_C)
    return pl.pallas_call(
        _mem_attn_kernel,
        out_shape=jax.ShapeDtypeStruct((b, l, D_C), BF16),
        grid=(b, H_C, l // tq),
        in_specs=[pl.BlockSpec((None, tq, DH_C), lambda bi, h, i: (bi, i, OFF_QC // DH_C + h)),
                  pl.BlockSpec((None, 1, DH_C), lambda bi, h, i: (layer, 0, 0)),
                  pl.BlockSpec(k_block, lambda bi, h, i: kv_index(bi, h, 0)),
                  pl.BlockSpec(v_block, lambda bi, h, i: kv_index(bi, h, v_col0))],
        out_specs=pl.BlockSpec((None, tq, DH_C), lambda bi, h, i: (bi, i, h)),
        compiler_params=_params("parallel", "parallel", "arbitrary"),
        name="mem_attn",
    )(z, g_qc, mem_k, mem_v)


def _mixer_and_ffn(x, layer, w, branches):
    b, l, d = x.shape
    m = b * l
    x2 = x.reshape(m, d)
    h = _rmsnorm(x2, w["norm_mix"], layer)
    z2 = _mm(h, w["w_in"], layer, tm=1024, tn=1024)
    o_a, o_b, o_c, extras = branches(z2.reshape(b, l, D_IN))
    merged = _mm_merge(o_a, o_b, o_c, z2, w["w_br_a"], w["w_br_b"], w["w_br_c"], layer, tm=1024, tn=512)
    x2 = _mm_residual(merged, w["w_out"], layer, x2, tm=1024, tn=512)
    h = _rmsnorm(x2, w["norm_ffn"], layer)
    act = _mm_swiglu(h, w["w_gate"], w["w_up"], layer, tm=1024, tn=512)
    x2 = _mm_residual(act, w["w_down"], layer, x2, tm=512, tn=512)
    return x2.reshape(b, l, d), z2.reshape(b, l, D_IN), extras


def kernel(x_prompt, x_sample, state_hgrn, cache_k, cache_v, cache_mem_k, cache_mem_v, page_table,
           mem_prompt, lb_logits, norm_mix, w_in, norm_o_a, norm_q_b, norm_k_b, norm_q_c, norm_k_c,
           norm_mem, w_mem_kv, w_br_a, w_br_b, w_br_c, w_out, norm_ffn, w_gate, w_up, w_down):
    depth = w_in.shape[0]
    n_prompt, seq, _ = x_prompt.shape
    n_seq, n_tok, _ = x_sample.shape
    n_pages = page_table.shape[1]
    assert seq % MOBA_BLOCK == 0 and n_pages % PAGES_PER_BLOCK == 0
    assert n_pages // PAGES_PER_BLOCK >= MOBA_TOPK and n_tok <= SAMPLE_ROWS <= MOBA_BLOCK

    row = lambda g: g.reshape(depth, 1, g.shape[-1])
    w = {"norm_mix": row(norm_mix), "norm_ffn": row(norm_ffn),
         "w_in": w_in.astype(BF16), "w_br_a": w_br_a.astype(BF16), "w_br_b": w_br_b.astype(BF16),
         "w_br_c": w_br_c.astype(BF16), "w_out": w_out.astype(BF16), "w_gate": w_gate.astype(BF16),
         "w_up": w_up.astype(BF16), "w_down": w_down.astype(BF16)}
    w_mem_kv_b = w_mem_kv.astype(BF16)
    g_oa, g_qb, g_kb, g_qc, g_kc, g_mem = (row(g) for g in (norm_o_a, norm_q_b, norm_k_b, norm_q_c,
                                                           norm_k_c, norm_mem))
    lower = _lower_bounds(lb_logits).reshape(depth, 1, D_A)
    slopes = 2.0 ** (-8.0 * jnp.arange(1, H_B + 1, dtype=F32) / H_B)

    cmk = cache_mem_k.reshape(depth, n_seq, N_MEM, D_C)
    cmv = cache_mem_v.reshape(depth, n_seq, N_MEM, D_C)
    pt_flat = page_table.reshape(-1)
    s0_prompt = jnp.zeros((1, n_prompt, H_A, DK_A, DV_A), F32)
    mem2 = mem_prompt.reshape(n_prompt * N_MEM, D_MODEL)

    xp = x_prompt
    xs = jnp.pad(x_sample, ((0, 0), (0, SAMPLE_ROWS - n_tok), (0, 0)))
    outs = {k: [] for k in ("hp", "kp", "vp", "mkp", "mvp", "hs", "ks", "vs")}

    for layer in range(depth):
        kv = _mm(_rmsnorm(mem2, g_mem, layer), w_mem_kv_b, layer, tm=1024, tn=1024)
        kv = kv.reshape(n_prompt, N_MEM, 2 * D_C)
        mem_k = _head_norm(kv, g_kc, layer, col_block=0, heads=H_C, hd=DH_C)
        mem_v = kv[:, :, D_C:]

        def prompt_branches(z):
            o_a, s_new = _hgrn2(z, lower, g_oa, s0_prompt, layer, 0, chunk=64)
            kn, kb, vt, kmean = _moba_kv(z, g_kb, layer)
            o_b = _moba_prompt(z, g_qb, kb, vt, kmean, slopes, layer)
            o_c = _mem_attn(z, g_qc, mem_k, kv, layer, tq=512, v_col0=D_C // DH_C,
                            kv_index=lambda bi, h, c0: (bi, 0, c0 + h))
            m = n_prompt * seq
            return o_a.reshape(m, D_A), o_b.reshape(m, D_B), o_c.reshape(m, D_C), (s_new, kn)

        def sample_branches(z):
            o_a, s_new = _hgrn2(z, lower, g_oa, state_hgrn, layer, layer, chunk=SAMPLE_ROWS, valid_len=n_tok)
            kn = _head_norm(z, g_kb, layer, col_block=OFF_KB // D_B, heads=H_B, hd=DH_B)
            kmean = _page_means(cache_k, pt_flat, layer, n_seq, n_pages).reshape(n_seq, -1, D_B)
            qn, picks = _moba_select(z, g_qb, kmean, layer)
            picks_flat = picks[:, :, :n_tok, :MOBA_TOPK].reshape(-1)
            o_b = _moba_sample(qn, kn, z, cache_k, cache_v, picks_flat, pt_flat, slopes, layer,
                               n_tok=n_tok, n_pages=n_pages)
            o_c = _mem_attn(z, g_qc, cmk, cmv, layer, tq=SAMPLE_ROWS, v_col0=0,
                            kv_index=lambda bi, h, c0: (layer, bi, 0, c0 + h))
            m = n_seq * SAMPLE_ROWS
            return o_a.reshape(m, D_A), o_b.reshape(m, D_B), o_c.reshape(m, D_C), (s_new, kn)

        xp, zp, (s_p, k_p) = _mixer_and_ffn(xp, layer, w, prompt_branches)
        xs, zs, (s_s, k_s) = _mixer_and_ffn(xs, layer, w, sample_branches)

        outs["hp"].append(s_p)
        outs["kp"].append(k_p.reshape(n_prompt, seq, H_B, DH_B))
        outs["vp"].append(zp[:, :, OFF_VB:OFF_VB + D_B].reshape(n_prompt, seq, H_B, DH_B))
        outs["mkp"].append(mem_k.reshape(n_prompt, N_MEM, H_C, DH_C))
        outs["mvp"].append(mem_v.reshape(n_prompt, N_MEM, H_C, DH_C))
        outs["hs"].append(s_s)
        outs["ks"].append(k_s[:, :n_tok].reshape(n_seq, n_tok, H_B, DH_B))
        outs["vs"].append(zs[:, :n_tok, OFF_VB:OFF_VB + D_B].reshape(n_seq, n_tok, H_B, DH_B))

    stack = lambda k: jnp.stack(outs[k])
    return (xp, xs[:, :n_tok], stack("hp"), stack("kp"), stack("vp"), stack("mkp"), stack("mvp"),
            stack("hs"), stack("ks"), stack("vs"))
```

```python
import functools

import jax
import jax.numpy as jnp
from jax import lax
from jax.experimental import pallas as pl
from jax.experimental.pallas import tpu as pltpu

D_MODEL = 2048
H_A, DK_A, DV_A = 8, 128, 128
D_A = H_A * DK_A
H_B, DH_B = 8, 128
D_B = H_B * DH_B
MOBA_BLOCK = 256
MOBA_TOPK = 3
PAGE_SIZE = 128
PAGES_PER_BLOCK = MOBA_BLOCK // PAGE_SIZE
N_MEM = 256
H_C, DH_C = 4, 256
D_C = H_C * DH_C
N_BRANCH = 3
D_IN = 4 * D_A + 3 * D_B + D_C + N_BRANCH * D_MODEL
EPS = 1e-6

OFF_QA, OFF_FA, OFF_IA, OFF_GA = 0, D_A, 2 * D_A, 3 * D_A
OFF_QB = 4 * D_A
OFF_KB = OFF_QB + D_B
OFF_VB = OFF_KB + D_B
OFF_QC = OFF_VB + D_B
OFF_GATE = OFF_QC + D_C

GLA_SUB = 16
GLA_SAFE_DROP = 80.0
SAMPLE_ROWS = 16
PAGE_MEAN_BLOCKS = 4
V7X_VMEM_LIMIT = 56 * 1024 * 1024

F32 = jnp.float32
BF16 = jnp.bfloat16
_NT = (((1,), (1,)), ((), ()))
_TN = (((0,), (0,)), ((), ()))


def _params(*semantics):
    return pltpu.CompilerParams(dimension_semantics=semantics, vmem_limit_bytes=V7X_VMEM_LIMIT)


def _tile(n, pref):
    t = min(n, pref)
    assert n % t == 0, (n, pref)
    return t


def _rms(x, gain):
    return x * lax.rsqrt(jnp.mean(x * x, axis=-1, keepdims=True) + EPS) * gain


def _rmsnorm_kernel(x_ref, g_ref, o_ref):
    o_ref[...] = _rms(x_ref[...], g_ref[...]).astype(o_ref.dtype)


def _rmsnorm(x, gains, layer):
    m, d = x.shape
    tm = _tile(m, 512)
    return pl.pallas_call(
        _rmsnorm_kernel,
        out_shape=jax.ShapeDtypeStruct((m, d), BF16),
        grid=(m // tm,),
        in_specs=[pl.BlockSpec((tm, d), lambda i: (i, 0)),
                  pl.BlockSpec((None, 1, d), lambda i: (layer, 0, 0))],
        out_specs=pl.BlockSpec((tm, d), lambda i: (i, 0)),
        compiler_params=_params("parallel"),
        name="rmsnorm",
    )(x, gains)


def _dot(a, b):
    return jnp.dot(a, b, preferred_element_type=F32)


def _mm_groups_kernel(*refs, n_groups, n_w, act_of_w, n_extra, epilogue):
    n_act = len(set(act_of_w))
    pos = 0
    acts = [refs[pos + g * n_act:pos + (g + 1) * n_act] for g in range(n_groups)]
    pos += n_groups * n_act
    weights = refs[pos:pos + n_w]
    pos += n_w
    extras = [refs[pos + g * n_extra:pos + (g + 1) * n_extra] for g in range(n_groups)]
    pos += n_groups * n_extra
    outs = refs[pos:pos + n_groups]
    wb = refs[pos + n_groups:]

    def compute(g):
        dots = [_dot(acts[g][act_of_w[k]][...], wb[k][...]) for k in range(n_w)]
        outs[g][...] = epilogue(dots, [e[...] for e in extras[g]]).astype(outs[g].dtype)

    @pl.when(pl.program_id(1) == 0)
    def _():
        for k in range(n_w):
            wb[k][...] = weights[k][...].astype(BF16)
        for g in range(1, n_groups):
            compute(g)

    compute(0)


def _mm_groups(acts, weights, layer, *, epilogue, out_dtype, tm, tn, act_of_w=None, extras=(), name):
    n_groups, n_w = len(acts), len(weights)
    act_of_w = tuple(range(n_w)) if act_of_w is None else tuple(act_of_w)
    k, n = weights[0].shape[1:]
    m = [a[0].shape[0] for a in acts]
    tm, tn = _tile(m[0], tm), _tile(n, tn)
    rows = [tm] + m[1:]

    def row_block(g):
        return (lambda j, i: (i, 0)) if g == 0 else (lambda j, i: (0, 0))

    def out_block(g, first=0):
        return (lambda j, i: (i, first + j)) if g == 0 else (lambda j, i: (0, first + j))

    in_specs, args = [], []
    for g in range(n_groups):
        in_specs += [pl.BlockSpec((rows[g], a.shape[1]), row_block(g)) for a in acts[g]]
        args += list(acts[g])
    in_specs += [pl.BlockSpec((None, k, tn), lambda j, i: (layer, 0, j))] * n_w
    args += list(weights)
    for g in range(n_groups):
        for arrays, col0 in extras:
            in_specs.append(pl.BlockSpec((rows[g], tn), out_block(g, col0 // tn)))
            args.append(arrays[g])
    return pl.pallas_call(
        functools.partial(_mm_groups_kernel, n_groups=n_groups, n_w=n_w, act_of_w=act_of_w,
                          n_extra=len(extras), epilogue=epilogue),
        out_shape=tuple(jax.ShapeDtypeStruct((m[g], n), out_dtype) for g in range(n_groups)),
        grid=(n // tn, m[0] // tm),
        in_specs=in_specs,
        out_specs=tuple(pl.BlockSpec((rows[g], tn), out_block(g)) for g in range(n_groups)),
        scratch_shapes=[pltpu.VMEM((k, tn), BF16)] * n_w,
        compiler_params=_params("arbitrary", "arbitrary"),
        name=name,
    )(*args)


def _ep_plain(dots, extras):
    return dots[0]


def _ep_residual(dots, extras):
    return extras[0] + dots[0]


def _ep_swiglu(dots, extras):
    return dots[0] * jax.nn.sigmoid(dots[0]) * dots[1]


def _ep_gated_sum(dots, extras):
    return functools.reduce(jnp.add, [jax.nn.sigmoid(g) * d for g, d in zip(extras, dots)])


def _lower_bounds_kernel(x_ref, o_ref):
    x = x_ref[...]
    e = jnp.exp(x - jnp.max(x, axis=0, keepdims=True))
    p = e / jnp.sum(e, axis=0, keepdims=True)
    rows = [jnp.zeros_like(p[0:1])]
    for r in range(1, x.shape[0]):
        rows.append(rows[-1] + p[r:r + 1])
    o_ref[...] = jnp.concatenate(rows, axis=0)


def _lower_bounds(lb_logits):
    return pl.pallas_call(
        _lower_bounds_kernel,
        out_shape=jax.ShapeDtypeStruct(lb_logits.shape, F32),
        name="hgrn2_lower_bounds",
    )(lb_logits)


def _hgrn2_pair_weights_factored(q, kk, G, g_mid):
    qt = (q * jnp.exp(G - g_mid)).astype(BF16)
    kt = (kk * jnp.exp(g_mid - G)).astype(BF16)
    a = lax.dot_general(qt, kt, _NT, preferred_element_type=F32)
    causal = lax.broadcasted_iota(jnp.int32, a.shape, 1) <= lax.broadcasted_iota(jnp.int32, a.shape, 0)
    return jnp.where(causal, a, 0.0).astype(BF16)


def _hgrn2_intra_exact(q, kk, v, G, rmod, chunk):
    pieces = [jnp.zeros((GLA_SUB, DV_A), F32)]
    for i in range(1, chunk // GLA_SUB):
        lo = i * GLA_SUB
        r = G[lo - 1:lo]
        qi = (q[lo:lo + GLA_SUB] * jnp.exp(G[lo:lo + GLA_SUB] - r)).astype(BF16)
        kj = (kk[:lo] * jnp.exp(r - G[:lo])).astype(BF16)
        a = lax.dot_general(qi, kj, _NT, preferred_element_type=F32)
        pieces.append(_dot(a.astype(BF16), v[:lo].astype(BF16)))
    o = jnp.concatenate(pieces, axis=0)
    for d in range(GLA_SUB):
        k_d = kk if d == 0 else pltpu.roll(kk, d, axis=0)
        g_d = G if d == 0 else pltpu.roll(G, d, axis=0)
        v_d = v if d == 0 else pltpu.roll(v, d, axis=0)
        decay = jnp.exp(jnp.where(rmod >= d, G - g_d, -jnp.inf))
        a_d = jnp.sum(q * k_d * decay, axis=-1, keepdims=True)
        o = o + a_d * v_d
    return o


def _hgrn2_kernel(q_ref, f_ref, i_ref, g_ref, lb_ref, gn_ref, s0_ref, o_ref, sout_ref,
                  st_ref, k_ref, qs_ref, G_ref, oi_ref, *, chunk, valid_len):
    c = pl.program_id(1)

    @pl.when(c == 0)
    def _():
        for h in range(H_A):
            st_ref[h] = s0_ref[h].T

    heads = [slice(h * DK_A, (h + 1) * DK_A) for h in range(H_A)]
    row = lax.broadcasted_iota(jnp.int32, (chunk, 1), 0)
    rmod = row % GLA_SUB
    tri = (lax.broadcasted_iota(jnp.int32, (chunk, chunk), 0)
           >= lax.broadcasted_iota(jnp.int32, (chunk, chunk), 1)).astype(F32)

    for sl in heads:
        zf = f_ref[:, sl]
        lb = lb_ref[:, sl]
        e = jnp.exp(-jnp.abs(zf))
        log_sig = jnp.minimum(zf, 0.0) - jnp.log1p(e)
        la = jnp.log(lb)
        lc = jnp.log1p(-lb) + log_sig
        log_f = jnp.maximum(la, lc) + jnp.log1p(jnp.exp(-jnp.abs(la - lc)))
        kk = (1.0 - lb) * (jnp.where(zf >= 0.0, e, 1.0) / (1.0 + e))
        if valid_len is not None:
            live = (c * chunk + row) < valid_len
            log_f = jnp.where(live, log_f, 0.0)
            kk = jnp.where(live, kk, 0.0)
        qr = q_ref[:, sl]
        k_ref[:, sl] = kk
        qs_ref[:, sl] = qr * jax.nn.sigmoid(qr) * (DK_A ** -0.5)
        G_ref[:, sl] = jnp.dot(tri, log_f, precision=lax.Precision.HIGHEST, preferred_element_type=F32)

    mid = chunk // 2
    g_mid = G_ref[mid - 1:mid, :]
    g_last = G_ref[chunk - 1:chunk, :]
    safe = jnp.max(jnp.maximum(-g_mid, g_mid - g_last)) < GLA_SAFE_DROP

    @pl.when(safe)
    def _():
        weights = [_hgrn2_pair_weights_factored(qs_ref[:, sl], k_ref[:, sl], G_ref[:, sl], g_mid[:, sl])
                   for sl in heads]
        for sl, a in zip(heads, weights):
            oi_ref[:, sl] = _dot(a, i_ref[:, sl].astype(BF16))

    @pl.when(jnp.logical_not(safe))
    def _():
        for sl in heads:
            oi_ref[:, sl] = _hgrn2_intra_exact(qs_ref[:, sl], k_ref[:, sl], i_ref[:, sl], G_ref[:, sl], rmod, chunk)

    carried = [lax.dot_general((qs_ref[:, sl] * jnp.exp(G_ref[:, sl])).astype(BF16), st_ref[h].astype(BF16),
                               _NT, preferred_element_type=F32) for h, sl in enumerate(heads)]
    added = [lax.dot_general(i_ref[:, sl].astype(BF16),
                             (k_ref[:, sl] * jnp.exp(g_last[:, sl] - G_ref[:, sl])).astype(BF16),
                             _TN, preferred_element_type=F32) for sl in heads]
    gn = gn_ref[...]
    for h, sl in enumerate(heads):
        st_ref[h] = st_ref[h] * jnp.exp(g_last[:, sl]) + added[h]
        gr = g_ref[:, sl]
        o = oi_ref[:, sl] + carried[h]
        o_ref[:, sl] = (_rms(o, gn) * (gr * jax.nn.sigmoid(gr))).astype(o_ref.dtype)

    @pl.when(c == pl.num_programs(1) - 1)
    def _():
        for h in range(H_A):
            sout_ref[h] = st_ref[h].T


def _hgrn2(z, lower, g_onorm, s0, layer, s0_layer, *, chunk, valid_len=None):
    b, l, _ = z.shape
    chunk = _tile(l, chunk)

    def col_spec(off):
        return pl.BlockSpec((None, chunk, D_A), lambda bi, c: (bi, c, off // D_A))

    return pl.pallas_call(
        functools.partial(_hgrn2_kernel, chunk=chunk, valid_len=valid_len),
        out_shape=(jax.ShapeDtypeStruct((b, l, D_A), BF16),
                   jax.ShapeDtypeStruct((b, H_A, DK_A, DV_A), F32)),
        grid=(b, l // chunk),
        in_specs=[col_spec(OFF_QA), col_spec(OFF_FA), col_spec(OFF_IA), col_spec(OFF_GA),
                  pl.BlockSpec((None, 1, D_A), lambda bi, c: (layer, 0, 0)),
                  pl.BlockSpec((None, 1, DV_A), lambda bi, c: (layer, 0, 0)),
                  pl.BlockSpec((None, None, H_A, DK_A, DV_A), lambda bi, c: (s0_layer, bi, 0, 0, 0))],
        out_specs=(pl.BlockSpec((None, chunk, D_A), lambda bi, c: (bi, c, 0)),
                   pl.BlockSpec((None, H_A, DK_A, DV_A), lambda bi, c: (bi, 0, 0, 0))),
        scratch_shapes=[pltpu.VMEM((H_A, DV_A, DK_A), F32)] + [pltpu.VMEM((chunk, D_A), F32)] * 4,
        compiler_params=_params("parallel", "arbitrary"),
        name="hgrn2",
    )(z, z, z, z, lower, g_onorm, s0)


def _head_norm_kernel(x_ref, g_ref, o_ref, *, heads, hd):
    g = g_ref[...]
    for h in range(heads):
        sl = slice(h * hd, (h + 1) * hd)
        o_ref[:, sl] = _rms(x_ref[:, sl], g)


def _head_norm(x, gains, layer, *, col_block, heads, hd):
    b, r, _ = x.shape
    width = heads * hd
    return pl.pallas_call(
        functools.partial(_head_norm_kernel, heads=heads, hd=hd),
        out_shape=jax.ShapeDtypeStruct((b, r, width), F32),
        grid=(b,),
        in_specs=[pl.BlockSpec((None, r, width), lambda bi: (bi, 0, col_block)),
                  pl.BlockSpec((None, 1, hd), lambda bi: (layer, 0, 0))],
        out_specs=pl.BlockSpec((None, r, width), lambda bi: (bi, 0, 0)),
        compiler_params=_params("parallel"),
        name="head_norm",
    )(x, gains)


def _moba_kv_kernel(k_ref, v_ref, g_ref, kn_ref, kb_ref, vt_ref, km_ref):
    g = g_ref[...]
    for h in range(H_B):
        sl = slice(h * DH_B, (h + 1) * DH_B)
        kn = _rms(k_ref[:, sl], g)
        kn_ref[:, sl] = kn
        kb_ref[:, sl] = kn.astype(BF16)
        km_ref[:, sl] = jnp.mean(kn, axis=0, keepdims=True)
        vt_ref[sl, :] = v_ref[:, sl].T.astype(BF16)


def _moba_kv(z, g_kb, layer):
    b, l, _ = z.shape
    nblk = l // MOBA_BLOCK
    act = lambda dtype: jax.ShapeDtypeStruct((b, l, D_B), dtype)
    act_spec = pl.BlockSpec((None, MOBA_BLOCK, D_B), lambda bi, n: (bi, n, 0))
    kn, kb, vt, km = pl.pallas_call(
        _moba_kv_kernel,
        out_shape=(act(F32), act(BF16), jax.ShapeDtypeStruct((b, D_B, l), BF16),
                   jax.ShapeDtypeStruct((b, nblk, 1, D_B), F32)),
        grid=(b, nblk),
        in_specs=[pl.BlockSpec((None, MOBA_BLOCK, D_B), lambda bi, n: (bi, n, OFF_KB // D_B)),
                  pl.BlockSpec((None, MOBA_BLOCK, D_B), lambda bi, n: (bi, n, OFF_VB // D_B)),
                  pl.BlockSpec((None, 1, DH_B), lambda bi, n: (layer, 0, 0))],
        out_specs=(act_spec, act_spec,
                   pl.BlockSpec((None, D_B, MOBA_BLOCK), lambda bi, n: (bi, 0, n)),
                   pl.BlockSpec((None, None, 1, D_B), lambda bi, n: (bi, n, 0, 0))),
        compiler_params=_params("parallel", "parallel"),
        name="moba_kv",
    )(z, z, g_kb)
    return kn, kb, vt, km.reshape(b, nblk, D_B)


def _topk_mask_t(gate_t, n_valid, topk):
    n, r = gate_t.shape
    rowi = lax.broadcasted_iota(jnp.int32, (n, r), 0)
    gm = jnp.where(rowi < n_valid, gate_t, -jnp.inf)
    sel = jnp.zeros((n, r), F32)
    for j in range(n):
        gj = gm[j:j + 1]
        beats = jnp.where(gm > gj, 1.0, jnp.where((gm == gj) & (rowi < j), 1.0, 0.0))
        rank = jnp.sum(beats, axis=0, keepdims=True)
        sel = jnp.where((rowi == j) & (rank < topk) & (rowi < n_valid), 1.0, sel)
    return sel


def _moba_prompt_kernel(slopes_ref, q_ref, gq_ref, k_ref, vt_ref, km_ref, o_ref, s_ref, qb_ref, sel_ref):
    h = pl.program_id(1)
    i = pl.program_id(2)
    blk = MOBA_BLOCK
    slope = slopes_ref[h]
    scale = DH_B ** -0.5

    @pl.when(i == 0)
    def _():
        qn = _rms(q_ref[...], gq_ref[...])
        qb_ref[...] = qn.astype(BF16)
        gate_t = lax.dot_general(km_ref[...], qn, _NT, precision=lax.Precision.HIGHEST,
                                 preferred_element_type=F32)
        q_block = lax.broadcasted_iota(jnp.int32, gate_t.shape, 1) // blk
        sel_ref[...] = _topk_mask_t(gate_t, q_block, MOBA_TOPK)

    d0 = (lax.broadcasted_iota(jnp.int32, (blk, blk), 1)
          - lax.broadcasted_iota(jnp.int32, (blk, blk), 0)).astype(F32)

    def attend(own):
        cols = slice(own * blk, (own + 1) * blk)
        qb = qb_ref[cols, :]
        m = None
        for j in range(own + 1):
            rows = slice(j * blk, (j + 1) * blk)
            s = lax.dot_general(k_ref[rows, :], qb, _NT, preferred_element_type=F32) * scale
            s = s - slope * (d0 + float((own - j) * blk))
            keep = (d0 >= 0.0) if j == own else (sel_ref[j:j + 1, cols] > 0.0)
            s = jnp.where(keep, s, -jnp.inf)
            s_ref[rows, :] = s
            m_j = jnp.max(s, axis=0, keepdims=True)
            m = m_j if m is None else jnp.maximum(m, m_j)
        l = jnp.zeros((1, blk), F32)
        acc = jnp.zeros((DH_B, blk), F32)
        for j in range(own + 1):
            rows = slice(j * blk, (j + 1) * blk)
            p = jnp.exp(s_ref[rows, :] - m)
            l = l + jnp.sum(p, axis=0, keepdims=True)
            acc = acc + _dot(vt_ref[:, rows], p.astype(BF16))
        o_ref[...] = (acc / l).T.astype(o_ref.dtype)

    for own in range(km_ref.shape[0]):
        pl.when(i == own)(functools.partial(attend, own))


def _moba_prompt(z, g_qb, kb, vt, kmean, slopes, layer):
    b, l, _ = z.shape
    nblk = l // MOBA_BLOCK
    return pl.pallas_call(
        _moba_prompt_kernel,
        out_shape=jax.ShapeDtypeStruct((b, l, D_B), BF16),
        grid=(b, H_B, nblk),
        in_specs=[pl.BlockSpec(memory_space=pltpu.SMEM),
                  pl.BlockSpec((None, l, DH_B), lambda bi, h, i: (bi, 0, OFF_QB // DH_B + h)),
                  pl.BlockSpec((None, 1, DH_B), lambda bi, h, i: (layer, 0, 0)),
                  pl.BlockSpec((None, l, DH_B), lambda bi, h, i: (bi, 0, h)),
                  pl.BlockSpec((None, DH_B, l), lambda bi, h, i: (bi, h, 0)),
                  pl.BlockSpec((None, nblk, DH_B), lambda bi, h, i: (bi, 0, h))],
        out_specs=pl.BlockSpec((None, MOBA_BLOCK, DH_B), lambda bi, h, i: (bi, i, h)),
        scratch_shapes=[pltpu.VMEM((l, MOBA_BLOCK), F32), pltpu.VMEM((l, DH_B), BF16),
                        pltpu.VMEM((nblk, l), F32)],
        compiler_params=_params("parallel", "parallel", "arbitrary"),
        name="moba_prompt",
    )(slopes, z, g_qb, kb, vt, kmean)


def _page_mean_kernel(pt_ref, *refs):
    del pt_ref
    page_refs, o_ref = refs[:-1], refs[-1]
    for n in range(PAGE_MEAN_BLOCKS):
        pages = page_refs[n * PAGES_PER_BLOCK:(n + 1) * PAGES_PER_BLOCK]
        total = jnp.sum(pages[0][...], axis=0)
        for r in pages[1:]:
            total = total + jnp.sum(r[...], axis=0)
        o_ref[n] = total * (1.0 / MOBA_BLOCK)


def _page_means(cache_k, page_table_flat, layer, n_seq, n_pages):
    n_blocks = n_pages // PAGES_PER_BLOCK
    assert n_blocks % PAGE_MEAN_BLOCKS == 0
    pages_per_step = PAGE_MEAN_BLOCKS * PAGES_PER_BLOCK

    def page_spec(p):
        return pl.BlockSpec(
            (None, None, PAGE_SIZE, H_B, DH_B),
            lambda bi, n, pt: (layer, pt[bi * n_pages + n * pages_per_step + p], 0, 0, 0))

    return pl.pallas_call(
        _page_mean_kernel,
        out_shape=jax.ShapeDtypeStruct((n_seq, n_blocks, H_B, DH_B), F32),
        grid_spec=pltpu.PrefetchScalarGridSpec(
            num_scalar_prefetch=1,
            grid=(n_seq, n_blocks // PAGE_MEAN_BLOCKS),
            in_specs=[page_spec(p) for p in range(pages_per_step)],
            out_specs=pl.BlockSpec((None, PAGE_MEAN_BLOCKS, H_B, DH_B), lambda bi, n, pt: (bi, n, 0, 0))),
        compiler_params=_params("parallel", "parallel"),
        name="moba_page_means",
    )(page_table_flat, *([cache_k] * pages_per_step))


def _moba_select_kernel(q_ref, gq_ref, km_ref, qn_ref, sel_ref):
    rows = q_ref.shape[0]
    n_blocks = km_ref.shape[0]
    g = gq_ref[...]
    col = lax.broadcasted_iota(jnp.int32, (rows, n_blocks), 1)
    lane = lax.broadcasted_iota(jnp.int32, (rows, sel_ref.shape[-1]), 1)
    for h in range(H_B):
        sl = slice(h * DH_B, (h + 1) * DH_B)
        qn = _rms(q_ref[:, sl], g)
        qn_ref[:, sl] = qn
        gate = lax.dot_general(qn, km_ref[:, sl], _NT, precision=lax.Precision.HIGHEST,
                               preferred_element_type=F32)
        picks = jnp.zeros(lane.shape, jnp.int32)
        for r in range(MOBA_TOPK):
            best = jnp.max(gate, axis=-1, keepdims=True)
            idx = jnp.min(jnp.where(gate == best, col, n_blocks), axis=-1, keepdims=True)
            picks = jnp.where(lane == r, idx, picks)
            gate = jnp.where(col == idx, -jnp.inf, gate)
        sel_ref[h] = picks


def _moba_select(z, g_qb, kmean, layer):
    b, rows, _ = z.shape
    n_blocks = kmean.shape[1]
    return pl.pallas_call(
        _moba_select_kernel,
        out_shape=(jax.ShapeDtypeStruct((b, rows, D_B), F32),
                   jax.ShapeDtypeStruct((b, H_B, rows, 128), jnp.int32)),
        grid=(b,),
        in_specs=[pl.BlockSpec((None, rows, D_B), lambda bi: (bi, 0, OFF_QB // D_B)),
                  pl.BlockSpec((None, 1, DH_B), lambda bi: (layer, 0, 0)),
                  pl.BlockSpec((None, n_blocks, D_B), lambda bi: (bi, 0, 0))],
        out_specs=(pl.BlockSpec((None, rows, D_B), lambda bi: (bi, 0, 0)),
                   pl.BlockSpec((None, H_B, rows, 128), lambda bi: (bi, 0, 0, 0))),
        compiler_params=_params("parallel"),
        name="moba_select",
    )(z, g_qb, kmean)


def _moba_sample_kernel(sel_ref, pt_ref, slopes_ref, q_ref, kown_ref, vown_ref, ck_ref, cv_ref, o_ref,
                        kbuf, vbuf, sem, *, layer, n_tok, n_pages):
    step = pl.program_id(0)
    n_steps = pl.num_programs(0)
    tiles_per_tok = MOBA_TOPK * PAGES_PER_BLOCK
    past_len = n_pages * PAGE_SIZE

    def picked_block(st, t, s):
        return sel_ref[(st * n_tok + t) * MOBA_TOPK + s]

    def tile_copies(st, slot):
        bi, h = st // H_B, st % H_B
        copies = []
        for t in range(n_tok):
            for s in range(MOBA_TOPK):
                block = picked_block(st, t, s)
                for p in range(PAGES_PER_BLOCK):
                    page = pt_ref[bi * n_pages + block * PAGES_PER_BLOCK + p]
                    j = t * tiles_per_tok + s * PAGES_PER_BLOCK + p
                    copies.append(pltpu.make_async_copy(
                        ck_ref.at[layer, page, :, h, :], kbuf.at[slot, j], sem.at[slot]))
                    copies.append(pltpu.make_async_copy(
                        cv_ref.at[layer, page, :, h, :], vbuf.at[slot, j], sem.at[slot]))
        return copies

    slot = step % 2

    @pl.when(step == 0)
    def _():
        for c in tile_copies(step, slot):
            c.start()

    @pl.when(step + 1 < n_steps)
    def _():
        for c in tile_copies(step + 1, 1 - slot):
            c.start()

    for c in tile_copies(step, slot):
        c.wait()

    slope = slopes_ref[step % H_B]
    scale = DH_B ** -0.5
    rows = kown_ref.shape[0]
    k_own, v_own = kown_ref[...], vown_ref[...]
    own_pos = past_len + lax.broadcasted_iota(jnp.int32, (rows, 1), 0)
    page_row = lax.broadcasted_iota(jnp.int32, (PAGE_SIZE, 1), 0)
    out_row = lax.broadcasted_iota(jnp.int32, (rows, 1), 0)
    out = jnp.zeros((rows, DH_B), F32)
    for t in range(n_tok):
        q = q_ref[t:t + 1, :]
        t_pos = past_len + t

        def score(keys, k_pos):
            return (jnp.sum(keys * q, axis=-1, keepdims=True) * scale
                    - slope * (t_pos - k_pos).astype(F32))

        scores = [jnp.where(own_pos <= t_pos, score(k_own, own_pos), -jnp.inf)]
        values = [v_own]
        for s in range(MOBA_TOPK):
            block = picked_block(step, t, s)
            for p in range(PAGES_PER_BLOCK):
                j = t * tiles_per_tok + s * PAGES_PER_BLOCK + p
                scores.append(score(kbuf[slot, j], block * MOBA_BLOCK + p * PAGE_SIZE + page_row))
                values.append(vbuf[slot, j])
        m = functools.reduce(jnp.maximum, [jnp.max(s, axis=0, keepdims=True) for s in scores])
        probs = [jnp.exp(s - m) for s in scores]
        l = functools.reduce(jnp.add, [jnp.sum(p, axis=0, keepdims=True) for p in probs])
        acc = functools.reduce(jnp.add, [jnp.sum(p * v, axis=0, keepdims=True) for p, v in zip(probs, values)])
        out = jnp.where(out_row == t, acc / l, out)
    o_ref[...] = out.astype(o_ref.dtype)


def _moba_sample(qn, kn, z, cache_k, cache_v, picks_flat, page_table_flat, slopes, layer, *, n_tok, n_pages):
    b, rows, _ = qn.shape
    n_tiles = n_tok * MOBA_TOPK * PAGES_PER_BLOCK
    own_spec = pl.BlockSpec((None, rows, DH_B), lambda st, sel, pt: (st // H_B, 0, st % H_B))
    return pl.pallas_call(
        functools.partial(_moba_sample_kernel, layer=layer, n_tok=n_tok, n_pages=n_pages),
        out_shape=jax.ShapeDtypeStruct((b, rows, D_B), BF16),
        grid_spec=pltpu.PrefetchScalarGridSpec(
            num_scalar_prefetch=2,
            grid=(b * H_B,),
            in_specs=[pl.BlockSpec(memory_space=pltpu.SMEM), own_spec, own_spec,
                      pl.BlockSpec((None, rows, DH_B),
                                   lambda st, sel, pt: (st // H_B, 0, OFF_VB // DH_B + st % H_B)),
                      pl.BlockSpec(memory_space=pl.ANY), pl.BlockSpec(memory_space=pl.ANY)],
            out_specs=pl.BlockSpec((None, rows, DH_B), lambda st, sel, pt: (st // H_B, 0, st % H_B)),
            scratch_shapes=[pltpu.VMEM((2, n_tiles, PAGE_SIZE, DH_B), F32),
                            pltpu.VMEM((2, n_tiles, PAGE_SIZE, DH_B), F32),
                            pltpu.SemaphoreType.DMA((2,))]),
        compiler_params=_params("arbitrary"),
        name="moba_sample",
    )(picks_flat, page_table_flat, slopes, qn, kn, z, cache_k, cache_v)


def _mem_attn_kernel(q_ref, gq_ref, k_ref, v_ref, o_ref):
    qn = _rms(q_ref[...], gq_ref[...])
    s = lax.dot_general(qn.astype(BF16), k_ref[...].astype(BF16), _NT, preferred_element_type=F32)
    s = s * (DH_C ** -0.5)
    e = jnp.exp(s - jnp.max(s, axis=-1, keepdims=True))
    p = e / jnp.sum(e, axis=-1, keepdims=True)
    o_ref[...] = _dot(p.astype(BF16), v_ref[...].astype(BF16)).astype(o_ref.dtype)


def _mem_attn(z, g_qc, mem_k, mem_v, layer, *, kv_index, v_col0, tq):
    b, l, _ = z.shape
    tq = _tile(l, tq)
    k_block = (None,) * (mem_k.ndim - 2) + (N_MEM, DH_C)
    v_block = (None,) * (mem_v.ndim - 2) + (N_MEM, DH_C)
    return pl.pallas_call(
        _mem_attn_kernel,
        out_shape=jax.ShapeDtypeStruct((b, l, D_C), BF16),
        grid=(b, H_C, l // tq),
        in_specs=[pl.BlockSpec((None, tq, DH_C), lambda bi, h, i: (bi, i, OFF_QC // DH_C + h)),
                  pl.BlockSpec((None, 1, DH_C), lambda bi, h, i: (layer, 0, 0)),
                  pl.BlockSpec(k_block, lambda bi, h, i: kv_index(bi, h, 0)),
                  pl.BlockSpec(v_block, lambda bi, h, i: kv_index(bi, h, v_col0))],
        out_specs=pl.BlockSpec((None, tq, DH_C), lambda bi, h, i: (bi, i, h)),
        compiler_params=_params("parallel", "parallel", "arbitrary"),
        name="mem_attn",
    )(z, g_qc, mem_k, mem_v)


def _mixer_and_ffn(xs, layer, w, branches):
    shapes = [x.shape[:2] for x in xs]
    x2 = [x.reshape(-1, D_MODEL) for x in xs]
    h = [_rmsnorm(x, w["norm_mix"], layer) for x in x2]
    z2 = _mm_groups([[a] for a in h], [w["w_in"]], layer, epilogue=_ep_plain, out_dtype=F32,
                    tm=1024, tn=1024, name="mm_in")
    outs = [fn(z.reshape(*s, D_IN)) for fn, z, s in zip(branches, z2, shapes)]
    gates = [(z2, OFF_GATE + k * D_MODEL) for k in range(N_BRANCH)]
    merged = _mm_groups([list(o[:N_BRANCH]) for o in outs], [w["w_br_a"], w["w_br_b"], w["w_br_c"]], layer,
                        epilogue=_ep_gated_sum, extras=gates, out_dtype=BF16, tm=1024, tn=512, name="mm_merge")
    x2 = _mm_groups([[a] for a in merged], [w["w_out"]], layer, epilogue=_ep_residual, extras=[(x2, 0)],
                    out_dtype=F32, tm=1024, tn=512, name="mm_out")
    h = [_rmsnorm(x, w["norm_ffn"], layer) for x in x2]
    act = _mm_groups([[a] for a in h], [w["w_gate"], w["w_up"]], layer, act_of_w=(0, 0), epilogue=_ep_swiglu,
                     out_dtype=BF16, tm=1024, tn=512, name="mm_swiglu")
    x2 = _mm_groups([[a] for a in act], [w["w_down"]], layer, epilogue=_ep_residual, extras=[(x2, 0)],
                    out_dtype=F32, tm=512, tn=512, name="mm_down")
    return ([x.reshape(*s, D_MODEL) for x, s in zip(x2, shapes)],
            [z.reshape(*s, D_IN) for z, s in zip(z2, shapes)], [o[N_BRANCH] for o in outs])


def kernel(x_prompt, x_sample, state_hgrn, cache_k, cache_v, cache_mem_k, cache_mem_v, page_table,
           mem_prompt, lb_logits, norm_mix, w_in, norm_o_a, norm_q_b, norm_k_b, norm_q_c, norm_k_c,
           norm_mem, w_mem_kv, w_br_a, w_br_b, w_br_c, w_out, norm_ffn, w_gate, w_up, w_down):
    depth = w_in.shape[0]
    n_prompt, seq, _ = x_prompt.shape
    n_seq, n_tok, _ = x_sample.shape
    n_pages = page_table.shape[1]
    assert seq % MOBA_BLOCK == 0 and n_pages % PAGES_PER_BLOCK == 0
    assert n_pages // PAGES_PER_BLOCK >= MOBA_TOPK and n_tok <= SAMPLE_ROWS <= MOBA_BLOCK

    row = lambda g: g.reshape(depth, 1, g.shape[-1])
    w = {"norm_mix": row(norm_mix), "norm_ffn": row(norm_ffn), "w_in": w_in, "w_br_a": w_br_a, "w_br_b": w_br_b,
         "w_br_c": w_br_c, "w_out": w_out, "w_gate": w_gate, "w_up": w_up, "w_down": w_down}
    g_oa, g_qb, g_kb, g_qc, g_kc, g_mem = (row(g) for g in (norm_o_a, norm_q_b, norm_k_b, norm_q_c,
                                                           norm_k_c, norm_mem))
    lower = _lower_bounds(lb_logits).reshape(depth, 1, D_A)
    slopes = 2.0 ** (-8.0 * jnp.arange(1, H_B + 1, dtype=F32) / H_B)

    cmk = cache_mem_k.reshape(depth, n_seq, N_MEM, D_C)
    cmv = cache_mem_v.reshape(depth, n_seq, N_MEM, D_C)
    pt_flat = page_table.reshape(-1)
    s0_prompt = jnp.zeros((1, n_prompt, H_A, DK_A, DV_A), F32)
    mem2 = mem_prompt.reshape(n_prompt * N_MEM, D_MODEL)

    xp = x_prompt
    xs = jnp.pad(x_sample, ((0, 0), (0, SAMPLE_ROWS - n_tok), (0, 0)))
    outs = {k: [] for k in ("hp", "kp", "vp", "mkp", "mvp", "hs", "ks", "vs")}

    for layer in range(depth):
        (kv,) = _mm_groups([[_rmsnorm(mem2, g_mem, layer)]], [w_mem_kv], layer, epilogue=_ep_plain,
                           out_dtype=F32, tm=1024, tn=1024, name="mm_mem")
        kv = kv.reshape(n_prompt, N_MEM, 2 * D_C)
        mem_k = _head_norm(kv, g_kc, layer, col_block=0, heads=H_C, hd=DH_C)
        mem_v = kv[:, :, D_C:]

        def prompt_branches(z):
            o_a, s_new = _hgrn2(z, lower, g_oa, s0_prompt, layer, 0, chunk=64)
            kn, kb, vt, kmean = _moba_kv(z, g_kb, layer)
            o_b = _moba_prompt(z, g_qb, kb, vt, kmean, slopes, layer)
            o_c = _mem_attn(z, g_qc, mem_k, kv, layer, tq=512, v_col0=D_C // DH_C,
                            kv_index=lambda bi, h, c0: (bi, 0, c0 + h))
            m = n_prompt * seq
            return o_a.reshape(m, D_A), o_b.reshape(m, D_B), o_c.reshape(m, D_C), (s_new, kn)

        def sample_branches(z):
            o_a, s_new = _hgrn2(z, lower, g_oa, state_hgrn, layer, layer, chunk=SAMPLE_ROWS, valid_len=n_tok)
            kn = _head_norm(z, g_kb, layer, col_block=OFF_KB // D_B, heads=H_B, hd=DH_B)
            kmean = _page_means(cache_k, pt_flat, layer, n_seq, n_pages).reshape(n_seq, -1, D_B)
            qn, picks = _moba_select(z, g_qb, kmean, layer)
            picks_flat = picks[:, :, :n_tok, :MOBA_TOPK].reshape(-1)
            o_b = _moba_sample(qn, kn, z, cache_k, cache_v, picks_flat, pt_flat, slopes, layer,
                               n_tok=n_tok, n_pages=n_pages)
            o_c = _mem_attn(z, g_qc, cmk, cmv, layer, tq=SAMPLE_ROWS, v_col0=0,
                            kv_index=lambda bi, h, c0: (layer, bi, 0, c0 + h))
            m = n_seq * SAMPLE_ROWS
            return o_a.reshape(m, D_A), o_b.reshape(m, D_B), o_c.reshape(m, D_C), (s_new, kn)

        (xp, xs), (zp, zs), ((s_p, k_p), (s_s, k_s)) = _mixer_and_ffn(
            [xp, xs], layer, w, [prompt_branches, sample_branches])

        outs["hp"].append(s_p)
        outs["kp"].append(k_p.reshape(n_prompt, seq, H_B, DH_B))
        outs["vp"].append(zp[:, :, OFF_VB:OFF_VB + D_B].reshape(n_prompt, seq, H_B, DH_B))
        outs["mkp"].append(mem_k.reshape(n_prompt, N_MEM, H_C, DH_C))
        outs["mvp"].append(mem_v.reshape(n_prompt, N_MEM, H_C, DH_C))
        outs["hs"].append(s_s)
        outs["ks"].append(k_s[:, :n_tok].reshape(n_seq, n_tok, H_B, DH_B))
        outs["vs"].append(zs[:, :n_tok, OFF_VB:OFF_VB + D_B].reshape(n_seq, n_tok, H_B, DH_B))

    stack = lambda k: jnp.stack(outs[k])
    return (xp, xs[:, :n_tok], stack("hp"), stack("kp"), stack("vp"), stack("mkp"), stack("mvp"),
            stack("hs"), stack("ks"), stack("vs"))
```

```python
import functools

import jax
import jax.numpy as jnp
from jax import lax
from jax.experimental import pallas as pl
from jax.experimental.pallas import tpu as pltpu

D_MODEL = 2048
H_A, DK_A, DV_A = 8, 128, 128
D_A = H_A * DK_A
H_B, DH_B = 8, 128
D_B = H_B * DH_B
MOBA_BLOCK = 256
MOBA_TOPK = 3
PAGE_SIZE = 128
PAGES_PER_BLOCK = MOBA_BLOCK // PAGE_SIZE
N_MEM = 256
H_C, DH_C = 4, 256
D_C = H_C * DH_C
N_BRANCH = 3
D_IN = 4 * D_A + 3 * D_B + D_C + N_BRANCH * D_MODEL
EPS = 1e-6

OFF_QA, OFF_FA, OFF_IA, OFF_GA = 0, D_A, 2 * D_A, 3 * D_A
OFF_QB = 4 * D_A
OFF_KB = OFF_QB + D_B
OFF_VB = OFF_KB + D_B
OFF_QC = OFF_VB + D_B
OFF_GATE = OFF_QC + D_C

GLA_SUB = 16
GLA_SAFE_DROP = 80.0
SAMPLE_ROWS = 16
MOBA_HEADS_PER_STEP = 2
PAGE_MEAN_BLOCKS = 4
V7X_VMEM_LIMIT = 56 * 1024 * 1024

F32 = jnp.float32
BF16 = jnp.bfloat16
_NT = (((1,), (1,)), ((), ()))
_TN = (((0,), (0,)), ((), ()))


def _params(*semantics):
    return pltpu.CompilerParams(dimension_semantics=semantics, vmem_limit_bytes=V7X_VMEM_LIMIT)


def _tile(n, pref):
    t = min(n, pref)
    assert n % t == 0, (n, pref)
    return t


def _rms(x, gain):
    return x * lax.rsqrt(jnp.mean(x * x, axis=-1, keepdims=True) + EPS) * gain


def _rmsnorm_kernel(x_ref, g_ref, o_ref):
    o_ref[...] = _rms(x_ref[...], g_ref[...]).astype(o_ref.dtype)


def _rmsnorm(x, gains, layer):
    m, d = x.shape
    tm = _tile(m, 512)
    return pl.pallas_call(
        _rmsnorm_kernel,
        out_shape=jax.ShapeDtypeStruct((m, d), BF16),
        grid=(m // tm,),
        in_specs=[pl.BlockSpec((tm, d), lambda i: (i, 0)),
                  pl.BlockSpec((None, 1, d), lambda i: (layer, 0, 0))],
        out_specs=pl.BlockSpec((tm, d), lambda i: (i, 0)),
        compiler_params=_params("parallel"),
        name="rmsnorm",
    )(x, gains)


def _dot(a, b):
    return jnp.dot(a, b, preferred_element_type=F32)


def _mm_groups_kernel(*refs, n_groups, n_w, act_of_w, n_extra, epilogue):
    n_act = len(set(act_of_w))
    pos = 0
    acts = [refs[pos + g * n_act:pos + (g + 1) * n_act] for g in range(n_groups)]
    pos += n_groups * n_act
    weights = refs[pos:pos + n_w]
    pos += n_w
    extras = [refs[pos + g * n_extra:pos + (g + 1) * n_extra] for g in range(n_groups)]
    pos += n_groups * n_extra
    outs = refs[pos:pos + n_groups]
    wb = refs[pos + n_groups:]

    def compute(g):
        dots = [_dot(acts[g][act_of_w[k]][...], wb[k][...]) for k in range(n_w)]
        outs[g][...] = epilogue(dots, [e[...] for e in extras[g]]).astype(outs[g].dtype)

    @pl.when(pl.program_id(1) == 0)
    def _():
        for k in range(n_w):
            wb[k][...] = weights[k][...].astype(BF16)
        for g in range(1, n_groups):
            compute(g)

    compute(0)


def _mm_groups(acts, weights, layer, *, epilogue, out_dtype, tm, tn, act_of_w=None, extras=(), name):
    n_groups, n_w = len(acts), len(weights)
    act_of_w = tuple(range(n_w)) if act_of_w is None else tuple(act_of_w)
    k, n = weights[0].shape[1:]
    m = [a[0].shape[0] for a in acts]
    tm, tn = _tile(m[0], tm), _tile(n, tn)
    rows = [tm] + m[1:]

    def row_block(g):
        return (lambda j, i: (i, 0)) if g == 0 else (lambda j, i: (0, 0))

    def out_block(g, first=0):
        return (lambda j, i: (i, first + j)) if g == 0 else (lambda j, i: (0, first + j))

    in_specs, args = [], []
    for g in range(n_groups):
        in_specs += [pl.BlockSpec((rows[g], a.shape[1]), row_block(g)) for a in acts[g]]
        args += list(acts[g])
    in_specs += [pl.BlockSpec((None, k, tn), lambda j, i: (layer, 0, j))] * n_w
    args += list(weights)
    for g in range(n_groups):
        for arrays, col0 in extras:
            in_specs.append(pl.BlockSpec((rows[g], tn), out_block(g, col0 // tn)))
            args.append(arrays[g])
    return pl.pallas_call(
        functools.partial(_mm_groups_kernel, n_groups=n_groups, n_w=n_w, act_of_w=act_of_w,
                          n_extra=len(extras), epilogue=epilogue),
        out_shape=tuple(jax.ShapeDtypeStruct((m[g], n), out_dtype) for g in range(n_groups)),
        grid=(n // tn, m[0] // tm),
        in_specs=in_specs,
        out_specs=tuple(pl.BlockSpec((rows[g], tn), out_block(g)) for g in range(n_groups)),
        scratch_shapes=[pltpu.VMEM((k, tn), BF16)] * n_w,
        compiler_params=_params("arbitrary", "arbitrary"),
        name=name,
    )(*args)


def _ep_plain(dots, extras):
    return dots[0]


def _ep_residual(dots, extras):
    return extras[0] + dots[0]


def _ep_swiglu(dots, extras):
    return dots[0] * jax.nn.sigmoid(dots[0]) * dots[1]


def _ep_gated_sum(dots, extras):
    return functools.reduce(jnp.add, [jax.nn.sigmoid(g) * d for g, d in zip(extras, dots)])


def _lower_bounds_kernel(x_ref, o_ref):
    x = x_ref[...]
    e = jnp.exp(x - jnp.max(x, axis=0, keepdims=True))
    p = e / jnp.sum(e, axis=0, keepdims=True)
    rows = [jnp.zeros_like(p[0:1])]
    for r in range(1, x.shape[0]):
        rows.append(rows[-1] + p[r:r + 1])
    o_ref[...] = jnp.concatenate(rows, axis=0)


def _lower_bounds(lb_logits):
    return pl.pallas_call(
        _lower_bounds_kernel,
        out_shape=jax.ShapeDtypeStruct(lb_logits.shape, F32),
        name="hgrn2_lower_bounds",
    )(lb_logits)


def _hgrn2_pair_weights_factored(q, kk, G, g_mid):
    qt = (q * jnp.exp(G - g_mid)).astype(BF16)
    kt = (kk * jnp.exp(g_mid - G)).astype(BF16)
    a = lax.dot_general(qt, kt, _NT, preferred_element_type=F32)
    causal = lax.broadcasted_iota(jnp.int32, a.shape, 1) <= lax.broadcasted_iota(jnp.int32, a.shape, 0)
    return jnp.where(causal, a, 0.0).astype(BF16)


def _hgrn2_intra_exact(q, kk, v, G, rmod, chunk):
    pieces = [jnp.zeros((GLA_SUB, DV_A), F32)]
    for i in range(1, chunk // GLA_SUB):
        lo = i * GLA_SUB
        r = G[lo - 1:lo]
        qi = (q[lo:lo + GLA_SUB] * jnp.exp(G[lo:lo + GLA_SUB] - r)).astype(BF16)
        kj = (kk[:lo] * jnp.exp(r - G[:lo])).astype(BF16)
        a = lax.dot_general(qi, kj, _NT, preferred_element_type=F32)
        pieces.append(_dot(a.astype(BF16), v[:lo].astype(BF16)))
    o = jnp.concatenate(pieces, axis=0)
    for d in range(GLA_SUB):
        k_d = kk if d == 0 else pltpu.roll(kk, d, axis=0)
        g_d = G if d == 0 else pltpu.roll(G, d, axis=0)
        v_d = v if d == 0 else pltpu.roll(v, d, axis=0)
        decay = jnp.exp(jnp.where(rmod >= d, G - g_d, -jnp.inf))
        a_d = jnp.sum(q * k_d * decay, axis=-1, keepdims=True)
        o = o + a_d * v_d
    return o


def _hgrn2_kernel(q_ref, f_ref, i_ref, g_ref, lb_ref, gn_ref, s0_ref, o_ref, sout_ref,
                  st_ref, k_ref, qs_ref, G_ref, oi_ref, *, chunk, valid_len):
    c = pl.program_id(1)

    @pl.when(c == 0)
    def _():
        for h in range(H_A):
            st_ref[h] = s0_ref[h].T

    heads = [slice(h * DK_A, (h + 1) * DK_A) for h in range(H_A)]
    row = lax.broadcasted_iota(jnp.int32, (chunk, 1), 0)
    rmod = row % GLA_SUB
    tri = (lax.broadcasted_iota(jnp.int32, (chunk, chunk), 0)
           >= lax.broadcasted_iota(jnp.int32, (chunk, chunk), 1)).astype(F32)

    for sl in heads:
        zf = f_ref[:, sl]
        lb = lb_ref[:, sl]
        e = jnp.exp(-jnp.abs(zf))
        log_sig = jnp.minimum(zf, 0.0) - jnp.log1p(e)
        la = jnp.log(lb)
        lc = jnp.log1p(-lb) + log_sig
        log_f = jnp.maximum(la, lc) + jnp.log1p(jnp.exp(-jnp.abs(la - lc)))
        kk = (1.0 - lb) * (jnp.where(zf >= 0.0, e, 1.0) / (1.0 + e))
        if valid_len is not None:
            live = (c * chunk + row) < valid_len
            log_f = jnp.where(live, log_f, 0.0)
            kk = jnp.where(live, kk, 0.0)
        qr = q_ref[:, sl]
        k_ref[:, sl] = kk
        qs_ref[:, sl] = qr * jax.nn.sigmoid(qr) * (DK_A ** -0.5)
        G_ref[:, sl] = jnp.dot(tri, log_f, precision=lax.Precision.HIGHEST, preferred_element_type=F32)

    mid = chunk // 2
    g_mid = G_ref[mid - 1:mid, :]
    g_last = G_ref[chunk - 1:chunk, :]
    safe = jnp.max(jnp.maximum(-g_mid, g_mid - g_last)) < GLA_SAFE_DROP

    @pl.when(safe)
    def _():
        weights = [_hgrn2_pair_weights_factored(qs_ref[:, sl], k_ref[:, sl], G_ref[:, sl], g_mid[:, sl])
                   for sl in heads]
        for sl, a in zip(heads, weights):
            oi_ref[:, sl] = _dot(a, i_ref[:, sl].astype(BF16))

    @pl.when(jnp.logical_not(safe))
    def _():
        for sl in heads:
            oi_ref[:, sl] = _hgrn2_intra_exact(qs_ref[:, sl], k_ref[:, sl], i_ref[:, sl], G_ref[:, sl], rmod, chunk)

    carried = [lax.dot_general((qs_ref[:, sl] * jnp.exp(G_ref[:, sl])).astype(BF16), st_ref[h].astype(BF16),
                               _NT, preferred_element_type=F32) for h, sl in enumerate(heads)]
    added = [lax.dot_general(i_ref[:, sl].astype(BF16),
                             (k_ref[:, sl] * jnp.exp(g_last[:, sl] - G_ref[:, sl])).astype(BF16),
                             _TN, preferred_element_type=F32) for sl in heads]
    gn = gn_ref[...]
    for h, sl in enumerate(heads):
        st_ref[h] = st_ref[h] * jnp.exp(g_last[:, sl]) + added[h]
        gr = g_ref[:, sl]
        o = oi_ref[:, sl] + carried[h]
        o_ref[:, sl] = (_rms(o, gn) * (gr * jax.nn.sigmoid(gr))).astype(o_ref.dtype)

    @pl.when(c == pl.num_programs(1) - 1)
    def _():
        for h in range(H_A):
            sout_ref[h] = st_ref[h].T


def _hgrn2(z, lower, g_onorm, s0, layer, s0_layer, *, chunk, valid_len=None):
    b, l, _ = z.shape
    chunk = _tile(l, chunk)

    def col_spec(off):
        return pl.BlockSpec((None, chunk, D_A), lambda bi, c: (bi, c, off // D_A))

    return pl.pallas_call(
        functools.partial(_hgrn2_kernel, chunk=chunk, valid_len=valid_len),
        out_shape=(jax.ShapeDtypeStruct((b, l, D_A), BF16),
                   jax.ShapeDtypeStruct((b, H_A, DK_A, DV_A), F32)),
        grid=(b, l // chunk),
        in_specs=[col_spec(OFF_QA), col_spec(OFF_FA), col_spec(OFF_IA), col_spec(OFF_GA),
                  pl.BlockSpec((None, 1, D_A), lambda bi, c: (layer, 0, 0)),
                  pl.BlockSpec((None, 1, DV_A), lambda bi, c: (layer, 0, 0)),
                  pl.BlockSpec((None, None, H_A, DK_A, DV_A), lambda bi, c: (s0_layer, bi, 0, 0, 0))],
        out_specs=(pl.BlockSpec((None, chunk, D_A), lambda bi, c: (bi, c, 0)),
                   pl.BlockSpec((None, H_A, DK_A, DV_A), lambda bi, c: (bi, 0, 0, 0))),
        scratch_shapes=[pltpu.VMEM((H_A, DV_A, DK_A), F32)] + [pltpu.VMEM((chunk, D_A), F32)] * 4,
        compiler_params=_params("parallel", "arbitrary"),
        name="hgrn2",
    )(z, z, z, z, lower, g_onorm, s0)


def _head_norm_kernel(x_ref, g_ref, o_ref, *, heads, hd):
    g = g_ref[...]
    for h in range(heads):
        sl = slice(h * hd, (h + 1) * hd)
        o_ref[:, sl] = _rms(x_ref[:, sl], g)


def _head_norm(x, gains, layer, *, col_block, heads, hd):
    b, r, _ = x.shape
    width = heads * hd
    return pl.pallas_call(
        functools.partial(_head_norm_kernel, heads=heads, hd=hd),
        out_shape=jax.ShapeDtypeStruct((b, r, width), F32),
        grid=(b,),
        in_specs=[pl.BlockSpec((None, r, width), lambda bi: (bi, 0, col_block)),
                  pl.BlockSpec((None, 1, hd), lambda bi: (layer, 0, 0))],
        out_specs=pl.BlockSpec((None, r, width), lambda bi: (bi, 0, 0)),
        compiler_params=_params("parallel"),
        name="head_norm",
    )(x, gains)


def _moba_kv_kernel(k_ref, v_ref, g_ref, kn_ref, kb_ref, vt_ref, km_ref):
    g = g_ref[...]
    for h in range(H_B):
        sl = slice(h * DH_B, (h + 1) * DH_B)
        kn = _rms(k_ref[:, sl], g)
        kn_ref[:, sl] = kn
        kb_ref[:, sl] = kn.astype(BF16)
        km_ref[:, sl] = jnp.mean(kn, axis=0, keepdims=True)
        vt_ref[sl, :] = v_ref[:, sl].T.astype(BF16)


def _moba_kv(z, g_kb, layer):
    b, l, _ = z.shape
    nblk = l // MOBA_BLOCK
    act = lambda dtype: jax.ShapeDtypeStruct((b, l, D_B), dtype)
    act_spec = pl.BlockSpec((None, MOBA_BLOCK, D_B), lambda bi, n: (bi, n, 0))
    kn, kb, vt, km = pl.pallas_call(
        _moba_kv_kernel,
        out_shape=(act(F32), act(BF16), jax.ShapeDtypeStruct((b, D_B, l), BF16),
                   jax.ShapeDtypeStruct((b, nblk, 1, D_B), F32)),
        grid=(b, nblk),
        in_specs=[pl.BlockSpec((None, MOBA_BLOCK, D_B), lambda bi, n: (bi, n, OFF_KB // D_B)),
                  pl.BlockSpec((None, MOBA_BLOCK, D_B), lambda bi, n: (bi, n, OFF_VB // D_B)),
                  pl.BlockSpec((None, 1, DH_B), lambda bi, n: (layer, 0, 0))],
        out_specs=(act_spec, act_spec,
                   pl.BlockSpec((None, D_B, MOBA_BLOCK), lambda bi, n: (bi, 0, n)),
                   pl.BlockSpec((None, None, 1, D_B), lambda bi, n: (bi, n, 0, 0))),
        compiler_params=_params("parallel", "parallel"),
        name="moba_kv",
    )(z, z, g_kb)
    return kn, kb, vt, km.reshape(b, nblk, D_B)


def _topk_mask_t(gate_t, n_valid, topk):
    n, r = gate_t.shape
    rowi = lax.broadcasted_iota(jnp.int32, (n, r), 0)
    gm = jnp.where(rowi < n_valid, gate_t, -jnp.inf)
    sel = jnp.zeros((n, r), F32)
    for j in range(n):
        gj = gm[j:j + 1]
        beats = jnp.where(gm > gj, 1.0, jnp.where((gm == gj) & (rowi < j), 1.0, 0.0))
        rank = jnp.sum(beats, axis=0, keepdims=True)
        sel = jnp.where((rowi == j) & (rank < topk) & (rowi < n_valid), 1.0, sel)
    return sel


def _moba_prompt_kernel(slopes_ref, q_ref, gq_ref, k_ref, vt_ref, km_ref, o_ref, s_ref, qb_ref, sel_ref):
    hg = pl.program_id(1)
    i = pl.program_id(2)
    blk = MOBA_BLOCK
    scale = DH_B ** -0.5
    heads = [slice(hh * DH_B, (hh + 1) * DH_B) for hh in range(MOBA_HEADS_PER_STEP)]

    @pl.when(i == 0)
    def _():
        for hh, sl in enumerate(heads):
            qn = _rms(q_ref[:, sl], gq_ref[...])
            qb_ref[:, sl] = qn.astype(BF16)
            gate_t = lax.dot_general(km_ref[:, sl], qn, _NT, precision=lax.Precision.HIGHEST,
                                     preferred_element_type=F32)
            q_block = lax.broadcasted_iota(jnp.int32, gate_t.shape, 1) // blk
            sel_ref[hh] = _topk_mask_t(gate_t, q_block, MOBA_TOPK)

    d0 = (lax.broadcasted_iota(jnp.int32, (blk, blk), 1)
          - lax.broadcasted_iota(jnp.int32, (blk, blk), 0)).astype(F32)
    slopes = [slopes_ref[hg * MOBA_HEADS_PER_STEP + hh] for hh in range(MOBA_HEADS_PER_STEP)]
    bias0 = [-slope * d0 for slope in slopes]

    def attend(own):
        cols = slice(own * blk, (own + 1) * blk)
        m = [None] * len(heads)
        for j in range(own + 1):
            rows = slice(j * blk, (j + 1) * blk)
            for hh, sl in enumerate(heads):
                s = lax.dot_general(k_ref[rows, sl], qb_ref[cols, sl], _NT, preferred_element_type=F32) * scale
                s = s + (bias0[hh] - slopes[hh] * float((own - j) * blk))
                keep = (d0 >= 0.0) if j == own else (sel_ref[hh, j:j + 1, cols] > 0.0)
                s = jnp.where(keep, s, -jnp.inf)
                s_ref[hh, rows, :] = s
                m_j = jnp.max(s, axis=0, keepdims=True)
                m[hh] = m_j if m[hh] is None else jnp.maximum(m[hh], m_j)
        l = [jnp.zeros((1, blk), F32) for _ in heads]
        acc = [jnp.zeros((DH_B, blk), F32) for _ in heads]
        for j in range(own + 1):
            rows = slice(j * blk, (j + 1) * blk)
            for hh, sl in enumerate(heads):
                p = jnp.exp(s_ref[hh, rows, :] - m[hh])
                l[hh] = l[hh] + jnp.sum(p, axis=0, keepdims=True)
                acc[hh] = acc[hh] + _dot(vt_ref[sl, rows], p.astype(BF16))
        for hh, sl in enumerate(heads):
            o_ref[:, sl] = (acc[hh] / l[hh]).T.astype(o_ref.dtype)

    for own in range(km_ref.shape[0]):
        pl.when(i == own)(functools.partial(attend, own))


def _moba_prompt(z, g_qb, kb, vt, kmean, slopes, layer):
    b, l, _ = z.shape
    nblk = l // MOBA_BLOCK
    width = MOBA_HEADS_PER_STEP * DH_B
    return pl.pallas_call(
        _moba_prompt_kernel,
        out_shape=jax.ShapeDtypeStruct((b, l, D_B), BF16),
        grid=(b, H_B // MOBA_HEADS_PER_STEP, nblk),
        in_specs=[pl.BlockSpec(memory_space=pltpu.SMEM),
                  pl.BlockSpec((None, l, width), lambda bi, h, i: (bi, 0, OFF_QB // width + h)),
                  pl.BlockSpec((None, 1, DH_B), lambda bi, h, i: (layer, 0, 0)),
                  pl.BlockSpec((None, l, width), lambda bi, h, i: (bi, 0, h)),
                  pl.BlockSpec((None, width, l), lambda bi, h, i: (bi, h, 0)),
                  pl.BlockSpec((None, nblk, width), lambda bi, h, i: (bi, 0, h))],
        out_specs=pl.BlockSpec((None, MOBA_BLOCK, width), lambda bi, h, i: (bi, i, h)),
        scratch_shapes=[pltpu.VMEM((MOBA_HEADS_PER_STEP, l, MOBA_BLOCK), F32), pltpu.VMEM((l, width), BF16),
                        pltpu.VMEM((MOBA_HEADS_PER_STEP, nblk, l), F32)],
        compiler_params=_params("parallel", "parallel", "arbitrary"),
        name="moba_prompt",
    )(slopes, z, g_qb, kb, vt, kmean)


def _page_mean_kernel(pt_ref, *refs):
    del pt_ref
    page_refs, o_ref = refs[:-1], refs[-1]
    for n in range(PAGE_MEAN_BLOCKS):
        pages = page_refs[n * PAGES_PER_BLOCK:(n + 1) * PAGES_PER_BLOCK]
        total = jnp.sum(pages[0][...], axis=0)
        for r in pages[1:]:
            total = total + jnp.sum(r[...], axis=0)
        o_ref[n] = total * (1.0 / MOBA_BLOCK)


def _page_means(cache_k, page_table_flat, layer, n_seq, n_pages):
    n_blocks = n_pages // PAGES_PER_BLOCK
    assert n_blocks % PAGE_MEAN_BLOCKS == 0
    pages_per_step = PAGE_MEAN_BLOCKS * PAGES_PER_BLOCK

    def page_spec(p):
        return pl.BlockSpec(
            (None, None, PAGE_SIZE, H_B, DH_B),
            lambda bi, n, pt: (layer, pt[bi * n_pages + n * pages_per_step + p], 0, 0, 0))

    return pl.pallas_call(
        _page_mean_kernel,
        out_shape=jax.ShapeDtypeStruct((n_seq, n_blocks, H_B, DH_B), F32),
        grid_spec=pltpu.PrefetchScalarGridSpec(
            num_scalar_prefetch=1,
            grid=(n_seq, n_blocks // PAGE_MEAN_BLOCKS),
            in_specs=[page_spec(p) for p in range(pages_per_step)],
            out_specs=pl.BlockSpec((None, PAGE_MEAN_BLOCKS, H_B, DH_B), lambda bi, n, pt: (bi, n, 0, 0))),
        compiler_params=_params("parallel", "parallel"),
        name="moba_page_means",
    )(page_table_flat, *([cache_k] * pages_per_step))


def _moba_select_kernel(q_ref, gq_ref, km_ref, qn_ref, sel_ref):
    rows = q_ref.shape[0]
    n_blocks = km_ref.shape[0]
    g = gq_ref[...]
    col = lax.broadcasted_iota(jnp.int32, (rows, n_blocks), 1)
    lane = lax.broadcasted_iota(jnp.int32, (rows, sel_ref.shape[-1]), 1)
    for h in range(H_B):
        sl = slice(h * DH_B, (h + 1) * DH_B)
        qn = _rms(q_ref[:, sl], g)
        qn_ref[:, sl] = qn
        gate = lax.dot_general(qn, km_ref[:, sl], _NT, precision=lax.Precision.HIGHEST,
                               preferred_element_type=F32)
        picks = jnp.zeros(lane.shape, jnp.int32)
        for r in range(MOBA_TOPK):
            best = jnp.max(gate, axis=-1, keepdims=True)
            idx = jnp.min(jnp.where(gate == best, col, n_blocks), axis=-1, keepdims=True)
            picks = jnp.where(lane == r, idx, picks)
            gate = jnp.where(col == idx, -jnp.inf, gate)
        sel_ref[h] = picks


def _moba_select(z, g_qb, kmean, layer):
    b, rows, _ = z.shape
    n_blocks = kmean.shape[1]
    return pl.pallas_call(
        _moba_select_kernel,
        out_shape=(jax.ShapeDtypeStruct((b, rows, D_B), F32),
                   jax.ShapeDtypeStruct((b, H_B, rows, 128), jnp.int32)),
        grid=(b,),
        in_specs=[pl.BlockSpec((None, rows, D_B), lambda bi: (bi, 0, OFF_QB // D_B)),
                  pl.BlockSpec((None, 1, DH_B), lambda bi: (layer, 0, 0)),
                  pl.BlockSpec((None, n_blocks, D_B), lambda bi: (bi, 0, 0))],
        out_specs=(pl.BlockSpec((None, rows, D_B), lambda bi: (bi, 0, 0)),
                   pl.BlockSpec((None, H_B, rows, 128), lambda bi: (bi, 0, 0, 0))),
        compiler_params=_params("parallel"),
        name="moba_select",
    )(z, g_qb, kmean)


def _moba_sample_kernel(sel_ref, pt_ref, slopes_ref, q_ref, kown_ref, vown_ref, ck_ref, cv_ref, o_ref,
                        kbuf, vbuf, sem, *, layer, n_tok, n_pages):
    step = pl.program_id(0)
    n_steps = pl.num_programs(0)
    tiles_per_tok = MOBA_TOPK * PAGES_PER_BLOCK
    past_len = n_pages * PAGE_SIZE

    def picked_block(st, t, s):
        return sel_ref[(st * n_tok + t) * MOBA_TOPK + s]

    def tile_copies(st, slot):
        bi, h = st // H_B, st % H_B
        copies = []
        for t in range(n_tok):
            for s in range(MOBA_TOPK):
                block = picked_block(st, t, s)
                for p in range(PAGES_PER_BLOCK):
                    page = pt_ref[bi * n_pages + block * PAGES_PER_BLOCK + p]
                    j = t * tiles_per_tok + s * PAGES_PER_BLOCK + p
                    copies.append(pltpu.make_async_copy(
                        ck_ref.at[layer, page, :, h, :], kbuf.at[slot, j], sem.at[slot]))
                    copies.append(pltpu.make_async_copy(
                        cv_ref.at[layer, page, :, h, :], vbuf.at[slot, j], sem.at[slot]))
        return copies

    slot = step % 2

    @pl.when(step == 0)
    def _():
        for c in tile_copies(step, slot):
            c.start()

    @pl.when(step + 1 < n_steps)
    def _():
        for c in tile_copies(step + 1, 1 - slot):
            c.start()

    for c in tile_copies(step, slot):
        c.wait()

    slope = slopes_ref[step % H_B]
    scale = DH_B ** -0.5
    rows = kown_ref.shape[0]
    k_own, v_own = kown_ref[...], vown_ref[...]
    own_pos = past_len + lax.broadcasted_iota(jnp.int32, (rows, 1), 0)
    page_row = lax.broadcasted_iota(jnp.int32, (PAGE_SIZE, 1), 0)
    out_row = lax.broadcasted_iota(jnp.int32, (rows, 1), 0)
    out = jnp.zeros((rows, DH_B), F32)
    for t in range(n_tok):
        q = q_ref[t:t + 1, :]
        t_pos = past_len + t

        def score(keys, k_pos):
            return (jnp.sum(keys * q, axis=-1, keepdims=True) * scale
                    - slope * (t_pos - k_pos).astype(F32))

        scores = [jnp.where(own_pos <= t_pos, score(k_own, own_pos), -jnp.inf)]
        values = [v_own]
        for s in range(MOBA_TOPK):
            block = picked_block(step, t, s)
            for p in range(PAGES_PER_BLOCK):
                j = t * tiles_per_tok + s * PAGES_PER_BLOCK + p
                scores.append(score(kbuf[slot, j], block * MOBA_BLOCK + p * PAGE_SIZE + page_row))
                values.append(vbuf[slot, j])
        m = functools.reduce(jnp.maximum, [jnp.max(s, axis=0, keepdims=True) for s in scores])
        probs = [jnp.exp(s - m) for s in scores]
        l = functools.reduce(jnp.add, [jnp.sum(p, axis=0, keepdims=True) for p in probs])
        acc = functools.reduce(jnp.add, [jnp.sum(p * v, axis=0, keepdims=True) for p, v in zip(probs, values)])
        out = jnp.where(out_row == t, acc / l, out)
    o_ref[...] = out.astype(o_ref.dtype)


def _moba_sample(qn, kn, z, cache_k, cache_v, picks_flat, page_table_flat, slopes, layer, *, n_tok, n_pages):
    b, rows, _ = qn.shape
    n_tiles = n_tok * MOBA_TOPK * PAGES_PER_BLOCK
    own_spec = pl.BlockSpec((None, rows, DH_B), lambda st, sel, pt: (st // H_B, 0, st % H_B))
    return pl.pallas_call(
        functools.partial(_moba_sample_kernel, layer=layer, n_tok=n_tok, n_pages=n_pages),
        out_shape=jax.ShapeDtypeStruct((b, rows, D_B), BF16),
        grid_spec=pltpu.PrefetchScalarGridSpec(
            num_scalar_prefetch=2,
            grid=(b * H_B,),
            in_specs=[pl.BlockSpec(memory_space=pltpu.SMEM), own_spec, own_spec,
                      pl.BlockSpec((None, rows, DH_B),
                                   lambda st, sel, pt: (st // H_B, 0, OFF_VB // DH_B + st % H_B)),
                      pl.BlockSpec(memory_space=pl.ANY), pl.BlockSpec(memory_space=pl.ANY)],
            out_specs=pl.BlockSpec((None, rows, DH_B), lambda st, sel, pt: (st // H_B, 0, st % H_B)),
            scratch_shapes=[pltpu.VMEM((2, n_tiles, PAGE_SIZE, DH_B), F32),
                            pltpu.VMEM((2, n_tiles, PAGE_SIZE, DH_B), F32),
                            pltpu.SemaphoreType.DMA((2,))]),
        compiler_params=_params("arbitrary"),
        name="moba_sample",
    )(picks_flat, page_table_flat, slopes, qn, kn, z, cache_k, cache_v)


def _mem_attn_kernel(q_ref, gq_ref, k_ref, v_ref, o_ref):
    qn = _rms(q_ref[...], gq_ref[...])
    s = lax.dot_general(qn.astype(BF16), k_ref[...].astype(BF16), _NT, preferred_element_type=F32)
    s = s * (DH_C ** -0.5)
    e = jnp.exp(s - jnp.max(s, axis=-1, keepdims=True))
    p = e / jnp.sum(e, axis=-1, keepdims=True)
    o_ref[...] = _dot(p.astype(BF16), v_ref[...].astype(BF16)).astype(o_ref.dtype)


def _mem_attn(z, g_qc, mem_k, mem_v, layer, *, kv_index, v_col0, tq):
    b, l, _ = z.shape
    tq = _tile(l, tq)
    k_block = (None,) * (mem_k.ndim - 2) + (N_MEM, DH_C)
    v_block = (None,) * (mem_v.ndim - 2) + (N_MEM, DH_C)
    return pl.pallas_call(
        _mem_attn_kernel,
        out_shape=jax.ShapeDtypeStruct((b, l, D_C), BF16),
        grid=(b, H_C, l // tq),
        in_specs=[pl.BlockSpec((None, tq, DH_C), lambda bi, h, i: (bi, i, OFF_QC // DH_C + h)),
                  pl.BlockSpec((None, 1, DH_C), lambda bi, h, i: (layer, 0, 0)),
                  pl.BlockSpec(k_block, lambda bi, h, i: kv_index(bi, h, 0)),
                  pl.BlockSpec(v_block, lambda bi, h, i: kv_index(bi, h, v_col0))],
        out_specs=pl.BlockSpec((None, tq, DH_C), lambda bi, h, i: (bi, i, h)),
        compiler_params=_params("parallel", "parallel", "arbitrary"),
        name="mem_attn",
    )(z, g_qc, mem_k, mem_v)


def _mixer_and_ffn(xs, layer, w, branches):
    shapes = [x.shape[:2] for x in xs]
    x2 = [x.reshape(-1, D_MODEL) for x in xs]
    h = [_rmsnorm(x, w["norm_mix"], layer) for x in x2]
    z2 = _mm_groups([[a] for a in h], [w["w_in"]], layer, epilogue=_ep_plain, out_dtype=F32,
                    tm=1024, tn=1024, name="mm_in")
    outs = [fn(z.reshape(*s, D_IN)) for fn, z, s in zip(branches, z2, shapes)]
    gates = [(z2, OFF_GATE + k * D_MODEL) for k in range(N_BRANCH)]
    merged = _mm_groups([list(o[:N_BRANCH]) for o in outs], [w["w_br_a"], w["w_br_b"], w["w_br_c"]], layer,
                        epilogue=_ep_gated_sum, extras=gates, out_dtype=BF16, tm=1024, tn=512, name="mm_merge")
    x2 = _mm_groups([[a] for a in merged], [w["w_out"]], layer, epilogue=_ep_residual, extras=[(x2, 0)],
                    out_dtype=F32, tm=1024, tn=1024, name="mm_out")
    h = [_rmsnorm(x, w["norm_ffn"], layer) for x in x2]
    act = _mm_groups([[a] for a in h], [w["w_gate"], w["w_up"]], layer, act_of_w=(0, 0), epilogue=_ep_swiglu,
                     out_dtype=BF16, tm=1024, tn=512, name="mm_swiglu")
    x2 = _mm_groups([[a] for a in act], [w["w_down"]], layer, epilogue=_ep_residual, extras=[(x2, 0)],
                    out_dtype=F32, tm=512, tn=512, name="mm_down")
    return ([x.reshape(*s, D_MODEL) for x, s in zip(x2, shapes)],
            [z.reshape(*s, D_IN) for z, s in zip(z2, shapes)], [o[N_BRANCH] for o in outs])


def kernel(x_prompt, x_sample, state_hgrn, cache_k, cache_v, cache_mem_k, cache_mem_v, page_table,
           mem_prompt, lb_logits, norm_mix, w_in, norm_o_a, norm_q_b, norm_k_b, norm_q_c, norm_k_c,
           norm_mem, w_mem_kv, w_br_a, w_br_b, w_br_c, w_out, norm_ffn, w_gate, w_up, w_down):
    depth = w_in.shape[0]
    n_prompt, seq, _ = x_prompt.shape
    n_seq, n_tok, _ = x_sample.shape
    n_pages = page_table.shape[1]
    assert seq % MOBA_BLOCK == 0 and n_pages % PAGES_PER_BLOCK == 0
    assert n_pages // PAGES_PER_BLOCK >= MOBA_TOPK and n_tok <= SAMPLE_ROWS <= MOBA_BLOCK

    row = lambda g: g.reshape(depth, 1, g.shape[-1])
    w = {"norm_mix": row(norm_mix), "norm_ffn": row(norm_ffn), "w_in": w_in, "w_br_a": w_br_a, "w_br_b": w_br_b,
         "w_br_c": w_br_c, "w_out": w_out, "w_gate": w_gate, "w_up": w_up, "w_down": w_down}
    g_oa, g_qb, g_kb, g_qc, g_kc, g_mem = (row(g) for g in (norm_o_a, norm_q_b, norm_k_b, norm_q_c,
                                                           norm_k_c, norm_mem))
    lower = _lower_bounds(lb_logits).reshape(depth, 1, D_A)
    slopes = 2.0 ** (-8.0 * jnp.arange(1, H_B + 1, dtype=F32) / H_B)

    cmk = cache_mem_k.reshape(depth, n_seq, N_MEM, D_C)
    cmv = cache_mem_v.reshape(depth, n_seq, N_MEM, D_C)
    pt_flat = page_table.reshape(-1)
    s0_prompt = jnp.zeros((1, n_prompt, H_A, DK_A, DV_A), F32)
    mem2 = mem_prompt.reshape(n_prompt * N_MEM, D_MODEL)

    xp = x_prompt
    xs = jnp.pad(x_sample, ((0, 0), (0, SAMPLE_ROWS - n_tok), (0, 0)))
    outs = {k: [] for k in ("hp", "kp", "vp", "mkp", "mvp", "hs", "ks", "vs")}

    for layer in range(depth):
        (kv,) = _mm_groups([[_rmsnorm(mem2, g_mem, layer)]], [w_mem_kv], layer, epilogue=_ep_plain,
                           out_dtype=F32, tm=1024, tn=1024, name="mm_mem")
        kv = kv.reshape(n_prompt, N_MEM, 2 * D_C)
        mem_k = _head_norm(kv, g_kc, layer, col_block=0, heads=H_C, hd=DH_C)
        mem_v = kv[:, :, D_C:]

        def prompt_branches(z):
            o_a, s_new = _hgrn2(z, lower, g_oa, s0_prompt, layer, 0, chunk=64)
            kn, kb, vt, kmean = _moba_kv(z, g_kb, layer)
            o_b = _moba_prompt(z, g_qb, kb, vt, kmean, slopes, layer)
            o_c = _mem_attn(z, g_qc, mem_k, kv, layer, tq=2048, v_col0=D_C // DH_C,
                            kv_index=lambda bi, h, c0: (bi, 0, c0 + h))
            m = n_prompt * seq
            return o_a.reshape(m, D_A), o_b.reshape(m, D_B), o_c.reshape(m, D_C), (s_new, kn)

        def sample_branches(z):
            o_a, s_new = _hgrn2(z, lower, g_oa, state_hgrn, layer, layer, chunk=SAMPLE_ROWS, valid_len=n_tok)
            kn = _head_norm(z, g_kb, layer, col_block=OFF_KB // D_B, heads=H_B, hd=DH_B)
            kmean = _page_means(cache_k, pt_flat, layer, n_seq, n_pages).reshape(n_seq, -1, D_B)
            qn, picks = _moba_select(z, g_qb, kmean, layer)
            picks_flat = picks[:, :, :n_tok, :MOBA_TOPK].reshape(-1)
            o_b = _moba_sample(qn, kn, z, cache_k, cache_v, picks_flat, pt_flat, slopes, layer,
                               n_tok=n_tok, n_pages=n_pages)
            o_c = _mem_attn(z, g_qc, cmk, cmv, layer, tq=SAMPLE_ROWS, v_col0=0,
                            kv_index=lambda bi, h, c0: (layer, bi, 0, c0 + h))
            m = n_seq * SAMPLE_ROWS
            return o_a.reshape(m, D_A), o_b.reshape(m, D_B), o_c.reshape(m, D_C), (s_new, kn)

        (xp, xs), (zp, zs), ((s_p, k_p), (s_s, k_s)) = _mixer_and_ffn(
            [xp, xs], layer, w, [prompt_branches, sample_branches])

        outs["hp"].append(s_p)
        outs["kp"].append(k_p.reshape(n_prompt, seq, H_B, DH_B))
        outs["vp"].append(zp[:, :, OFF_VB:OFF_VB + D_B].reshape(n_prompt, seq, H_B, DH_B))
        outs["mkp"].append(mem_k.reshape(n_prompt, N_MEM, H_C, DH_C))
        outs["mvp"].append(mem_v.reshape(n_prompt, N_MEM, H_C, DH_C))
        outs["hs"].append(s_s)
        outs["ks"].append(k_s[:, :n_tok].reshape(n_seq, n_tok, H_B, DH_B))
        outs["vs"].append(zs[:, :n_tok, OFF_VB:OFF_VB + D_B].reshape(n_seq, n_tok, H_B, DH_B))

    stack = lambda k: jnp.stack(outs[k])
    return (xp, xs[:, :n_tok], stack("hp"), stack("kp"), stack("vp"), stack("mkp"), stack("mvp"),
            stack("hs"), stack("ks"), stack("vs"))
```

```python
import functools

import jax
import jax.numpy as jnp
from jax import lax
from jax.experimental import pallas as pl
from jax.experimental.pallas import tpu as pltpu

D_MODEL = 2048
H_A, DK_A, DV_A = 8, 128, 128
D_A = H_A * DK_A
H_B, DH_B = 8, 128
D_B = H_B * DH_B
MOBA_BLOCK = 256
MOBA_TOPK = 3
PAGE_SIZE = 128
PAGES_PER_BLOCK = MOBA_BLOCK // PAGE_SIZE
N_MEM = 256
H_C, DH_C = 4, 256
D_C = H_C * DH_C
N_BRANCH = 3
D_IN = 4 * D_A + 3 * D_B + D_C + N_BRANCH * D_MODEL
EPS = 1e-6

OFF_QA, OFF_FA, OFF_IA, OFF_GA = 0, D_A, 2 * D_A, 3 * D_A
OFF_QB = 4 * D_A
OFF_KB = OFF_QB + D_B
OFF_VB = OFF_KB + D_B
OFF_QC = OFF_VB + D_B
OFF_GATE = OFF_QC + D_C

GLA_SUB = 16
GLA_SAFE_DROP = 80.0
SAMPLE_ROWS = 16
MOBA_HEADS_PER_STEP = 2
PAGE_MEAN_BLOCKS = 4
V7X_VMEM_LIMIT = 56 * 1024 * 1024

F32 = jnp.float32
BF16 = jnp.bfloat16
_NT = (((1,), (1,)), ((), ()))
_TN = (((0,), (0,)), ((), ()))


def _params(*semantics):
    return pltpu.CompilerParams(dimension_semantics=semantics, vmem_limit_bytes=V7X_VMEM_LIMIT)


def _tile(n, pref):
    t = min(n, pref)
    assert n % t == 0, (n, pref)
    return t


def _rms(x, gain):
    return x * lax.rsqrt(jnp.mean(x * x, axis=-1, keepdims=True) + EPS) * gain


def _rmsnorm_kernel(x_ref, g_ref, o_ref):
    o_ref[...] = _rms(x_ref[...], g_ref[...]).astype(o_ref.dtype)


def _rmsnorm(x, gains, layer):
    m, d = x.shape
    tm = _tile(m, 512)
    return pl.pallas_call(
        _rmsnorm_kernel,
        out_shape=jax.ShapeDtypeStruct((m, d), BF16),
        grid=(m // tm,),
        in_specs=[pl.BlockSpec((tm, d), lambda i: (i, 0)),
                  pl.BlockSpec((None, 1, d), lambda i: (layer, 0, 0))],
        out_specs=pl.BlockSpec((tm, d), lambda i: (i, 0)),
        compiler_params=_params("parallel"),
        name="rmsnorm",
    )(x, gains)


def _dot(a, b):
    return jnp.dot(a, b, preferred_element_type=F32)


def _mm_groups_kernel(*refs, n_groups, n_w, act_of_w, n_extra, epilogue, side=None):
    n_act = len(set(act_of_w))
    pos = 0
    acts = [refs[pos + g * n_act:pos + (g + 1) * n_act] for g in range(n_groups)]
    pos += n_groups * n_act
    weights = refs[pos:pos + n_w]
    pos += n_w
    extras = [refs[pos + g * n_extra:pos + (g + 1) * n_extra] for g in range(n_groups)]
    pos += n_groups * n_extra
    outs = refs[pos:pos + n_groups]
    wb = refs[pos + n_groups:]

    def compute(g):
        dots = [_dot(acts[g][act_of_w[k]][...], wb[k][...]) for k in range(n_w)]
        outs[g][...] = epilogue(dots, [e[...] for e in extras[g]]).astype(outs[g].dtype)

    if side is not None:
        side[0]()

    @pl.when(pl.program_id(1) == 0)
    def _():
        for k in range(n_w):
            wb[k][...] = weights[k][...].astype(BF16)
        for g in range(1, n_groups):
            compute(g)

    compute(0)
    if side is not None:
        side[1]()


def _page_mean_rider(pt_ref, cache_ref, km_ref, buf, sem, *, layer, n_seq, n_pages, per_step):
    blocks_per_seq = n_pages // PAGES_PER_BLOCK
    last_block = n_seq * blocks_per_seq - 1
    step = pl.program_id(0) * pl.num_programs(1) + pl.program_id(1)
    n_steps = pl.num_programs(0) * pl.num_programs(1)
    slot = step % 2

    def block_of(st, r):
        g = jnp.minimum(st * per_step + r, last_block)
        return g // blocks_per_seq, g % blocks_per_seq

    def page_copies(st, into):
        copies = []
        for r in range(per_step):
            seq, blk = block_of(st, r)
            for p in range(PAGES_PER_BLOCK):
                page = pt_ref[seq * n_pages + blk * PAGES_PER_BLOCK + p]
                copies.append(pltpu.make_async_copy(cache_ref.at[layer, page],
                                                    buf.at[into, r * PAGES_PER_BLOCK + p], sem.at[into]))
        return copies

    def before():
        @pl.when(step == 0)
        def _():
            for c in page_copies(step, slot):
                c.start()

        @pl.when(step + 1 < n_steps)
        def _():
            for c in page_copies(step + 1, 1 - slot):
                c.start()

        for c in page_copies(step, slot):
            c.wait()

    def beside():
        for r in range(per_step):
            seq, blk = block_of(step, r)
            total = jnp.sum(buf[slot, r * PAGES_PER_BLOCK], axis=0)
            for p in range(1, PAGES_PER_BLOCK):
                total = total + jnp.sum(buf[slot, r * PAGES_PER_BLOCK + p], axis=0)
            km_ref[seq, blk] = total * (1.0 / MOBA_BLOCK)

    return before, beside


def _mm_groups_rider_kernel(pt_ref, *refs, n_in, n_out, n_w, mm_kwargs, rider_kwargs):
    ins, cache_ref = refs[:n_in], refs[n_in]
    outs, km_ref = refs[n_in + 1:n_in + 1 + n_out], refs[n_in + 1 + n_out]
    wb = refs[n_in + 2 + n_out:n_in + 2 + n_out + n_w]
    buf, sem = refs[n_in + 2 + n_out + n_w:]
    side = _page_mean_rider(pt_ref, cache_ref, km_ref, buf, sem, **rider_kwargs)
    _mm_groups_kernel(*ins, *outs, *wb, side=side, **mm_kwargs)


def _mm_groups(acts, weights, layer, *, epilogue, out_dtype, tm, tn, act_of_w=None, extras=(), name,
               page_means=None):
    n_groups, n_w = len(acts), len(weights)
    act_of_w = tuple(range(n_w)) if act_of_w is None else tuple(act_of_w)
    k, n = weights[0].shape[1:]
    m = [a[0].shape[0] for a in acts]
    tm, tn = _tile(m[0], tm), _tile(n, tn)
    rows = [tm] + m[1:]
    grid = (n // tn, m[0] // tm)

    def row_block(g):
        return (lambda j, i, *_: (i, 0)) if g == 0 else (lambda j, i, *_: (0, 0))

    def out_block(g, first=0):
        return (lambda j, i, *_: (i, first + j)) if g == 0 else (lambda j, i, *_: (0, first + j))

    in_specs, args = [], []
    for g in range(n_groups):
        in_specs += [pl.BlockSpec((rows[g], a.shape[1]), row_block(g)) for a in acts[g]]
        args += list(acts[g])
    in_specs += [pl.BlockSpec((None, k, tn), lambda j, i, *_: (layer, 0, j))] * n_w
    args += list(weights)
    for g in range(n_groups):
        for arrays, col0 in extras:
            in_specs.append(pl.BlockSpec((rows[g], tn), out_block(g, col0 // tn)))
            args.append(arrays[g])
    out_shape = [jax.ShapeDtypeStruct((m[g], n), out_dtype) for g in range(n_groups)]
    out_specs = [pl.BlockSpec((rows[g], tn), out_block(g)) for g in range(n_groups)]
    scratch = [pltpu.VMEM((k, tn), BF16)] * n_w
    mm_kwargs = dict(n_groups=n_groups, n_w=n_w, act_of_w=act_of_w, n_extra=len(extras), epilogue=epilogue)
    if page_means is None:
        return pl.pallas_call(
            functools.partial(_mm_groups_kernel, **mm_kwargs),
            out_shape=tuple(out_shape), grid=grid, in_specs=in_specs, out_specs=tuple(out_specs),
            scratch_shapes=scratch, compiler_params=_params("arbitrary", "arbitrary"), name=name,
        )(*args)

    cache_k, pt_flat, cache_layer, n_seq, n_pages = page_means
    n_blocks = n_pages // PAGES_PER_BLOCK
    per_step = -(-n_seq * n_blocks // (grid[0] * grid[1]))
    km_shape = (n_seq, n_blocks, H_B, DH_B)
    return pl.pallas_call(
        functools.partial(_mm_groups_rider_kernel, n_in=len(args), n_out=n_groups, n_w=n_w, mm_kwargs=mm_kwargs,
                          rider_kwargs=dict(layer=cache_layer, n_seq=n_seq, n_pages=n_pages, per_step=per_step)),
        out_shape=tuple(out_shape) + (jax.ShapeDtypeStruct(km_shape, F32),),
        grid_spec=pltpu.PrefetchScalarGridSpec(
            num_scalar_prefetch=1, grid=grid,
            in_specs=in_specs + [pl.BlockSpec(memory_space=pl.ANY)],
            out_specs=tuple(out_specs) + (pl.BlockSpec(km_shape, lambda j, i, *_: (0, 0, 0, 0)),),
            scratch_shapes=scratch + [pltpu.VMEM((2, per_step * PAGES_PER_BLOCK, PAGE_SIZE, H_B, DH_B), F32),
                                      pltpu.SemaphoreType.DMA((2,))]),
        compiler_params=_params("arbitrary", "arbitrary"), name=name,
    )(pt_flat, *args, cache_k)


def _ep_plain(dots, extras):
    return dots[0]


def _ep_residual(dots, extras):
    return extras[0] + dots[0]


def _ep_swiglu(dots, extras):
    return dots[0] * jax.nn.sigmoid(dots[0]) * dots[1]


def _ep_gated_sum(dots, extras):
    return functools.reduce(jnp.add, [jax.nn.sigmoid(g) * d for g, d in zip(extras, dots)])


def _lower_bounds_kernel(x_ref, o_ref):
    x = x_ref[...]
    e = jnp.exp(x - jnp.max(x, axis=0, keepdims=True))
    p = e / jnp.sum(e, axis=0, keepdims=True)
    rows = [jnp.zeros_like(p[0:1])]
    for r in range(1, x.shape[0]):
        rows.append(rows[-1] + p[r:r + 1])
    o_ref[...] = jnp.concatenate(rows, axis=0)


def _lower_bounds(lb_logits):
    return pl.pallas_call(
        _lower_bounds_kernel,
        out_shape=jax.ShapeDtypeStruct(lb_logits.shape, F32),
        name="hgrn2_lower_bounds",
    )(lb_logits)


def _hgrn2_pair_weights_factored(q, kk, G, g_mid):
    qt = (q * jnp.exp(G - g_mid)).astype(BF16)
    kt = (kk * jnp.exp(g_mid - G)).astype(BF16)
    a = lax.dot_general(qt, kt, _NT, preferred_element_type=F32)
    causal = lax.broadcasted_iota(jnp.int32, a.shape, 1) <= lax.broadcasted_iota(jnp.int32, a.shape, 0)
    return jnp.where(causal, a, 0.0).astype(BF16)


def _hgrn2_intra_exact(q, kk, v, G, rmod, chunk):
    pieces = [jnp.zeros((GLA_SUB, DV_A), F32)]
    for i in range(1, chunk // GLA_SUB):
        lo = i * GLA_SUB
        r = G[lo - 1:lo]
        qi = (q[lo:lo + GLA_SUB] * jnp.exp(G[lo:lo + GLA_SUB] - r)).astype(BF16)
        kj = (kk[:lo] * jnp.exp(r - G[:lo])).astype(BF16)
        a = lax.dot_general(qi, kj, _NT, preferred_element_type=F32)
        pieces.append(_dot(a.astype(BF16), v[:lo].astype(BF16)))
    o = jnp.concatenate(pieces, axis=0)
    for d in range(GLA_SUB):
        k_d = kk if d == 0 else pltpu.roll(kk, d, axis=0)
        g_d = G if d == 0 else pltpu.roll(G, d, axis=0)
        v_d = v if d == 0 else pltpu.roll(v, d, axis=0)
        decay = jnp.exp(jnp.where(rmod >= d, G - g_d, -jnp.inf))
        a_d = jnp.sum(q * k_d * decay, axis=-1, keepdims=True)
        o = o + a_d * v_d
    return o


def _hgrn2_kernel(q_ref, f_ref, i_ref, g_ref, lb_ref, gn_ref, s0_ref, o_ref, sout_ref,
                  st_ref, k_ref, qs_ref, G_ref, oi_ref, *, chunk, valid_len):
    c = pl.program_id(1)

    @pl.when(c == 0)
    def _():
        for h in range(H_A):
            st_ref[h] = s0_ref[h].T

    heads = [slice(h * DK_A, (h + 1) * DK_A) for h in range(H_A)]
    row = lax.broadcasted_iota(jnp.int32, (chunk, 1), 0)
    rmod = row % GLA_SUB
    tri = (lax.broadcasted_iota(jnp.int32, (chunk, chunk), 0)
           >= lax.broadcasted_iota(jnp.int32, (chunk, chunk), 1)).astype(F32)

    for sl in heads:
        zf = f_ref[:, sl]
        lb = lb_ref[:, sl]
        e = jnp.exp(-jnp.abs(zf))
        log_sig = jnp.minimum(zf, 0.0) - jnp.log1p(e)
        la = jnp.log(lb)
        lc = jnp.log1p(-lb) + log_sig
        log_f = jnp.maximum(la, lc) + jnp.log1p(jnp.exp(-jnp.abs(la - lc)))
        kk = (1.0 - lb) * (jnp.where(zf >= 0.0, e, 1.0) / (1.0 + e))
        if valid_len is not None:
            live = (c * chunk + row) < valid_len
            log_f = jnp.where(live, log_f, 0.0)
            kk = jnp.where(live, kk, 0.0)
        qr = q_ref[:, sl]
        k_ref[:, sl] = kk
        qs_ref[:, sl] = qr * jax.nn.sigmoid(qr) * (DK_A ** -0.5)
        G_ref[:, sl] = jnp.dot(tri, log_f, precision=lax.Precision.HIGHEST, preferred_element_type=F32)

    mid = chunk // 2
    g_mid = G_ref[mid - 1:mid, :]
    g_last = G_ref[chunk - 1:chunk, :]
    safe = jnp.max(jnp.maximum(-g_mid, g_mid - g_last)) < GLA_SAFE_DROP

    @pl.when(safe)
    def _():
        weights = [_hgrn2_pair_weights_factored(qs_ref[:, sl], k_ref[:, sl], G_ref[:, sl], g_mid[:, sl])
                   for sl in heads]
        for sl, a in zip(heads, weights):
            oi_ref[:, sl] = _dot(a, i_ref[:, sl].astype(BF16))

    @pl.when(jnp.logical_not(safe))
    def _():
        for sl in heads:
            oi_ref[:, sl] = _hgrn2_intra_exact(qs_ref[:, sl], k_ref[:, sl], i_ref[:, sl], G_ref[:, sl], rmod, chunk)

    carried = [lax.dot_general((qs_ref[:, sl] * jnp.exp(G_ref[:, sl])).astype(BF16), st_ref[h].astype(BF16),
                               _NT, preferred_element_type=F32) for h, sl in enumerate(heads)]
    added = [lax.dot_general(i_ref[:, sl].astype(BF16),
                             (k_ref[:, sl] * jnp.exp(g_last[:, sl] - G_ref[:, sl])).astype(BF16),
                             _TN, preferred_element_type=F32) for sl in heads]
    gn = gn_ref[...]
    for h, sl in enumerate(heads):
        st_ref[h] = st_ref[h] * jnp.exp(g_last[:, sl]) + added[h]
        gr = g_ref[:, sl]
        o = oi_ref[:, sl] + carried[h]
        o_ref[:, sl] = (_rms(o, gn) * (gr * jax.nn.sigmoid(gr))).astype(o_ref.dtype)

    @pl.when(c == pl.num_programs(1) - 1)
    def _():
        for h in range(H_A):
            sout_ref[h] = st_ref[h].T


def _hgrn2(z, lower, g_onorm, s0, layer, s0_layer, *, chunk, valid_len=None):
    b, l, _ = z.shape
    chunk = _tile(l, chunk)

    def col_spec(off):
        return pl.BlockSpec((None, chunk, D_A), lambda bi, c: (bi, c, off // D_A))

    return pl.pallas_call(
        functools.partial(_hgrn2_kernel, chunk=chunk, valid_len=valid_len),
        out_shape=(jax.ShapeDtypeStruct((b, l, D_A), BF16),
                   jax.ShapeDtypeStruct((b, H_A, DK_A, DV_A), F32)),
        grid=(b, l // chunk),
        in_specs=[col_spec(OFF_QA), col_spec(OFF_FA), col_spec(OFF_IA), col_spec(OFF_GA),
                  pl.BlockSpec((None, 1, D_A), lambda bi, c: (layer, 0, 0)),
                  pl.BlockSpec((None, 1, DV_A), lambda bi, c: (layer, 0, 0)),
                  pl.BlockSpec((None, None, H_A, DK_A, DV_A), lambda bi, c: (s0_layer, bi, 0, 0, 0))],
        out_specs=(pl.BlockSpec((None, chunk, D_A), lambda bi, c: (bi, c, 0)),
                   pl.BlockSpec((None, H_A, DK_A, DV_A), lambda bi, c: (bi, 0, 0, 0))),
        scratch_shapes=[pltpu.VMEM((H_A, DV_A, DK_A), F32)] + [pltpu.VMEM((chunk, D_A), F32)] * 4,
        compiler_params=_params("parallel", "arbitrary"),
        name="hgrn2",
    )(z, z, z, z, lower, g_onorm, s0)


def _head_norm_kernel(x_ref, g_ref, o_ref, *, heads, hd):
    g = g_ref[...]
    for h in range(heads):
        sl = slice(h * hd, (h + 1) * hd)
        o_ref[:, sl] = _rms(x_ref[:, sl], g)


def _head_norm(x, gains, layer, *, col_block, heads, hd):
    b, r, _ = x.shape
    width = heads * hd
    return pl.pallas_call(
        functools.partial(_head_norm_kernel, heads=heads, hd=hd),
        out_shape=jax.ShapeDtypeStruct((b, r, width), F32),
        grid=(b,),
        in_specs=[pl.BlockSpec((None, r, width), lambda bi: (bi, 0, col_block)),
                  pl.BlockSpec((None, 1, hd), lambda bi: (layer, 0, 0))],
        out_specs=pl.BlockSpec((None, r, width), lambda bi: (bi, 0, 0)),
        compiler_params=_params("parallel"),
        name="head_norm",
    )(x, gains)


def _moba_kv_kernel(k_ref, v_ref, g_ref, kn_ref, kb_ref, vt_ref, km_ref):
    g = g_ref[...]
    for h in range(H_B):
        sl = slice(h * DH_B, (h + 1) * DH_B)
        kn = _rms(k_ref[:, sl], g)
        kn_ref[:, sl] = kn
        kb_ref[:, sl] = kn.astype(BF16)
        km_ref[:, sl] = jnp.mean(kn, axis=0, keepdims=True)
        vt_ref[sl, :] = v_ref[:, sl].T.astype(BF16)


def _moba_kv(z, g_kb, layer):
    b, l, _ = z.shape
    nblk = l // MOBA_BLOCK
    act = lambda dtype: jax.ShapeDtypeStruct((b, l, D_B), dtype)
    act_spec = pl.BlockSpec((None, MOBA_BLOCK, D_B), lambda bi, n: (bi, n, 0))
    kn, kb, vt, km = pl.pallas_call(
        _moba_kv_kernel,
        out_shape=(act(F32), act(BF16), jax.ShapeDtypeStruct((b, D_B, l), BF16),
                   jax.ShapeDtypeStruct((b, nblk, 1, D_B), F32)),
        grid=(b, nblk),
        in_specs=[pl.BlockSpec((None, MOBA_BLOCK, D_B), lambda bi, n: (bi, n, OFF_KB // D_B)),
                  pl.BlockSpec((None, MOBA_BLOCK, D_B), lambda bi, n: (bi, n, OFF_VB // D_B)),
                  pl.BlockSpec((None, 1, DH_B), lambda bi, n: (layer, 0, 0))],
        out_specs=(act_spec, act_spec,
                   pl.BlockSpec((None, D_B, MOBA_BLOCK), lambda bi, n: (bi, 0, n)),
                   pl.BlockSpec((None, None, 1, D_B), lambda bi, n: (bi, n, 0, 0))),
        compiler_params=_params("parallel", "parallel"),
        name="moba_kv",
    )(z, z, g_kb)
    return kn, kb, vt, km.reshape(b, nblk, D_B)


def _topk_mask_t(gate_t, n_valid, topk):
    n, r = gate_t.shape
    rowi = lax.broadcasted_iota(jnp.int32, (n, r), 0)
    gm = jnp.where(rowi < n_valid, gate_t, -jnp.inf)
    sel = jnp.zeros((n, r), F32)
    for j in range(n):
        gj = gm[j:j + 1]
        beats = jnp.where(gm > gj, 1.0, jnp.where((gm == gj) & (rowi < j), 1.0, 0.0))
        rank = jnp.sum(beats, axis=0, keepdims=True)
        sel = jnp.where((rowi == j) & (rank < topk) & (rowi < n_valid), 1.0, sel)
    return sel


def _moba_prompt_kernel(slopes_ref, q_ref, gq_ref, k_ref, vt_ref, km_ref, o_ref, s_ref, qb_ref, sel_ref):
    hg = pl.program_id(1)
    i = pl.program_id(2)
    blk = MOBA_BLOCK
    scale = DH_B ** -0.5
    heads = [slice(hh * DH_B, (hh + 1) * DH_B) for hh in range(MOBA_HEADS_PER_STEP)]

    @pl.when(i == 0)
    def _():
        for hh, sl in enumerate(heads):
            qn = _rms(q_ref[:, sl], gq_ref[...])
            qb_ref[:, sl] = qn.astype(BF16)
            gate_t = lax.dot_general(km_ref[:, sl], qn, _NT, precision=lax.Precision.HIGHEST,
                                     preferred_element_type=F32)
            q_block = lax.broadcasted_iota(jnp.int32, gate_t.shape, 1) // blk
            sel_ref[hh] = _topk_mask_t(gate_t, q_block, MOBA_TOPK)

    d0 = (lax.broadcasted_iota(jnp.int32, (blk, blk), 1)
          - lax.broadcasted_iota(jnp.int32, (blk, blk), 0)).astype(F32)
    slopes = [slopes_ref[hg * MOBA_HEADS_PER_STEP + hh] for hh in range(MOBA_HEADS_PER_STEP)]
    bias0 = [-slope * d0 for slope in slopes]

    def attend(own):
        cols = slice(own * blk, (own + 1) * blk)
        m = [None] * len(heads)
        for j in range(own + 1):
            rows = slice(j * blk, (j + 1) * blk)
            for hh, sl in enumerate(heads):
                s = lax.dot_general(k_ref[rows, sl], qb_ref[cols, sl], _NT, preferred_element_type=F32) * scale
                s = s + (bias0[hh] - slopes[hh] * float((own - j) * blk))
                keep = (d0 >= 0.0) if j == own else (sel_ref[hh, j:j + 1, cols] > 0.0)
                s = jnp.where(keep, s, -jnp.inf)
                s_ref[hh, rows, :] = s
                m_j = jnp.max(s, axis=0, keepdims=True)
                m[hh] = m_j if m[hh] is None else jnp.maximum(m[hh], m_j)
        l = [jnp.zeros((1, blk), F32) for _ in heads]
        acc = [jnp.zeros((DH_B, blk), F32) for _ in heads]
        for j in range(own + 1):
            rows = slice(j * blk, (j + 1) * blk)
            for hh, sl in enumerate(heads):
                p = jnp.exp(s_ref[hh, rows, :] - m[hh])
                l[hh] = l[hh] + jnp.sum(p, axis=0, keepdims=True)
                acc[hh] = acc[hh] + _dot(vt_ref[sl, rows], p.astype(BF16))
        for hh, sl in enumerate(heads):
            o_ref[:, sl] = (acc[hh] / l[hh]).T.astype(o_ref.dtype)

    for own in range(km_ref.shape[0]):
        pl.when(i == own)(functools.partial(attend, own))


def _moba_prompt(z, g_qb, kb, vt, kmean, slopes, layer):
    b, l, _ = z.shape
    nblk = l // MOBA_BLOCK
    width = MOBA_HEADS_PER_STEP * DH_B
    return pl.pallas_call(
        _moba_prompt_kernel,
        out_shape=jax.ShapeDtypeStruct((b, l, D_B), BF16),
        grid=(b, H_B // MOBA_HEADS_PER_STEP, nblk),
        in_specs=[pl.BlockSpec(memory_space=pltpu.SMEM),
                  pl.BlockSpec((None, l, width), lambda bi, h, i: (bi, 0, OFF_QB // width + h)),
                  pl.BlockSpec((None, 1, DH_B), lambda bi, h, i: (layer, 0, 0)),
                  pl.BlockSpec((None, l, width), lambda bi, h, i: (bi, 0, h)),
                  pl.BlockSpec((None, width, l), lambda bi, h, i: (bi, h, 0)),
                  pl.BlockSpec((None, nblk, width), lambda bi, h, i: (bi, 0, h))],
        out_specs=pl.BlockSpec((None, MOBA_BLOCK, width), lambda bi, h, i: (bi, i, h)),
        scratch_shapes=[pltpu.VMEM((MOBA_HEADS_PER_STEP, l, MOBA_BLOCK), F32), pltpu.VMEM((l, width), BF16),
                        pltpu.VMEM((MOBA_HEADS_PER_STEP, nblk, l), F32)],
        compiler_params=_params("parallel", "parallel", "arbitrary"),
        name="moba_prompt",
    )(slopes, z, g_qb, kb, vt, kmean)


def _page_mean_kernel(pt_ref, *refs):
    del pt_ref
    page_refs, o_ref = refs[:-1], refs[-1]
    for n in range(PAGE_MEAN_BLOCKS):
        pages = page_refs[n * PAGES_PER_BLOCK:(n + 1) * PAGES_PER_BLOCK]
        total = jnp.sum(pages[0][...], axis=0)
        for r in pages[1:]:
            total = total + jnp.sum(r[...], axis=0)
        o_ref[n] = total * (1.0 / MOBA_BLOCK)


def _page_means(cache_k, page_table_flat, layer, n_seq, n_pages):
    n_blocks = n_pages // PAGES_PER_BLOCK
    assert n_blocks % PAGE_MEAN_BLOCKS == 0
    pages_per_step = PAGE_MEAN_BLOCKS * PAGES_PER_BLOCK

    def page_spec(p):
        return pl.BlockSpec(
            (None, None, PAGE_SIZE, H_B, DH_B),
            lambda bi, n, pt: (layer, pt[bi * n_pages + n * pages_per_step + p], 0, 0, 0))

    return pl.pallas_call(
        _page_mean_kernel,
        out_shape=jax.ShapeDtypeStruct((n_seq, n_blocks, H_B, DH_B), F32),
        grid_spec=pltpu.PrefetchScalarGridSpec(
            num_scalar_prefetch=1,
            grid=(n_seq, n_blocks // PAGE_MEAN_BLOCKS),
            in_specs=[page_spec(p) for p in range(pages_per_step)],
            out_specs=pl.BlockSpec((None, PAGE_MEAN_BLOCKS, H_B, DH_B), lambda bi, n, pt: (bi, n, 0, 0))),
        compiler_params=_params("parallel", "parallel"),
        name="moba_page_means",
    )(page_table_flat, *([cache_k] * pages_per_step))


def _moba_select_kernel(q_ref, gq_ref, km_ref, qn_ref, sel_ref):
    rows = q_ref.shape[0]
    n_blocks = km_ref.shape[0]
    g = gq_ref[...]
    col = lax.broadcasted_iota(jnp.int32, (rows, n_blocks), 1)
    lane = lax.broadcasted_iota(jnp.int32, (rows, sel_ref.shape[-1]), 1)
    for h in range(H_B):
        sl = slice(h * DH_B, (h + 1) * DH_B)
        qn = _rms(q_ref[:, sl], g)
        qn_ref[:, sl] = qn
        gate = lax.dot_general(qn, km_ref[:, sl], _NT, precision=lax.Precision.HIGHEST,
                               preferred_element_type=F32)
        picks = jnp.zeros(lane.shape, jnp.int32)
        for r in range(MOBA_TOPK):
            best = jnp.max(gate, axis=-1, keepdims=True)
            idx = jnp.min(jnp.where(gate == best, col, n_blocks), axis=-1, keepdims=True)
            picks = jnp.where(lane == r, idx, picks)
            gate = jnp.where(col == idx, -jnp.inf, gate)
        sel_ref[h] = picks


def _moba_select(z, g_qb, kmean, layer):
    b, rows, _ = z.shape
    n_blocks = kmean.shape[1]
    return pl.pallas_call(
        _moba_select_kernel,
        out_shape=(jax.ShapeDtypeStruct((b, rows, D_B), F32),
                   jax.ShapeDtypeStruct((b, H_B, rows, 128), jnp.int32)),
        grid=(b,),
        in_specs=[pl.BlockSpec((None, rows, D_B), lambda bi: (bi, 0, OFF_QB // D_B)),
                  pl.BlockSpec((None, 1, DH_B), lambda bi: (layer, 0, 0)),
                  pl.BlockSpec((None, n_blocks, D_B), lambda bi: (bi, 0, 0))],
        out_specs=(pl.BlockSpec((None, rows, D_B), lambda bi: (bi, 0, 0)),
                   pl.BlockSpec((None, H_B, rows, 128), lambda bi: (bi, 0, 0, 0))),
        compiler_params=_params("parallel"),
        name="moba_select",
    )(z, g_qb, kmean)


def _moba_sample_kernel(sel_ref, pt_ref, slopes_ref, q_ref, kown_ref, vown_ref, ck_ref, cv_ref, o_ref,
                        kbuf, vbuf, sem, *, layer, n_tok, n_pages):
    step = pl.program_id(0)
    n_steps = pl.num_programs(0)
    tiles_per_tok = MOBA_TOPK * PAGES_PER_BLOCK
    past_len = n_pages * PAGE_SIZE

    def picked_block(st, t, s):
        return sel_ref[(st * n_tok + t) * MOBA_TOPK + s]

    def tile_copies(st, slot):
        bi, h = st // H_B, st % H_B
        copies = []
        for t in range(n_tok):
            for s in range(MOBA_TOPK):
                block = picked_block(st, t, s)
                for p in range(PAGES_PER_BLOCK):
                    page = pt_ref[bi * n_pages + block * PAGES_PER_BLOCK + p]
                    j = t * tiles_per_tok + s * PAGES_PER_BLOCK + p
                    copies.append(pltpu.make_async_copy(
                        ck_ref.at[layer, page, :, h, :], kbuf.at[slot, j], sem.at[slot]))
                    copies.append(pltpu.make_async_copy(
                        cv_ref.at[layer, page, :, h, :], vbuf.at[slot, j], sem.at[slot]))
        return copies

    slot = step % 2

    @pl.when(step == 0)
    def _():
        for c in tile_copies(step, slot):
            c.start()

    @pl.when(step + 1 < n_steps)
    def _():
        for c in tile_copies(step + 1, 1 - slot):
            c.start()

    for c in tile_copies(step, slot):
        c.wait()

    slope = slopes_ref[step % H_B]
    scale = DH_B ** -0.5
    rows = kown_ref.shape[0]
    k_own, v_own = kown_ref[...], vown_ref[...]
    own_pos = past_len + lax.broadcasted_iota(jnp.int32, (rows, 1), 0)
    page_row = lax.broadcasted_iota(jnp.int32, (PAGE_SIZE, 1), 0)
    out_row = lax.broadcasted_iota(jnp.int32, (rows, 1), 0)
    out = jnp.zeros((rows, DH_B), F32)
    for t in range(n_tok):
        q = q_ref[t:t + 1, :]
        t_pos = past_len + t

        def score(keys, k_pos):
            return (jnp.sum(keys * q, axis=-1, keepdims=True) * scale
                    - slope * (t_pos - k_pos).astype(F32))

        scores = [jnp.where(own_pos <= t_pos, score(k_own, own_pos), -jnp.inf)]
        values = [v_own]
        for s in range(MOBA_TOPK):
            block = picked_block(step, t, s)
            for p in range(PAGES_PER_BLOCK):
                j = t * tiles_per_tok + s * PAGES_PER_BLOCK + p
                scores.append(score(kbuf[slot, j], block * MOBA_BLOCK + p * PAGE_SIZE + page_row))
                values.append(vbuf[slot, j])
        m = functools.reduce(jnp.maximum, [jnp.max(s, axis=0, keepdims=True) for s in scores])
        probs = [jnp.exp(s - m) for s in scores]
        l = functools.reduce(jnp.add, [jnp.sum(p, axis=0, keepdims=True) for p in probs])
        acc = functools.reduce(jnp.add, [jnp.sum(p * v, axis=0, keepdims=True) for p, v in zip(probs, values)])
        out = jnp.where(out_row == t, acc / l, out)
    o_ref[...] = out.astype(o_ref.dtype)


def _moba_sample(qn, kn, z, cache_k, cache_v, picks_flat, page_table_flat, slopes, layer, *, n_tok, n_pages):
    b, rows, _ = qn.shape
    n_tiles = n_tok * MOBA_TOPK * PAGES_PER_BLOCK
    own_spec = pl.BlockSpec((None, rows, DH_B), lambda st, sel, pt: (st // H_B, 0, st % H_B))
    return pl.pallas_call(
        functools.partial(_moba_sample_kernel, layer=layer, n_tok=n_tok, n_pages=n_pages),
        out_shape=jax.ShapeDtypeStruct((b, rows, D_B), BF16),
        grid_spec=pltpu.PrefetchScalarGridSpec(
            num_scalar_prefetch=2,
            grid=(b * H_B,),
            in_specs=[pl.BlockSpec(memory_space=pltpu.SMEM), own_spec, own_spec,
                      pl.BlockSpec((None, rows, DH_B),
                                   lambda st, sel, pt: (st // H_B, 0, OFF_VB // DH_B + st % H_B)),
                      pl.BlockSpec(memory_space=pl.ANY), pl.BlockSpec(memory_space=pl.ANY)],
            out_specs=pl.BlockSpec((None, rows, DH_B), lambda st, sel, pt: (st // H_B, 0, st % H_B)),
            scratch_shapes=[pltpu.VMEM((2, n_tiles, PAGE_SIZE, DH_B), F32),
                            pltpu.VMEM((2, n_tiles, PAGE_SIZE, DH_B), F32),
                            pltpu.SemaphoreType.DMA((2,))]),
        compiler_params=_params("arbitrary"),
        name="moba_sample",
    )(picks_flat, page_table_flat, slopes, qn, kn, z, cache_k, cache_v)


def _mem_attn_kernel(q_ref, gq_ref, k_ref, v_ref, o_ref):
    qn = _rms(q_ref[...], gq_ref[...])
    s = lax.dot_general(qn.astype(BF16), k_ref[...].astype(BF16), _NT, preferred_element_type=F32)
    s = s * (DH_C ** -0.5)
    e = jnp.exp(s - jnp.max(s, axis=-1, keepdims=True))
    p = e / jnp.sum(e, axis=-1, keepdims=True)
    o_ref[...] = _dot(p.astype(BF16), v_ref[...].astype(BF16)).astype(o_ref.dtype)


def _mem_attn(z, g_qc, mem_k, mem_v, layer, *, kv_index, v_col0, tq):
    b, l, _ = z.shape
    tq = _tile(l, tq)
    k_block = (None,) * (mem_k.ndim - 2) + (N_MEM, DH_C)
    v_block = (None,) * (mem_v.ndim - 2) + (N_MEM, DH_C)
    return pl.pallas_call(
        _mem_attn_kernel,
        out_shape=jax.ShapeDtypeStruct((b, l, D_C), BF16),
        grid=(b, H_C, l // tq),
        in_specs=[pl.BlockSpec((None, tq, DH_C), lambda bi, h, i: (bi, i, OFF_QC // DH_C + h)),
                  pl.BlockSpec((None, 1, DH_C), lambda bi, h, i: (layer, 0, 0)),
                  pl.BlockSpec(k_block, lambda bi, h, i: kv_index(bi, h, 0)),
                  pl.BlockSpec(v_block, lambda bi, h, i: kv_index(bi, h, v_col0))],
        out_specs=pl.BlockSpec((None, tq, DH_C), lambda bi, h, i: (bi, i, h)),
        compiler_params=_params("parallel", "parallel", "arbitrary"),
        name="mem_attn",
    )(z, g_qc, mem_k, mem_v)


def _mixer_and_ffn(xs, layer, w, branches, next_page_means=None):
    shapes = [x.shape[:2] for x in xs]
    x2 = [x.reshape(-1, D_MODEL) for x in xs]
    h = [_rmsnorm(x, w["norm_mix"], layer) for x in x2]
    z2 = _mm_groups([[a] for a in h], [w["w_in"]], layer, epilogue=_ep_plain, out_dtype=F32,
                    tm=1024, tn=1024, name="mm_in")
    outs = [fn(z.reshape(*s, D_IN)) for fn, z, s in zip(branches, z2, shapes)]
    gates = [(z2, OFF_GATE + k * D_MODEL) for k in range(N_BRANCH)]
    merged = _mm_groups([list(o[:N_BRANCH]) for o in outs], [w["w_br_a"], w["w_br_b"], w["w_br_c"]], layer,
                        epilogue=_ep_gated_sum, extras=gates, out_dtype=BF16, tm=1024, tn=512, name="mm_merge")
    x2 = _mm_groups([[a] for a in merged], [w["w_out"]], layer, epilogue=_ep_residual, extras=[(x2, 0)],
                    out_dtype=F32, tm=1024, tn=1024, name="mm_out")
    h = [_rmsnorm(x, w["norm_ffn"], layer) for x in x2]
    act = _mm_groups([[a] for a in h], [w["w_gate"], w["w_up"]], layer, act_of_w=(0, 0), epilogue=_ep_swiglu,
                     out_dtype=BF16, tm=1024, tn=512, name="mm_swiglu", page_means=next_page_means)
    act, kmean_next = (act[:-1], act[-1]) if next_page_means is not None else (act, None)
    x2 = _mm_groups([[a] for a in act], [w["w_down"]], layer, epilogue=_ep_residual, extras=[(x2, 0)],
                    out_dtype=F32, tm=512, tn=512, name="mm_down")
    return ([x.reshape(*s, D_MODEL) for x, s in zip(x2, shapes)],
            [z.reshape(*s, D_IN) for z, s in zip(z2, shapes)], [o[N_BRANCH] for o in outs], kmean_next)


def kernel(x_prompt, x_sample, state_hgrn, cache_k, cache_v, cache_mem_k, cache_mem_v, page_table,
           mem_prompt, lb_logits, norm_mix, w_in, norm_o_a, norm_q_b, norm_k_b, norm_q_c, norm_k_c,
           norm_mem, w_mem_kv, w_br_a, w_br_b, w_br_c, w_out, norm_ffn, w_gate, w_up, w_down):
    depth = w_in.shape[0]
    n_prompt, seq, _ = x_prompt.shape
    n_seq, n_tok, _ = x_sample.shape
    n_pages = page_table.shape[1]
    assert seq % MOBA_BLOCK == 0 and n_pages % PAGES_PER_BLOCK == 0
    assert n_pages // PAGES_PER_BLOCK >= MOBA_TOPK and n_tok <= SAMPLE_ROWS <= MOBA_BLOCK

    row = lambda g: g.reshape(depth, 1, g.shape[-1])
    w = {"norm_mix": row(norm_mix), "norm_ffn": row(norm_ffn), "w_in": w_in, "w_br_a": w_br_a, "w_br_b": w_br_b,
         "w_br_c": w_br_c, "w_out": w_out, "w_gate": w_gate, "w_up": w_up, "w_down": w_down}
    g_oa, g_qb, g_kb, g_qc, g_kc, g_mem = (row(g) for g in (norm_o_a, norm_q_b, norm_k_b, norm_q_c,
                                                           norm_k_c, norm_mem))
    lower = _lower_bounds(lb_logits).reshape(depth, 1, D_A)
    slopes = 2.0 ** (-8.0 * jnp.arange(1, H_B + 1, dtype=F32) / H_B)

    cmk = cache_mem_k.reshape(depth, n_seq, N_MEM, D_C)
    cmv = cache_mem_v.reshape(depth, n_seq, N_MEM, D_C)
    pt_flat = page_table.reshape(-1)
    s0_prompt = jnp.zeros((1, n_prompt, H_A, DK_A, DV_A), F32)
    mem2 = mem_prompt.reshape(n_prompt * N_MEM, D_MODEL)

    xp = x_prompt
    xs = jnp.pad(x_sample, ((0, 0), (0, SAMPLE_ROWS - n_tok), (0, 0)))
    outs = {k: [] for k in ("hp", "kp", "vp", "mkp", "mvp", "hs", "ks", "vs")}

    kmean_layer = _page_means(cache_k, pt_flat, 0, n_seq, n_pages)
    for layer in range(depth):
        (kv,) = _mm_groups([[_rmsnorm(mem2, g_mem, layer)]], [w_mem_kv], layer, epilogue=_ep_plain,
                           out_dtype=F32, tm=1024, tn=1024, name="mm_mem")
        kv = kv.reshape(n_prompt, N_MEM, 2 * D_C)
        mem_k = _head_norm(kv, g_kc, layer, col_block=0, heads=H_C, hd=DH_C)
        mem_v = kv[:, :, D_C:]

        def prompt_branches(z):
            o_a, s_new = _hgrn2(z, lower, g_oa, s0_prompt, layer, 0, chunk=64)
            kn, kb, vt, kmean = _moba_kv(z, g_kb, layer)
            o_b = _moba_prompt(z, g_qb, kb, vt, kmean, slopes, layer)
            o_c = _mem_attn(z, g_qc, mem_k, kv, layer, tq=2048, v_col0=D_C // DH_C,
                            kv_index=lambda bi, h, c0: (bi, 0, c0 + h))
            m = n_prompt * seq
            return o_a.reshape(m, D_A), o_b.reshape(m, D_B), o_c.reshape(m, D_C), (s_new, kn)

        def sample_branches(z):
            o_a, s_new = _hgrn2(z, lower, g_oa, state_hgrn, layer, layer, chunk=SAMPLE_ROWS, valid_len=n_tok)
            kn = _head_norm(z, g_kb, layer, col_block=OFF_KB // D_B, heads=H_B, hd=DH_B)
            kmean = kmean_layer.reshape(n_seq, -1, D_B)
            qn, picks = _moba_select(z, g_qb, kmean, layer)
            picks_flat = picks[:, :, :n_tok, :MOBA_TOPK].reshape(-1)
            o_b = _moba_sample(qn, kn, z, cache_k, cache_v, picks_flat, pt_flat, slopes, layer,
                               n_tok=n_tok, n_pages=n_pages)
            o_c = _mem_attn(z, g_qc, cmk, cmv, layer, tq=SAMPLE_ROWS, v_col0=0,
                            kv_index=lambda bi, h, c0: (layer, bi, 0, c0 + h))
            m = n_seq * SAMPLE_ROWS
            return o_a.reshape(m, D_A), o_b.reshape(m, D_B), o_c.reshape(m, D_C), (s_new, kn)

        next_means = (cache_k, pt_flat, layer + 1, n_seq, n_pages) if layer + 1 < depth else None
        (xp, xs), (zp, zs), ((s_p, k_p), (s_s, k_s)), kmean_layer = _mixer_and_ffn(
            [xp, xs], layer, w, [prompt_branches, sample_branches], next_means)

        outs["hp"].append(s_p)
        outs["kp"].append(k_p.reshape(n_prompt, seq, H_B, DH_B))
        outs["vp"].append(zp[:, :, OFF_VB:OFF_VB + D_B].reshape(n_prompt, seq, H_B, DH_B))
        outs["mkp"].append(mem_k.reshape(n_prompt, N_MEM, H_C, DH_C))
        outs["mvp"].append(mem_v.reshape(n_prompt, N_MEM, H_C, DH_C))
        outs["hs"].append(s_s)
        outs["ks"].append(k_s[:, :n_tok].reshape(n_seq, n_tok, H_B, DH_B))
        outs["vs"].append(zs[:, :n_tok, OFF_VB:OFF_VB + D_B].reshape(n_seq, n_tok, H_B, DH_B))

    stack = lambda k: jnp.stack(outs[k])
    return (xp, xs[:, :n_tok], stack("hp"), stack("kp"), stack("vp"), stack("mkp"), stack("mvp"),
            stack("hs"), stack("ks"), stack("vs"))
```

```python
import functools

import jax
import jax.numpy as jnp
from jax import lax
from jax.experimental import pallas as pl
from jax.experimental.pallas import tpu as pltpu

D_MODEL = 2048
H_A, DK_A, DV_A = 8, 128, 128
D_A = H_A * DK_A
H_B, DH_B = 8, 128
D_B = H_B * DH_B
MOBA_BLOCK = 256
MOBA_TOPK = 3
PAGE_SIZE = 128
PAGES_PER_BLOCK = MOBA_BLOCK // PAGE_SIZE
N_MEM = 256
H_C, DH_C = 4, 256
D_C = H_C * DH_C
N_BRANCH = 3
D_IN = 4 * D_A + 3 * D_B + D_C + N_BRANCH * D_MODEL
EPS = 1e-6

OFF_QA, OFF_FA, OFF_IA, OFF_GA = 0, D_A, 2 * D_A, 3 * D_A
OFF_QB = 4 * D_A
OFF_KB = OFF_QB + D_B
OFF_VB = OFF_KB + D_B
OFF_QC = OFF_VB + D_B
OFF_GATE = OFF_QC + D_C

GLA_SUB = 16
GLA_SAFE_DROP = 80.0
SAMPLE_ROWS = 16
MOBA_HEADS_PER_STEP = 2
PAGE_MEAN_BLOCKS = 4
V7X_VMEM_LIMIT = 56 * 1024 * 1024

F32 = jnp.float32
BF16 = jnp.bfloat16
_NT = (((1,), (1,)), ((), ()))
_TN = (((0,), (0,)), ((), ()))


def _params(*semantics):
    return pltpu.CompilerParams(dimension_semantics=semantics, vmem_limit_bytes=V7X_VMEM_LIMIT)


def _tile(n, pref):
    t = min(n, pref)
    assert n % t == 0, (n, pref)
    return t


def _rms(x, gain):
    return x * lax.rsqrt(jnp.mean(x * x, axis=-1, keepdims=True) + EPS) * gain


def _rmsnorm_kernel(x_ref, g_ref, o_ref):
    o_ref[...] = _rms(x_ref[...], g_ref[...]).astype(o_ref.dtype)


def _rmsnorm(x, gains, layer):
    m, d = x.shape
    tm = _tile(m, 512)
    return pl.pallas_call(
        _rmsnorm_kernel,
        out_shape=jax.ShapeDtypeStruct((m, d), BF16),
        grid=(m // tm,),
        in_specs=[pl.BlockSpec((tm, d), lambda i: (i, 0)),
                  pl.BlockSpec((None, 1, d), lambda i: (layer, 0, 0))],
        out_specs=pl.BlockSpec((tm, d), lambda i: (i, 0)),
        compiler_params=_params("parallel"),
        name="rmsnorm",
    )(x, gains)


def _dot(a, b):
    return jnp.dot(a, b, preferred_element_type=F32)


def _mm_groups_kernel(*refs, n_groups, n_w, act_of_w, n_extra, epilogue, side=None):
    n_act = len(set(act_of_w))
    pos = 0
    acts = [refs[pos + g * n_act:pos + (g + 1) * n_act] for g in range(n_groups)]
    pos += n_groups * n_act
    weights = refs[pos:pos + n_w]
    pos += n_w
    extras = [refs[pos + g * n_extra:pos + (g + 1) * n_extra] for g in range(n_groups)]
    pos += n_groups * n_extra
    outs = refs[pos:pos + n_groups]
    wb = refs[pos + n_groups:]

    def compute(g):
        dots = [_dot(acts[g][act_of_w[k]][...], wb[k][...]) for k in range(n_w)]
        outs[g][...] = epilogue(dots, [e[...] for e in extras[g]]).astype(outs[g].dtype)

    if side is not None:
        side[0]()

    @pl.when(pl.program_id(1) == 0)
    def _():
        for k in range(n_w):
            wb[k][...] = weights[k][...].astype(BF16)
        for g in range(1, n_groups):
            compute(g)

    compute(0)
    if side is not None:
        side[1]()


def _page_mean_rider(pt_ref, cache_ref, km_ref, buf, sem, *, layer, n_seq, n_pages, per_step):
    blocks_per_seq = n_pages // PAGES_PER_BLOCK
    last_block = n_seq * blocks_per_seq - 1
    step = pl.program_id(0) * pl.num_programs(1) + pl.program_id(1)
    n_steps = pl.num_programs(0) * pl.num_programs(1)
    slot = step % 2

    def block_of(st, r):
        g = jnp.minimum(st * per_step + r, last_block)
        return g // blocks_per_seq, g % blocks_per_seq

    def page_copies(st, into):
        copies = []
        for r in range(per_step):
            seq, blk = block_of(st, r)
            for p in range(PAGES_PER_BLOCK):
                page = pt_ref[seq * n_pages + blk * PAGES_PER_BLOCK + p]
                copies.append(pltpu.make_async_copy(cache_ref.at[layer, page],
                                                    buf.at[into, r * PAGES_PER_BLOCK + p], sem.at[into]))
        return copies

    def before():
        @pl.when(step == 0)
        def _():
            for c in page_copies(step, slot):
                c.start()

        @pl.when(step + 1 < n_steps)
        def _():
            for c in page_copies(step + 1, 1 - slot):
                c.start()

        for c in page_copies(step, slot):
            c.wait()

    def beside():
        for r in range(per_step):
            seq, blk = block_of(step, r)
            total = jnp.sum(buf[slot, r * PAGES_PER_BLOCK], axis=0)
            for p in range(1, PAGES_PER_BLOCK):
                total = total + jnp.sum(buf[slot, r * PAGES_PER_BLOCK + p], axis=0)
            km_ref[seq, blk] = total * (1.0 / MOBA_BLOCK)

    return before, beside


def _mm_groups_rider_kernel(pt_ref, *refs, n_in, n_out, n_w, mm_kwargs, rider_kwargs):
    ins, cache_ref = refs[:n_in], refs[n_in]
    outs, km_ref = refs[n_in + 1:n_in + 1 + n_out], refs[n_in + 1 + n_out]
    wb = refs[n_in + 2 + n_out:n_in + 2 + n_out + n_w]
    buf, sem = refs[n_in + 2 + n_out + n_w:]
    side = _page_mean_rider(pt_ref, cache_ref, km_ref, buf, sem, **rider_kwargs)
    _mm_groups_kernel(*ins, *outs, *wb, side=side, **mm_kwargs)


def _mm_groups(acts, weights, layer, *, epilogue, out_dtype, tm, tn, act_of_w=None, extras=(), name,
               page_means=None):
    n_groups, n_w = len(acts), len(weights)
    act_of_w = tuple(range(n_w)) if act_of_w is None else tuple(act_of_w)
    k, n = weights[0].shape[1:]
    m = [a[0].shape[0] for a in acts]
    tm, tn = _tile(m[0], tm), _tile(n, tn)
    rows = [tm] + m[1:]
    grid = (n // tn, m[0] // tm)

    def row_block(g):
        return (lambda j, i, *_: (i, 0)) if g == 0 else (lambda j, i, *_: (0, 0))

    def out_block(g, first=0):
        return (lambda j, i, *_: (i, first + j)) if g == 0 else (lambda j, i, *_: (0, first + j))

    in_specs, args = [], []
    for g in range(n_groups):
        in_specs += [pl.BlockSpec((rows[g], a.shape[1]), row_block(g)) for a in acts[g]]
        args += list(acts[g])
    in_specs += [pl.BlockSpec((None, k, tn), lambda j, i, *_: (layer, 0, j))] * n_w
    args += list(weights)
    for g in range(n_groups):
        for arrays, col0 in extras:
            in_specs.append(pl.BlockSpec((rows[g], tn), out_block(g, col0 // tn)))
            args.append(arrays[g])
    out_shape = [jax.ShapeDtypeStruct((m[g], n), out_dtype) for g in range(n_groups)]
    out_specs = [pl.BlockSpec((rows[g], tn), out_block(g)) for g in range(n_groups)]
    scratch = [pltpu.VMEM((k, tn), BF16)] * n_w
    mm_kwargs = dict(n_groups=n_groups, n_w=n_w, act_of_w=act_of_w, n_extra=len(extras), epilogue=epilogue)
    if page_means is None:
        return pl.pallas_call(
            functools.partial(_mm_groups_kernel, **mm_kwargs),
            out_shape=tuple(out_shape), grid=grid, in_specs=in_specs, out_specs=tuple(out_specs),
            scratch_shapes=scratch, compiler_params=_params("arbitrary", "arbitrary"), name=name,
        )(*args)

    cache_k, pt_flat, cache_layer, n_seq, n_pages = page_means
    n_blocks = n_pages // PAGES_PER_BLOCK
    per_step = -(-n_seq * n_blocks // (grid[0] * grid[1]))
    km_shape = (n_seq, n_blocks, H_B, DH_B)
    return pl.pallas_call(
        functools.partial(_mm_groups_rider_kernel, n_in=len(args), n_out=n_groups, n_w=n_w, mm_kwargs=mm_kwargs,
                          rider_kwargs=dict(layer=cache_layer, n_seq=n_seq, n_pages=n_pages, per_step=per_step)),
        out_shape=tuple(out_shape) + (jax.ShapeDtypeStruct(km_shape, F32),),
        grid_spec=pltpu.PrefetchScalarGridSpec(
            num_scalar_prefetch=1, grid=grid,
            in_specs=in_specs + [pl.BlockSpec(memory_space=pl.ANY)],
            out_specs=tuple(out_specs) + (pl.BlockSpec(km_shape, lambda j, i, *_: (0, 0, 0, 0)),),
            scratch_shapes=scratch + [pltpu.VMEM((2, per_step * PAGES_PER_BLOCK, PAGE_SIZE, H_B, DH_B), F32),
                                      pltpu.SemaphoreType.DMA((2,))]),
        compiler_params=_params("arbitrary", "arbitrary"), name=name,
    )(pt_flat, *args, cache_k)


def _ep_plain(dots, extras):
    return dots[0]


def _ep_residual(dots, extras):
    return extras[0] + dots[0]


def _ep_swiglu(dots, extras):
    return dots[0] * jax.nn.sigmoid(dots[0]) * dots[1]


def _ep_gated_sum(dots, extras):
    return functools.reduce(jnp.add, [jax.nn.sigmoid(g) * d for g, d in zip(extras, dots)])


def _lower_bounds_kernel(x_ref, o_ref):
    x = x_ref[...]
    e = jnp.exp(x - jnp.max(x, axis=0, keepdims=True))
    p = e / jnp.sum(e, axis=0, keepdims=True)
    rows = [jnp.zeros_like(p[0:1])]
    for r in range(1, x.shape[0]):
        rows.append(rows[-1] + p[r:r + 1])
    o_ref[...] = jnp.concatenate(rows, axis=0)


def _lower_bounds(lb_logits):
    return pl.pallas_call(
        _lower_bounds_kernel,
        out_shape=jax.ShapeDtypeStruct(lb_logits.shape, F32),
        name="hgrn2_lower_bounds",
    )(lb_logits)


def _hgrn2_pair_weights_factored(q, kk, G, g_mid):
    qt = (q * jnp.exp(G - g_mid)).astype(BF16)
    kt = (kk * jnp.exp(g_mid - G)).astype(BF16)
    a = lax.dot_general(qt, kt, _NT, preferred_element_type=F32)
    causal = lax.broadcasted_iota(jnp.int32, a.shape, 1) <= lax.broadcasted_iota(jnp.int32, a.shape, 0)
    return jnp.where(causal, a, 0.0).astype(BF16)


def _hgrn2_intra_exact(q, kk, v, G, rmod, chunk):
    pieces = [jnp.zeros((GLA_SUB, DV_A), F32)]
    for i in range(1, chunk // GLA_SUB):
        lo = i * GLA_SUB
        r = G[lo - 1:lo]
        qi = (q[lo:lo + GLA_SUB] * jnp.exp(G[lo:lo + GLA_SUB] - r)).astype(BF16)
        kj = (kk[:lo] * jnp.exp(r - G[:lo])).astype(BF16)
        a = lax.dot_general(qi, kj, _NT, preferred_element_type=F32)
        pieces.append(_dot(a.astype(BF16), v[:lo].astype(BF16)))
    o = jnp.concatenate(pieces, axis=0)
    for d in range(GLA_SUB):
        k_d = kk if d == 0 else pltpu.roll(kk, d, axis=0)
        g_d = G if d == 0 else pltpu.roll(G, d, axis=0)
        v_d = v if d == 0 else pltpu.roll(v, d, axis=0)
        decay = jnp.exp(jnp.where(rmod >= d, G - g_d, -jnp.inf))
        a_d = jnp.sum(q * k_d * decay, axis=-1, keepdims=True)
        o = o + a_d * v_d
    return o


def _hgrn2_kernel(q_ref, f_ref, i_ref, g_ref, lb_ref, gn_ref, s0_ref, o_ref, sout_ref,
                  st_ref, k_ref, qs_ref, G_ref, oi_ref, *, chunk, valid_len):
    c = pl.program_id(1)

    @pl.when(c == 0)
    def _():
        for h in range(H_A):
            st_ref[h] = s0_ref[h].T

    heads = [slice(h * DK_A, (h + 1) * DK_A) for h in range(H_A)]
    row = lax.broadcasted_iota(jnp.int32, (chunk, 1), 0)
    rmod = row % GLA_SUB
    tri = (lax.broadcasted_iota(jnp.int32, (chunk, chunk), 0)
           >= lax.broadcasted_iota(jnp.int32, (chunk, chunk), 1)).astype(F32)

    for sl in heads:
        zf = f_ref[:, sl]
        lb = lb_ref[:, sl]
        e = jnp.exp(-jnp.abs(zf))
        log_sig = jnp.minimum(zf, 0.0) - jnp.log1p(e)
        la = jnp.log(lb)
        lc = jnp.log1p(-lb) + log_sig
        log_f = jnp.maximum(la, lc) + jnp.log1p(jnp.exp(-jnp.abs(la - lc)))
        kk = (1.0 - lb) * (jnp.where(zf >= 0.0, e, 1.0) / (1.0 + e))
        if valid_len is not None:
            live = (c * chunk + row) < valid_len
            log_f = jnp.where(live, log_f, 0.0)
            kk = jnp.where(live, kk, 0.0)
        qr = q_ref[:, sl]
        k_ref[:, sl] = kk
        qs_ref[:, sl] = qr * jax.nn.sigmoid(qr) * (DK_A ** -0.5)
        G_ref[:, sl] = jnp.dot(tri, log_f, precision=lax.Precision.HIGHEST, preferred_element_type=F32)

    mid = chunk // 2
    g_mid = G_ref[mid - 1:mid, :]
    g_last = G_ref[chunk - 1:chunk, :]
    safe = jnp.max(jnp.maximum(-g_mid, g_mid - g_last)) < GLA_SAFE_DROP

    @pl.when(safe)
    def _():
        weights = [_hgrn2_pair_weights_factored(qs_ref[:, sl], k_ref[:, sl], G_ref[:, sl], g_mid[:, sl])
                   for sl in heads]
        for sl, a in zip(heads, weights):
            oi_ref[:, sl] = _dot(a, i_ref[:, sl].astype(BF16))

    @pl.when(jnp.logical_not(safe))
    def _():
        for sl in heads:
            oi_ref[:, sl] = _hgrn2_intra_exact(qs_ref[:, sl], k_ref[:, sl], i_ref[:, sl], G_ref[:, sl], rmod, chunk)

    carried = [lax.dot_general((qs_ref[:, sl] * jnp.exp(G_ref[:, sl])).astype(BF16), st_ref[h].astype(BF16),
                               _NT, preferred_element_type=F32) for h, sl in enumerate(heads)]
    added = [lax.dot_general(i_ref[:, sl].astype(BF16),
                             (k_ref[:, sl] * jnp.exp(g_last[:, sl] - G_ref[:, sl])).astype(BF16),
                             _TN, preferred_element_type=F32) for sl in heads]
    gn = gn_ref[...]
    for h, sl in enumerate(heads):
        st_ref[h] = st_ref[h] * jnp.exp(g_last[:, sl]) + added[h]
        gr = g_ref[:, sl]
        o = oi_ref[:, sl] + carried[h]
        o_ref[:, sl] = (_rms(o, gn) * (gr * jax.nn.sigmoid(gr))).astype(o_ref.dtype)

    @pl.when(c == pl.num_programs(1) - 1)
    def _():
        for h in range(H_A):
            sout_ref[h] = st_ref[h].T


def _hgrn2(z, lower, g_onorm, s0, layer, s0_layer, *, chunk, valid_len=None):
    b, l, _ = z.shape
    chunk = _tile(l, chunk)

    def col_spec(off):
        return pl.BlockSpec((None, chunk, D_A), lambda bi, c: (bi, c, off // D_A))

    return pl.pallas_call(
        functools.partial(_hgrn2_kernel, chunk=chunk, valid_len=valid_len),
        out_shape=(jax.ShapeDtypeStruct((b, l, D_A), BF16),
                   jax.ShapeDtypeStruct((b, H_A, DK_A, DV_A), F32)),
        grid=(b, l // chunk),
        in_specs=[col_spec(OFF_QA), col_spec(OFF_FA), col_spec(OFF_IA), col_spec(OFF_GA),
                  pl.BlockSpec((None, 1, D_A), lambda bi, c: (layer, 0, 0)),
                  pl.BlockSpec((None, 1, DV_A), lambda bi, c: (layer, 0, 0)),
                  pl.BlockSpec((None, None, H_A, DK_A, DV_A), lambda bi, c: (s0_layer, bi, 0, 0, 0))],
        out_specs=(pl.BlockSpec((None, chunk, D_A), lambda bi, c: (bi, c, 0)),
                   pl.BlockSpec((None, H_A, DK_A, DV_A), lambda bi, c: (bi, 0, 0, 0))),
        scratch_shapes=[pltpu.VMEM((H_A, DV_A, DK_A), F32)] + [pltpu.VMEM((chunk, D_A), F32)] * 4,
        compiler_params=_params("parallel", "arbitrary"),
        name="hgrn2",
    )(z, z, z, z, lower, g_onorm, s0)


def _head_norm_kernel(x_ref, g_ref, o_ref, *, heads, hd):
    g = g_ref[...]
    for h in range(heads):
        sl = slice(h * hd, (h + 1) * hd)
        o_ref[:, sl] = _rms(x_ref[:, sl], g)


def _head_norm(x, gains, layer, *, col_block, heads, hd):
    b, r, _ = x.shape
    width = heads * hd
    return pl.pallas_call(
        functools.partial(_head_norm_kernel, heads=heads, hd=hd),
        out_shape=jax.ShapeDtypeStruct((b, r, width), F32),
        grid=(b,),
        in_specs=[pl.BlockSpec((None, r, width), lambda bi: (bi, 0, col_block)),
                  pl.BlockSpec((None, 1, hd), lambda bi: (layer, 0, 0))],
        out_specs=pl.BlockSpec((None, r, width), lambda bi: (bi, 0, 0)),
        compiler_params=_params("parallel"),
        name="head_norm",
    )(x, gains)


def _moba_kv_kernel(k_ref, v_ref, g_ref, kb_ref, vt_ref, km_ref, knew_ref, vnew_ref, ks_ref, vs_ref, sem):
    n_blk = pl.num_programs(1)
    step = pl.program_id(0) * n_blk + pl.program_id(1)
    n_steps = pl.num_programs(0) * n_blk
    slot = step % 2

    def out_copies(st, half):
        bi, n = st // n_blk, st % n_blk
        rows = pl.ds(pl.multiple_of(n * MOBA_BLOCK, MOBA_BLOCK), MOBA_BLOCK)
        copies = []
        for h in range(H_B):
            cols = slice(h * DH_B, (h + 1) * DH_B)
            copies.append(pltpu.make_async_copy(ks_ref.at[half, :, cols], knew_ref.at[bi, rows, h, :], sem.at[half]))
            copies.append(pltpu.make_async_copy(vs_ref.at[half, :, cols], vnew_ref.at[bi, rows, h, :], sem.at[half]))
        return copies

    @pl.when(step >= 2)
    def _():
        for c in out_copies(step - 2, slot):
            c.wait()

    g = g_ref[...]
    for h in range(H_B):
        sl = slice(h * DH_B, (h + 1) * DH_B)
        kn = _rms(k_ref[:, sl], g)
        ks_ref[slot, :, sl] = kn
        kb_ref[:, sl] = kn.astype(BF16)
        km_ref[:, sl] = jnp.mean(kn, axis=0, keepdims=True)
        vt_ref[sl, :] = v_ref[:, sl].T.astype(BF16)
    vs_ref[slot] = v_ref[...]
    for c in out_copies(step, slot):
        c.start()

    @pl.when((step == n_steps - 1) & (step >= 1))
    def _():
        for c in out_copies(step - 1, 1 - slot):
            c.wait()

    @pl.when(step == n_steps - 1)
    def _():
        for c in out_copies(step, slot):
            c.wait()


def _moba_kv(z, g_kb, layer):
    b, l, _ = z.shape
    nblk = l // MOBA_BLOCK
    heads_out = jax.ShapeDtypeStruct((b, l, H_B, DH_B), F32)
    kb, vt, km, k_new, v_new = pl.pallas_call(
        _moba_kv_kernel,
        out_shape=(jax.ShapeDtypeStruct((b, l, D_B), BF16), jax.ShapeDtypeStruct((b, D_B, l), BF16),
                   jax.ShapeDtypeStruct((b, nblk, 1, D_B), F32), heads_out, heads_out),
        grid=(b, nblk),
        in_specs=[pl.BlockSpec((None, MOBA_BLOCK, D_B), lambda bi, n: (bi, n, OFF_KB // D_B)),
                  pl.BlockSpec((None, MOBA_BLOCK, D_B), lambda bi, n: (bi, n, OFF_VB // D_B)),
                  pl.BlockSpec((None, 1, DH_B), lambda bi, n: (layer, 0, 0))],
        out_specs=(pl.BlockSpec((None, MOBA_BLOCK, D_B), lambda bi, n: (bi, n, 0)),
                   pl.BlockSpec((None, D_B, MOBA_BLOCK), lambda bi, n: (bi, 0, n)),
                   pl.BlockSpec((None, None, 1, D_B), lambda bi, n: (bi, n, 0, 0)),
                   pl.BlockSpec(memory_space=pl.ANY), pl.BlockSpec(memory_space=pl.ANY)),
        scratch_shapes=[pltpu.VMEM((2, MOBA_BLOCK, D_B), F32), pltpu.VMEM((2, MOBA_BLOCK, D_B), F32),
                        pltpu.SemaphoreType.DMA((2,))],
        compiler_params=_params("arbitrary", "arbitrary"),
        name="moba_kv",
    )(z, z, g_kb)
    return kb, vt, km.reshape(b, nblk, D_B), k_new, v_new


def _topk_mask_t(gate_t, n_valid, topk):
    n, r = gate_t.shape
    rowi = lax.broadcasted_iota(jnp.int32, (n, r), 0)
    gm = jnp.where(rowi < n_valid, gate_t, -jnp.inf)
    sel = jnp.zeros((n, r), F32)
    for j in range(n):
        gj = gm[j:j + 1]
        beats = jnp.where(gm > gj, 1.0, jnp.where((gm == gj) & (rowi < j), 1.0, 0.0))
        rank = jnp.sum(beats, axis=0, keepdims=True)
        sel = jnp.where((rowi == j) & (rank < topk) & (rowi < n_valid), 1.0, sel)
    return sel


def _moba_prompt_kernel(slopes_ref, q_ref, gq_ref, k_ref, vt_ref, km_ref, o_ref, s_ref, qb_ref, sel_ref):
    hg = pl.program_id(1)
    i = pl.program_id(2)
    blk = MOBA_BLOCK
    scale = DH_B ** -0.5
    heads = [slice(hh * DH_B, (hh + 1) * DH_B) for hh in range(MOBA_HEADS_PER_STEP)]

    @pl.when(i == 0)
    def _():
        for hh, sl in enumerate(heads):
            qn = _rms(q_ref[:, sl], gq_ref[...])
            qb_ref[:, sl] = qn.astype(BF16)
            gate_t = lax.dot_general(km_ref[:, sl], qn, _NT, precision=lax.Precision.HIGHEST,
                                     preferred_element_type=F32)
            q_block = lax.broadcasted_iota(jnp.int32, gate_t.shape, 1) // blk
            sel_ref[hh] = _topk_mask_t(gate_t, q_block, MOBA_TOPK)

    d0 = (lax.broadcasted_iota(jnp.int32, (blk, blk), 1)
          - lax.broadcasted_iota(jnp.int32, (blk, blk), 0)).astype(F32)
    slopes = [slopes_ref[hg * MOBA_HEADS_PER_STEP + hh] for hh in range(MOBA_HEADS_PER_STEP)]
    bias0 = [-slope * d0 for slope in slopes]

    def attend(own):
        cols = slice(own * blk, (own + 1) * blk)
        m = [None] * len(heads)
        for j in range(own + 1):
            rows = slice(j * blk, (j + 1) * blk)
            for hh, sl in enumerate(heads):
                s = lax.dot_general(k_ref[rows, sl], qb_ref[cols, sl], _NT, preferred_element_type=F32) * scale
                s = s + (bias0[hh] - slopes[hh] * float((own - j) * blk))
                keep = (d0 >= 0.0) if j == own else (sel_ref[hh, j:j + 1, cols] > 0.0)
                s = jnp.where(keep, s, -jnp.inf)
                s_ref[hh, rows, :] = s
                m_j = jnp.max(s, axis=0, keepdims=True)
                m[hh] = m_j if m[hh] is None else jnp.maximum(m[hh], m_j)
        l = [jnp.zeros((1, blk), F32) for _ in heads]
        acc = [jnp.zeros((DH_B, blk), F32) for _ in heads]
        for j in range(own + 1):
            rows = slice(j * blk, (j + 1) * blk)
            for hh, sl in enumerate(heads):
                p = jnp.exp(s_ref[hh, rows, :] - m[hh])
                l[hh] = l[hh] + jnp.sum(p, axis=0, keepdims=True)
                acc[hh] = acc[hh] + _dot(vt_ref[sl, rows], p.astype(BF16))
        for hh, sl in enumerate(heads):
            o_ref[:, sl] = (acc[hh] / l[hh]).T.astype(o_ref.dtype)

    for own in range(km_ref.shape[0]):
        pl.when(i == own)(functools.partial(attend, own))


def _moba_prompt(z, g_qb, kb, vt, kmean, slopes, layer):
    b, l, _ = z.shape
    nblk = l // MOBA_BLOCK
    width = MOBA_HEADS_PER_STEP * DH_B
    return pl.pallas_call(
        _moba_prompt_kernel,
        out_shape=jax.ShapeDtypeStruct((b, l, D_B), BF16),
        grid=(b, H_B // MOBA_HEADS_PER_STEP, nblk),
        in_specs=[pl.BlockSpec(memory_space=pltpu.SMEM),
                  pl.BlockSpec((None, l, width), lambda bi, h, i: (bi, 0, OFF_QB // width + h)),
                  pl.BlockSpec((None, 1, DH_B), lambda bi, h, i: (layer, 0, 0)),
                  pl.BlockSpec((None, l, width), lambda bi, h, i: (bi, 0, h)),
                  pl.BlockSpec((None, width, l), lambda bi, h, i: (bi, h, 0)),
                  pl.BlockSpec((None, nblk, width), lambda bi, h, i: (bi, 0, h))],
        out_specs=pl.BlockSpec((None, MOBA_BLOCK, width), lambda bi, h, i: (bi, i, h)),
        scratch_shapes=[pltpu.VMEM((MOBA_HEADS_PER_STEP, l, MOBA_BLOCK), F32), pltpu.VMEM((l, width), BF16),
                        pltpu.VMEM((MOBA_HEADS_PER_STEP, nblk, l), F32)],
        compiler_params=_params("parallel", "parallel", "arbitrary"),
        name="moba_prompt",
    )(slopes, z, g_qb, kb, vt, kmean)


def _page_mean_kernel(pt_ref, *refs):
    del pt_ref
    page_refs, o_ref = refs[:-1], refs[-1]
    for n in range(PAGE_MEAN_BLOCKS):
        pages = page_refs[n * PAGES_PER_BLOCK:(n + 1) * PAGES_PER_BLOCK]
        total = jnp.sum(pages[0][...], axis=0)
        for r in pages[1:]:
            total = total + jnp.sum(r[...], axis=0)
        o_ref[n] = total * (1.0 / MOBA_BLOCK)


def _page_means(cache_k, page_table_flat, layer, n_seq, n_pages):
    n_blocks = n_pages // PAGES_PER_BLOCK
    assert n_blocks % PAGE_MEAN_BLOCKS == 0
    pages_per_step = PAGE_MEAN_BLOCKS * PAGES_PER_BLOCK

    def page_spec(p):
        return pl.BlockSpec(
            (None, None, PAGE_SIZE, H_B, DH_B),
            lambda bi, n, pt: (layer, pt[bi * n_pages + n * pages_per_step + p], 0, 0, 0))

    return pl.pallas_call(
        _page_mean_kernel,
        out_shape=jax.ShapeDtypeStruct((n_seq, n_blocks, H_B, DH_B), F32),
        grid_spec=pltpu.PrefetchScalarGridSpec(
            num_scalar_prefetch=1,
            grid=(n_seq, n_blocks // PAGE_MEAN_BLOCKS),
            in_specs=[page_spec(p) for p in range(pages_per_step)],
            out_specs=pl.BlockSpec((None, PAGE_MEAN_BLOCKS, H_B, DH_B), lambda bi, n, pt: (bi, n, 0, 0))),
        compiler_params=_params("parallel", "parallel"),
        name="moba_page_means",
    )(page_table_flat, *([cache_k] * pages_per_step))


def _moba_select_kernel(q_ref, gq_ref, km_ref, qn_ref, sel_ref):
    rows = q_ref.shape[0]
    n_blocks = km_ref.shape[0]
    g = gq_ref[...]
    col = lax.broadcasted_iota(jnp.int32, (rows, n_blocks), 1)
    lane = lax.broadcasted_iota(jnp.int32, (rows, sel_ref.shape[-1]), 1)
    for h in range(H_B):
        sl = slice(h * DH_B, (h + 1) * DH_B)
        qn = _rms(q_ref[:, sl], g)
        qn_ref[:, sl] = qn
        gate = lax.dot_general(qn, km_ref[:, sl], _NT, precision=lax.Precision.HIGHEST,
                               preferred_element_type=F32)
        picks = jnp.zeros(lane.shape, jnp.int32)
        for r in range(MOBA_TOPK):
            best = jnp.max(gate, axis=-1, keepdims=True)
            idx = jnp.min(jnp.where(gate == best, col, n_blocks), axis=-1, keepdims=True)
            picks = jnp.where(lane == r, idx, picks)
            gate = jnp.where(col == idx, -jnp.inf, gate)
        sel_ref[h] = picks


def _moba_select(z, g_qb, kmean, layer):
    b, rows, _ = z.shape
    n_blocks = kmean.shape[1]
    return pl.pallas_call(
        _moba_select_kernel,
        out_shape=(jax.ShapeDtypeStruct((b, rows, D_B), F32),
                   jax.ShapeDtypeStruct((b, H_B, rows, 128), jnp.int32)),
        grid=(b,),
        in_specs=[pl.BlockSpec((None, rows, D_B), lambda bi: (bi, 0, OFF_QB // D_B)),
                  pl.BlockSpec((None, 1, DH_B), lambda bi: (layer, 0, 0)),
                  pl.BlockSpec((None, n_blocks, D_B), lambda bi: (bi, 0, 0))],
        out_specs=(pl.BlockSpec((None, rows, D_B), lambda bi: (bi, 0, 0)),
                   pl.BlockSpec((None, H_B, rows, 128), lambda bi: (bi, 0, 0, 0))),
        compiler_params=_params("parallel"),
        name="moba_select",
    )(z, g_qb, kmean)


def _moba_sample_kernel(sel_ref, pt_ref, slopes_ref, q_ref, kown_ref, vown_ref, ck_ref, cv_ref, o_ref,
                        kbuf, vbuf, sem, *, layer, n_tok, n_pages):
    step = pl.program_id(0)
    n_steps = pl.num_programs(0)
    tiles_per_tok = MOBA_TOPK * PAGES_PER_BLOCK
    past_len = n_pages * PAGE_SIZE

    def picked_block(st, t, s):
        return sel_ref[(st * n_tok + t) * MOBA_TOPK + s]

    def tile_copies(st, slot):
        bi, h = st // H_B, st % H_B
        copies = []
        for t in range(n_tok):
            for s in range(MOBA_TOPK):
                block = picked_block(st, t, s)
                for p in range(PAGES_PER_BLOCK):
                    page = pt_ref[bi * n_pages + block * PAGES_PER_BLOCK + p]
                    j = t * tiles_per_tok + s * PAGES_PER_BLOCK + p
                    copies.append(pltpu.make_async_copy(
                        ck_ref.at[layer, page, :, h, :], kbuf.at[slot, j], sem.at[slot]))
                    copies.append(pltpu.make_async_copy(
                        cv_ref.at[layer, page, :, h, :], vbuf.at[slot, j], sem.at[slot]))
        return copies

    slot = step % 2

    @pl.when(step == 0)
    def _():
        for c in tile_copies(step, slot):
            c.start()

    @pl.when(step + 1 < n_steps)
    def _():
        for c in tile_copies(step + 1, 1 - slot):
            c.start()

    for c in tile_copies(step, slot):
        c.wait()

    slope = slopes_ref[step % H_B]
    scale = DH_B ** -0.5
    rows = kown_ref.shape[0]
    k_own, v_own = kown_ref[...], vown_ref[...]
    own_pos = past_len + lax.broadcasted_iota(jnp.int32, (rows, 1), 0)
    page_row = lax.broadcasted_iota(jnp.int32, (PAGE_SIZE, 1), 0)
    out_row = lax.broadcasted_iota(jnp.int32, (rows, 1), 0)
    out = jnp.zeros((rows, DH_B), F32)
    for t in range(n_tok):
        q = q_ref[t:t + 1, :]
        t_pos = past_len + t

        def score(keys, k_pos):
            return (jnp.sum(keys * q, axis=-1, keepdims=True) * scale
                    - slope * (t_pos - k_pos).astype(F32))

        scores = [jnp.where(own_pos <= t_pos, score(k_own, own_pos), -jnp.inf)]
        values = [v_own]
        for s in range(MOBA_TOPK):
            block = picked_block(step, t, s)
            for p in range(PAGES_PER_BLOCK):
                j = t * tiles_per_tok + s * PAGES_PER_BLOCK + p
                scores.append(score(kbuf[slot, j], block * MOBA_BLOCK + p * PAGE_SIZE + page_row))
                values.append(vbuf[slot, j])
        m = functools.reduce(jnp.maximum, [jnp.max(s, axis=0, keepdims=True) for s in scores])
        probs = [jnp.exp(s - m) for s in scores]
        l = functools.reduce(jnp.add, [jnp.sum(p, axis=0, keepdims=True) for p in probs])
        acc = functools.reduce(jnp.add, [jnp.sum(p * v, axis=0, keepdims=True) for p, v in zip(probs, values)])
        out = jnp.where(out_row == t, acc / l, out)
    o_ref[...] = out.astype(o_ref.dtype)


def _moba_sample(qn, kn, z, cache_k, cache_v, picks_flat, page_table_flat, slopes, layer, *, n_tok, n_pages):
    b, rows, _ = qn.shape
    n_tiles = n_tok * MOBA_TOPK * PAGES_PER_BLOCK
    own_spec = pl.BlockSpec((None, rows, DH_B), lambda st, sel, pt: (st // H_B, 0, st % H_B))
    return pl.pallas_call(
        functools.partial(_moba_sample_kernel, layer=layer, n_tok=n_tok, n_pages=n_pages),
        out_shape=jax.ShapeDtypeStruct((b, rows, D_B), BF16),
        grid_spec=pltpu.PrefetchScalarGridSpec(
            num_scalar_prefetch=2,
            grid=(b * H_B,),
            in_specs=[pl.BlockSpec(memory_space=pltpu.SMEM), own_spec, own_spec,
                      pl.BlockSpec((None, rows, DH_B),
                                   lambda st, sel, pt: (st // H_B, 0, OFF_VB // DH_B + st % H_B)),
                      pl.BlockSpec(memory_space=pl.ANY), pl.BlockSpec(memory_space=pl.ANY)],
            out_specs=pl.BlockSpec((None, rows, DH_B), lambda st, sel, pt: (st // H_B, 0, st % H_B)),
            scratch_shapes=[pltpu.VMEM((2, n_tiles, PAGE_SIZE, DH_B), F32),
                            pltpu.VMEM((2, n_tiles, PAGE_SIZE, DH_B), F32),
                            pltpu.SemaphoreType.DMA((2,))]),
        compiler_params=_params("arbitrary"),
        name="moba_sample",
    )(picks_flat, page_table_flat, slopes, qn, kn, z, cache_k, cache_v)


def _mem_attn_kernel(q_ref, gq_ref, k_ref, v_ref, o_ref):
    qn = _rms(q_ref[...], gq_ref[...])
    s = lax.dot_general(qn.astype(BF16), k_ref[...].astype(BF16), _NT, preferred_element_type=F32)
    s = s * (DH_C ** -0.5)
    e = jnp.exp(s - jnp.max(s, axis=-1, keepdims=True))
    p = e / jnp.sum(e, axis=-1, keepdims=True)
    o_ref[...] = _dot(p.astype(BF16), v_ref[...].astype(BF16)).astype(o_ref.dtype)


def _mem_attn(z, g_qc, mem_k, mem_v, layer, *, kv_index, v_col0, tq):
    b, l, _ = z.shape
    tq = _tile(l, tq)
    k_block = (None,) * (mem_k.ndim - 2) + (N_MEM, DH_C)
    v_block = (None,) * (mem_v.ndim - 2) + (N_MEM, DH_C)
    return pl.pallas_call(
        _mem_attn_kernel,
        out_shape=jax.ShapeDtypeStruct((b, l, D_C), BF16),
        grid=(b, H_C, l // tq),
        in_specs=[pl.BlockSpec((None, tq, DH_C), lambda bi, h, i: (bi, i, OFF_QC // DH_C + h)),
                  pl.BlockSpec((None, 1, DH_C), lambda bi, h, i: (layer, 0, 0)),
                  pl.BlockSpec(k_block, lambda bi, h, i: kv_index(bi, h, 0)),
                  pl.BlockSpec(v_block, lambda bi, h, i: kv_index(bi, h, v_col0))],
        out_specs=pl.BlockSpec((None, tq, DH_C), lambda bi, h, i: (bi, i, h)),
        compiler_params=_params("parallel", "parallel", "arbitrary"),
        name="mem_attn",
    )(z, g_qc, mem_k, mem_v)


def _mixer_and_ffn(xs, layer, w, branches, next_page_means=None):
    shapes = [x.shape[:2] for x in xs]
    x2 = [x.reshape(-1, D_MODEL) for x in xs]
    h = [_rmsnorm(x, w["norm_mix"], layer) for x in x2]
    z2 = _mm_groups([[a] for a in h], [w["w_in"]], layer, epilogue=_ep_plain, out_dtype=F32,
                    tm=1024, tn=1024, name="mm_in")
    outs = [fn(z.reshape(*s, D_IN)) for fn, z, s in zip(branches, z2, shapes)]
    gates = [(z2, OFF_GATE + k * D_MODEL) for k in range(N_BRANCH)]
    merged = _mm_groups([list(o[:N_BRANCH]) for o in outs], [w["w_br_a"], w["w_br_b"], w["w_br_c"]], layer,
                        epilogue=_ep_gated_sum, extras=gates, out_dtype=BF16, tm=1024, tn=512, name="mm_merge")
    x2 = _mm_groups([[a] for a in merged], [w["w_out"]], layer, epilogue=_ep_residual, extras=[(x2, 0)],
                    out_dtype=F32, tm=1024, tn=1024, name="mm_out")
    h = [_rmsnorm(x, w["norm_ffn"], layer) for x in x2]
    act = _mm_groups([[a] for a in h], [w["w_gate"], w["w_up"]], layer, act_of_w=(0, 0), epilogue=_ep_swiglu,
                     out_dtype=BF16, tm=1024, tn=512, name="mm_swiglu", page_means=next_page_means)
    act, kmean_next = (act[:-1], act[-1]) if next_page_means is not None else (act, None)
    x2 = _mm_groups([[a] for a in act], [w["w_down"]], layer, epilogue=_ep_residual, extras=[(x2, 0)],
                    out_dtype=F32, tm=512, tn=512, name="mm_down")
    return ([x.reshape(*s, D_MODEL) for x, s in zip(x2, shapes)],
            [z.reshape(*s, D_IN) for z, s in zip(z2, shapes)], [o[N_BRANCH] for o in outs], kmean_next)


def kernel(x_prompt, x_sample, state_hgrn, cache_k, cache_v, cache_mem_k, cache_mem_v, page_table,
           mem_prompt, lb_logits, norm_mix, w_in, norm_o_a, norm_q_b, norm_k_b, norm_q_c, norm_k_c,
           norm_mem, w_mem_kv, w_br_a, w_br_b, w_br_c, w_out, norm_ffn, w_gate, w_up, w_down):
    depth = w_in.shape[0]
    n_prompt, seq, _ = x_prompt.shape
    n_seq, n_tok, _ = x_sample.shape
    n_pages = page_table.shape[1]
    assert seq % MOBA_BLOCK == 0 and n_pages % PAGES_PER_BLOCK == 0
    assert n_pages // PAGES_PER_BLOCK >= MOBA_TOPK and n_tok <= SAMPLE_ROWS <= MOBA_BLOCK

    row = lambda g: g.reshape(depth, 1, g.shape[-1])
    w = {"norm_mix": row(norm_mix), "norm_ffn": row(norm_ffn), "w_in": w_in, "w_br_a": w_br_a, "w_br_b": w_br_b,
         "w_br_c": w_br_c, "w_out": w_out, "w_gate": w_gate, "w_up": w_up, "w_down": w_down}
    g_oa, g_qb, g_kb, g_qc, g_kc, g_mem = (row(g) for g in (norm_o_a, norm_q_b, norm_k_b, norm_q_c,
                                                           norm_k_c, norm_mem))
    lower = _lower_bounds(lb_logits).reshape(depth, 1, D_A)
    slopes = 2.0 ** (-8.0 * jnp.arange(1, H_B + 1, dtype=F32) / H_B)

    cmk = cache_mem_k.reshape(depth, n_seq, N_MEM, D_C)
    cmv = cache_mem_v.reshape(depth, n_seq, N_MEM, D_C)
    pt_flat = page_table.reshape(-1)
    s0_prompt = jnp.zeros((1, n_prompt, H_A, DK_A, DV_A), F32)
    mem2 = mem_prompt.reshape(n_prompt * N_MEM, D_MODEL)

    xp = x_prompt
    xs = jnp.pad(x_sample, ((0, 0), (0, SAMPLE_ROWS - n_tok), (0, 0)))
    outs = {k: [] for k in ("hp", "kp", "vp", "mkp", "mvp", "hs", "ks", "vs")}

    kmean_layer = _page_means(cache_k, pt_flat, 0, n_seq, n_pages)
    for layer in range(depth):
        (kv,) = _mm_groups([[_rmsnorm(mem2, g_mem, layer)]], [w_mem_kv], layer, epilogue=_ep_plain,
                           out_dtype=F32, tm=1024, tn=1024, name="mm_mem")
        kv = kv.reshape(n_prompt, N_MEM, 2 * D_C)
        mem_k = _head_norm(kv, g_kc, layer, col_block=0, heads=H_C, hd=DH_C)
        mem_v = kv[:, :, D_C:]

        def prompt_branches(z):
            o_a, s_new = _hgrn2(z, lower, g_oa, s0_prompt, layer, 0, chunk=64)
            kb, vt, kmean, k_new, v_new = _moba_kv(z, g_kb, layer)
            o_b = _moba_prompt(z, g_qb, kb, vt, kmean, slopes, layer)
            o_c = _mem_attn(z, g_qc, mem_k, kv, layer, tq=2048, v_col0=D_C // DH_C,
                            kv_index=lambda bi, h, c0: (bi, 0, c0 + h))
            m = n_prompt * seq
            return o_a.reshape(m, D_A), o_b.reshape(m, D_B), o_c.reshape(m, D_C), (s_new, (k_new, v_new))

        def sample_branches(z):
            o_a, s_new = _hgrn2(z, lower, g_oa, state_hgrn, layer, layer, chunk=SAMPLE_ROWS, valid_len=n_tok)
            kn = _head_norm(z, g_kb, layer, col_block=OFF_KB // D_B, heads=H_B, hd=DH_B)
            kmean = kmean_layer.reshape(n_seq, -1, D_B)
            qn, picks = _moba_select(z, g_qb, kmean, layer)
            picks_flat = picks[:, :, :n_tok, :MOBA_TOPK].reshape(-1)
            o_b = _moba_sample(qn, kn, z, cache_k, cache_v, picks_flat, pt_flat, slopes, layer,
                               n_tok=n_tok, n_pages=n_pages)
            o_c = _mem_attn(z, g_qc, cmk, cmv, layer, tq=SAMPLE_ROWS, v_col0=0,
                            kv_index=lambda bi, h, c0: (layer, bi, 0, c0 + h))
            m = n_seq * SAMPLE_ROWS
            return o_a.reshape(m, D_A), o_b.reshape(m, D_B), o_c.reshape(m, D_C), (s_new, kn)

        next_means = (cache_k, pt_flat, layer + 1, n_seq, n_pages) if layer + 1 < depth else None
        (xp, xs), (_, zs), ((s_p, (k_p, v_p)), (s_s, k_s)), kmean_layer = _mixer_and_ffn(
            [xp, xs], layer, w, [prompt_branches, sample_branches], next_means)

        outs["hp"].append(s_p)
        outs["kp"].append(k_p)
        outs["vp"].append(v_p)
        outs["mkp"].append(mem_k.reshape(n_prompt, N_MEM, H_C, DH_C))
        outs["mvp"].append(mem_v.reshape(n_prompt, N_MEM, H_C, DH_C))
        outs["hs"].append(s_s)
        outs["ks"].append(k_s[:, :n_tok].reshape(n_seq, n_tok, H_B, DH_B))
        outs["vs"].append(zs[:, :n_tok, OFF_VB:OFF_VB + D_B].reshape(n_seq, n_tok, H_B, DH_B))

    stack = lambda k: jnp.stack(outs[k])
    return (xp, xs[:, :n_tok], stack("hp"), stack("kp"), stack("vp"), stack("mkp"), stack("mvp"),
            stack("hs"), stack("ks"), stack("vs"))
```

```python
import functools

import jax
import jax.numpy as jnp
from jax import lax
from jax.experimental import pallas as pl
from jax.experimental.pallas import tpu as pltpu

D_MODEL = 2048
H_A, DK_A, DV_A = 8, 128, 128
D_A = H_A * DK_A
H_B, DH_B = 8, 128
D_B = H_B * DH_B
MOBA_BLOCK = 256
MOBA_TOPK = 3
PAGE_SIZE = 128
PAGES_PER_BLOCK = MOBA_BLOCK // PAGE_SIZE
N_MEM = 256
H_C, DH_C = 4, 256
D_C = H_C * DH_C
N_BRANCH = 3
D_IN = 4 * D_A + 3 * D_B + D_C + N_BRANCH * D_MODEL
EPS = 1e-6

OFF_QA, OFF_FA, OFF_IA, OFF_GA = 0, D_A, 2 * D_A, 3 * D_A
OFF_QB = 4 * D_A
OFF_KB = OFF_QB + D_B
OFF_VB = OFF_KB + D_B
OFF_QC = OFF_VB + D_B
OFF_GATE = OFF_QC + D_C

GLA_SUB = 16
GLA_SAFE_DROP = 80.0
SAMPLE_ROWS = 16
MOBA_HEADS_PER_STEP = 4
PAGE_MEAN_BLOCKS = 4
V7X_VMEM_LIMIT = 56 * 1024 * 1024

F32 = jnp.float32
BF16 = jnp.bfloat16
_NT = (((1,), (1,)), ((), ()))
_TN = (((0,), (0,)), ((), ()))


def _params(*semantics):
    return pltpu.CompilerParams(dimension_semantics=semantics, vmem_limit_bytes=V7X_VMEM_LIMIT)


def _tile(n, pref):
    t = min(n, pref)
    assert n % t == 0, (n, pref)
    return t


def _rms(x, gain):
    return x * lax.rsqrt(jnp.mean(x * x, axis=-1, keepdims=True) + EPS) * gain


def _rmsnorm_kernel(x_ref, g_ref, o_ref):
    o_ref[...] = _rms(x_ref[...], g_ref[...]).astype(o_ref.dtype)


def _rmsnorm(x, gains, layer):
    m, d = x.shape
    tm = _tile(m, 512)
    return pl.pallas_call(
        _rmsnorm_kernel,
        out_shape=jax.ShapeDtypeStruct((m, d), BF16),
        grid=(m // tm,),
        in_specs=[pl.BlockSpec((tm, d), lambda i: (i, 0)),
                  pl.BlockSpec((None, 1, d), lambda i: (layer, 0, 0))],
        out_specs=pl.BlockSpec((tm, d), lambda i: (i, 0)),
        compiler_params=_params("parallel"),
        name="rmsnorm",
    )(x, gains)


def _dot(a, b):
    return jnp.dot(a, b, preferred_element_type=F32)


def _mm_groups_kernel(*refs, n_groups, n_w, act_of_w, n_extra, epilogue, side=None):
    n_act = len(set(act_of_w))
    pos = 0
    acts = [refs[pos + g * n_act:pos + (g + 1) * n_act] for g in range(n_groups)]
    pos += n_groups * n_act
    weights = refs[pos:pos + n_w]
    pos += n_w
    extras = [refs[pos + g * n_extra:pos + (g + 1) * n_extra] for g in range(n_groups)]
    pos += n_groups * n_extra
    outs = refs[pos:pos + n_groups]
    wb = refs[pos + n_groups:]

    def compute(g):
        dots = [_dot(acts[g][act_of_w[k]][...], wb[k][...]) for k in range(n_w)]
        outs[g][...] = epilogue(dots, [e[...] for e in extras[g]]).astype(outs[g].dtype)

    if side is not None:
        side[0]()

    @pl.when(pl.program_id(1) == 0)
    def _():
        for k in range(n_w):
            wb[k][...] = weights[k][...].astype(BF16)
        for g in range(1, n_groups):
            compute(g)

    compute(0)
    if side is not None:
        side[1]()


def _page_mean_rider(pt_ref, cache_ref, km_ref, buf, sem, *, layer, n_seq, n_pages, per_step):
    blocks_per_seq = n_pages // PAGES_PER_BLOCK
    last_block = n_seq * blocks_per_seq - 1
    step = pl.program_id(0) * pl.num_programs(1) + pl.program_id(1)
    n_steps = pl.num_programs(0) * pl.num_programs(1)
    slot = step % 2

    def block_of(st, r):
        g = jnp.minimum(st * per_step + r, last_block)
        return g // blocks_per_seq, g % blocks_per_seq

    def page_copies(st, into):
        copies = []
        for r in range(per_step):
            seq, blk = block_of(st, r)
            for p in range(PAGES_PER_BLOCK):
                page = pt_ref[seq * n_pages + blk * PAGES_PER_BLOCK + p]
                copies.append(pltpu.make_async_copy(cache_ref.at[layer, page],
                                                    buf.at[into, r * PAGES_PER_BLOCK + p], sem.at[into]))
        return copies

    def before():
        @pl.when(step == 0)
        def _():
            for c in page_copies(step, slot):
                c.start()

        @pl.when(step + 1 < n_steps)
        def _():
            for c in page_copies(step + 1, 1 - slot):
                c.start()

        for c in page_copies(step, slot):
            c.wait()

    def beside():
        for r in range(per_step):
            seq, blk = block_of(step, r)
            total = jnp.sum(buf[slot, r * PAGES_PER_BLOCK], axis=0)
            for p in range(1, PAGES_PER_BLOCK):
                total = total + jnp.sum(buf[slot, r * PAGES_PER_BLOCK + p], axis=0)
            km_ref[seq, blk] = total * (1.0 / MOBA_BLOCK)

    return before, beside


def _mm_groups_rider_kernel(pt_ref, *refs, n_in, n_out, n_w, mm_kwargs, rider_kwargs):
    ins, cache_ref = refs[:n_in], refs[n_in]
    outs, km_ref = refs[n_in + 1:n_in + 1 + n_out], refs[n_in + 1 + n_out]
    wb = refs[n_in + 2 + n_out:n_in + 2 + n_out + n_w]
    buf, sem = refs[n_in + 2 + n_out + n_w:]
    side = _page_mean_rider(pt_ref, cache_ref, km_ref, buf, sem, **rider_kwargs)
    _mm_groups_kernel(*ins, *outs, *wb, side=side, **mm_kwargs)


def _mm_groups(acts, weights, layer, *, epilogue, out_dtype, tm, tn, act_of_w=None, extras=(), name,
               page_means=None):
    n_groups, n_w = len(acts), len(weights)
    act_of_w = tuple(range(n_w)) if act_of_w is None else tuple(act_of_w)
    k, n = weights[0].shape[1:]
    m = [a[0].shape[0] for a in acts]
    tm, tn = _tile(m[0], tm), _tile(n, tn)
    rows = [tm] + m[1:]
    grid = (n // tn, m[0] // tm)

    def row_block(g):
        return (lambda j, i, *_: (i, 0)) if g == 0 else (lambda j, i, *_: (0, 0))

    def out_block(g, first=0):
        return (lambda j, i, *_: (i, first + j)) if g == 0 else (lambda j, i, *_: (0, first + j))

    in_specs, args = [], []
    for g in range(n_groups):
        in_specs += [pl.BlockSpec((rows[g], a.shape[1]), row_block(g)) for a in acts[g]]
        args += list(acts[g])
    in_specs += [pl.BlockSpec((None, k, tn), lambda j, i, *_: (layer, 0, j))] * n_w
    args += list(weights)
    for g in range(n_groups):
        for arrays, col0 in extras:
            in_specs.append(pl.BlockSpec((rows[g], tn), out_block(g, col0 // tn)))
            args.append(arrays[g])
    out_shape = [jax.ShapeDtypeStruct((m[g], n), out_dtype) for g in range(n_groups)]
    out_specs = [pl.BlockSpec((rows[g], tn), out_block(g)) for g in range(n_groups)]
    scratch = [pltpu.VMEM((k, tn), BF16)] * n_w
    mm_kwargs = dict(n_groups=n_groups, n_w=n_w, act_of_w=act_of_w, n_extra=len(extras), epilogue=epilogue)
    if page_means is None:
        return pl.pallas_call(
            functools.partial(_mm_groups_kernel, **mm_kwargs),
            out_shape=tuple(out_shape), grid=grid, in_specs=in_specs, out_specs=tuple(out_specs),
            scratch_shapes=scratch, compiler_params=_params("arbitrary", "arbitrary"), name=name,
        )(*args)

    cache_k, pt_flat, cache_layer, n_seq, n_pages = page_means
    n_blocks = n_pages // PAGES_PER_BLOCK
    per_step = -(-n_seq * n_blocks // (grid[0] * grid[1]))
    km_shape = (n_seq, n_blocks, H_B, DH_B)
    return pl.pallas_call(
        functools.partial(_mm_groups_rider_kernel, n_in=len(args), n_out=n_groups, n_w=n_w, mm_kwargs=mm_kwargs,
                          rider_kwargs=dict(layer=cache_layer, n_seq=n_seq, n_pages=n_pages, per_step=per_step)),
        out_shape=tuple(out_shape) + (jax.ShapeDtypeStruct(km_shape, F32),),
        grid_spec=pltpu.PrefetchScalarGridSpec(
            num_scalar_prefetch=1, grid=grid,
            in_specs=in_specs + [pl.BlockSpec(memory_space=pl.ANY)],
            out_specs=tuple(out_specs) + (pl.BlockSpec(km_shape, lambda j, i, *_: (0, 0, 0, 0)),),
            scratch_shapes=scratch + [pltpu.VMEM((2, per_step * PAGES_PER_BLOCK, PAGE_SIZE, H_B, DH_B), F32),
                                      pltpu.SemaphoreType.DMA((2,))]),
        compiler_params=_params("arbitrary", "arbitrary"), name=name,
    )(pt_flat, *args, cache_k)


def _ep_plain(dots, extras):
    return dots[0]


def _ep_residual(dots, extras):
    return extras[0] + dots[0]


def _ep_swiglu(dots, extras):
    return dots[0] * jax.nn.sigmoid(dots[0]) * dots[1]


def _ep_gated_sum(dots, extras):
    return functools.reduce(jnp.add, [jax.nn.sigmoid(g) * d for g, d in zip(extras, dots)])


def _lower_bounds_kernel(x_ref, o_ref):
    x = x_ref[...]
    e = jnp.exp(x - jnp.max(x, axis=0, keepdims=True))
    p = e / jnp.sum(e, axis=0, keepdims=True)
    rows = [jnp.zeros_like(p[0:1])]
    for r in range(1, x.shape[0]):
        rows.append(rows[-1] + p[r:r + 1])
    o_ref[...] = jnp.concatenate(rows, axis=0)


def _lower_bounds(lb_logits):
    return pl.pallas_call(
        _lower_bounds_kernel,
        out_shape=jax.ShapeDtypeStruct(lb_logits.shape, F32),
        name="hgrn2_lower_bounds",
    )(lb_logits)


def _hgrn2_pair_weights_factored(q, kk, G, g_mid):
    qt = (q * jnp.exp(G - g_mid)).astype(BF16)
    kt = (kk * jnp.exp(g_mid - G)).astype(BF16)
    a = lax.dot_general(qt, kt, _NT, preferred_element_type=F32)
    causal = lax.broadcasted_iota(jnp.int32, a.shape, 1) <= lax.broadcasted_iota(jnp.int32, a.shape, 0)
    return jnp.where(causal, a, 0.0).astype(BF16)


def _hgrn2_intra_exact(q, kk, v, G, rmod, chunk):
    pieces = [jnp.zeros((GLA_SUB, DV_A), F32)]
    for i in range(1, chunk // GLA_SUB):
        lo = i * GLA_SUB
        r = G[lo - 1:lo]
        qi = (q[lo:lo + GLA_SUB] * jnp.exp(G[lo:lo + GLA_SUB] - r)).astype(BF16)
        kj = (kk[:lo] * jnp.exp(r - G[:lo])).astype(BF16)
        a = lax.dot_general(qi, kj, _NT, preferred_element_type=F32)
        pieces.append(_dot(a.astype(BF16), v[:lo].astype(BF16)))
    o = jnp.concatenate(pieces, axis=0)
    for d in range(GLA_SUB):
        k_d = kk if d == 0 else pltpu.roll(kk, d, axis=0)
        g_d = G if d == 0 else pltpu.roll(G, d, axis=0)
        v_d = v if d == 0 else pltpu.roll(v, d, axis=0)
        decay = jnp.exp(jnp.where(rmod >= d, G - g_d, -jnp.inf))
        a_d = jnp.sum(q * k_d * decay, axis=-1, keepdims=True)
        o = o + a_d * v_d
    return o


def _hgrn2_kernel(q_ref, f_ref, i_ref, g_ref, lb_ref, gn_ref, s0_ref, o_ref, sout_ref,
                  st_ref, k_ref, qs_ref, G_ref, oi_ref, *, chunk, valid_len):
    c = pl.program_id(1)

    @pl.when(c == 0)
    def _():
        for h in range(H_A):
            st_ref[h] = s0_ref[h].T

    heads = [slice(h * DK_A, (h + 1) * DK_A) for h in range(H_A)]
    row = lax.broadcasted_iota(jnp.int32, (chunk, 1), 0)
    rmod = row % GLA_SUB
    tri = (lax.broadcasted_iota(jnp.int32, (chunk, chunk), 0)
           >= lax.broadcasted_iota(jnp.int32, (chunk, chunk), 1)).astype(F32)

    for sl in heads:
        zf = f_ref[:, sl]
        lb = lb_ref[:, sl]
        e = jnp.exp(-jnp.abs(zf))
        log_sig = jnp.minimum(zf, 0.0) - jnp.log1p(e)
        la = jnp.log(lb)
        lc = jnp.log1p(-lb) + log_sig
        log_f = jnp.maximum(la, lc) + jnp.log1p(jnp.exp(-jnp.abs(la - lc)))
        kk = (1.0 - lb) * (jnp.where(zf >= 0.0, e, 1.0) / (1.0 + e))
        if valid_len is not None:
            live = (c * chunk + row) < valid_len
            log_f = jnp.where(live, log_f, 0.0)
            kk = jnp.where(live, kk, 0.0)
        qr = q_ref[:, sl]
        k_ref[:, sl] = kk
        qs_ref[:, sl] = qr * jax.nn.sigmoid(qr) * (DK_A ** -0.5)
        G_ref[:, sl] = jnp.dot(tri, log_f, precision=lax.Precision.HIGHEST, preferred_element_type=F32)

    mid = chunk // 2
    g_mid = G_ref[mid - 1:mid, :]
    g_last = G_ref[chunk - 1:chunk, :]
    safe = jnp.max(jnp.maximum(-g_mid, g_mid - g_last)) < GLA_SAFE_DROP

    @pl.when(safe)
    def _():
        weights = [_hgrn2_pair_weights_factored(qs_ref[:, sl], k_ref[:, sl], G_ref[:, sl], g_mid[:, sl])
                   for sl in heads]
        for sl, a in zip(heads, weights):
            oi_ref[:, sl] = _dot(a, i_ref[:, sl].astype(BF16))

    @pl.when(jnp.logical_not(safe))
    def _():
        for sl in heads:
            oi_ref[:, sl] = _hgrn2_intra_exact(qs_ref[:, sl], k_ref[:, sl], i_ref[:, sl], G_ref[:, sl], rmod, chunk)

    carried = [lax.dot_general((qs_ref[:, sl] * jnp.exp(G_ref[:, sl])).astype(BF16), st_ref[h].astype(BF16),
                               _NT, preferred_element_type=F32) for h, sl in enumerate(heads)]
    added = [lax.dot_general(i_ref[:, sl].astype(BF16),
                             (k_ref[:, sl] * jnp.exp(g_last[:, sl] - G_ref[:, sl])).astype(BF16),
                             _TN, preferred_element_type=F32) for sl in heads]
    gn = gn_ref[...]
    for h, sl in enumerate(heads):
        st_ref[h] = st_ref[h] * jnp.exp(g_last[:, sl]) + added[h]
        gr = g_ref[:, sl]
        o = oi_ref[:, sl] + carried[h]
        o_ref[:, sl] = (_rms(o, gn) * (gr * jax.nn.sigmoid(gr))).astype(o_ref.dtype)

    @pl.when(c == pl.num_programs(1) - 1)
    def _():
        for h in range(H_A):
            sout_ref[h] = st_ref[h].T


def _hgrn2(z, lower, g_onorm, s0, layer, s0_layer, *, chunk, valid_len=None):
    b, l, _ = z.shape
    chunk = _tile(l, chunk)

    def col_spec(off):
        return pl.BlockSpec((None, chunk, D_A), lambda bi, c: (bi, c, off // D_A))

    return pl.pallas_call(
        functools.partial(_hgrn2_kernel, chunk=chunk, valid_len=valid_len),
        out_shape=(jax.ShapeDtypeStruct((b, l, D_A), BF16),
                   jax.ShapeDtypeStruct((b, H_A, DK_A, DV_A), F32)),
        grid=(b, l // chunk),
        in_specs=[col_spec(OFF_QA), col_spec(OFF_FA), col_spec(OFF_IA), col_spec(OFF_GA),
                  pl.BlockSpec((None, 1, D_A), lambda bi, c: (layer, 0, 0)),
                  pl.BlockSpec((None, 1, DV_A), lambda bi, c: (layer, 0, 0)),
                  pl.BlockSpec((None, None, H_A, DK_A, DV_A), lambda bi, c: (s0_layer, bi, 0, 0, 0))],
        out_specs=(pl.BlockSpec((None, chunk, D_A), lambda bi, c: (bi, c, 0)),
                   pl.BlockSpec((None, H_A, DK_A, DV_A), lambda bi, c: (bi, 0, 0, 0))),
        scratch_shapes=[pltpu.VMEM((H_A, DV_A, DK_A), F32)] + [pltpu.VMEM((chunk, D_A), F32)] * 4,
        compiler_params=_params("parallel", "arbitrary"),
        name="hgrn2",
    )(z, z, z, z, lower, g_onorm, s0)


def _head_norm_kernel(x_ref, g_ref, o_ref, *, heads, hd):
    g = g_ref[...]
    for h in range(heads):
        sl = slice(h * hd, (h + 1) * hd)
        o_ref[:, sl] = _rms(x_ref[:, sl], g)


def _head_norm(x, gains, layer, *, col_block, heads, hd):
    b, r, _ = x.shape
    width = heads * hd
    return pl.pallas_call(
        functools.partial(_head_norm_kernel, heads=heads, hd=hd),
        out_shape=jax.ShapeDtypeStruct((b, r, width), F32),
        grid=(b,),
        in_specs=[pl.BlockSpec((None, r, width), lambda bi: (bi, 0, col_block)),
                  pl.BlockSpec((None, 1, hd), lambda bi: (layer, 0, 0))],
        out_specs=pl.BlockSpec((None, r, width), lambda bi: (bi, 0, 0)),
        compiler_params=_params("parallel"),
        name="head_norm",
    )(x, gains)


def _moba_kv_kernel(k_ref, v_ref, g_ref, *refs, layer, n_carried):
    kb_ref, vt_ref, km_ref, knew_ref, vnew_ref, ks_ref, vs_ref, sem = refs[n_carried:]
    n_blk = pl.num_programs(1)
    step = pl.program_id(0) * n_blk + pl.program_id(1)
    n_steps = pl.num_programs(0) * n_blk
    slot = step % 2

    def out_copies(st, half):
        bi, n = st // n_blk, st % n_blk
        rows = pl.ds(pl.multiple_of(n * MOBA_BLOCK, MOBA_BLOCK), MOBA_BLOCK)
        copies = []
        for h in range(H_B):
            cols = slice(h * DH_B, (h + 1) * DH_B)
            copies.append(pltpu.make_async_copy(ks_ref.at[half, :, cols], knew_ref.at[layer, bi, rows, h, :], sem.at[half]))
            copies.append(pltpu.make_async_copy(vs_ref.at[half, :, cols], vnew_ref.at[layer, bi, rows, h, :], sem.at[half]))
        return copies

    @pl.when(step >= 2)
    def _():
        for c in out_copies(step - 2, slot):
            c.wait()

    g = g_ref[...]
    for h in range(H_B):
        sl = slice(h * DH_B, (h + 1) * DH_B)
        kn = _rms(k_ref[:, sl], g)
        ks_ref[slot, :, sl] = kn
        kb_ref[:, sl] = kn.astype(BF16)
        km_ref[:, sl] = jnp.mean(kn, axis=0, keepdims=True)
        vt_ref[sl, :] = v_ref[:, sl].T.astype(BF16)
    vs_ref[slot] = v_ref[...]
    for c in out_copies(step, slot):
        c.start()

    @pl.when((step == n_steps - 1) & (step >= 1))
    def _():
        for c in out_copies(step - 1, 1 - slot):
            c.wait()

    @pl.when(step == n_steps - 1)
    def _():
        for c in out_copies(step, slot):
            c.wait()


def _moba_kv(z, g_kb, layer, depth, stacked):
    b, l, _ = z.shape
    nblk = l // MOBA_BLOCK
    heads_out = jax.ShapeDtypeStruct((depth, b, l, H_B, DH_B), F32)
    hbm = pl.BlockSpec(memory_space=pl.ANY)
    carried = tuple(stacked)
    kb, vt, km, k_new, v_new = pl.pallas_call(
        functools.partial(_moba_kv_kernel, layer=layer, n_carried=len(carried)),
        out_shape=(jax.ShapeDtypeStruct((b, l, D_B), BF16), jax.ShapeDtypeStruct((b, D_B, l), BF16),
                   jax.ShapeDtypeStruct((b, nblk, 1, D_B), F32), heads_out, heads_out),
        grid=(b, nblk),
        in_specs=[pl.BlockSpec((None, MOBA_BLOCK, D_B), lambda bi, n: (bi, n, OFF_KB // D_B)),
                  pl.BlockSpec((None, MOBA_BLOCK, D_B), lambda bi, n: (bi, n, OFF_VB // D_B)),
                  pl.BlockSpec((None, 1, DH_B), lambda bi, n: (layer, 0, 0))] + [hbm] * len(carried),
        out_specs=(pl.BlockSpec((None, MOBA_BLOCK, D_B), lambda bi, n: (bi, n, 0)),
                   pl.BlockSpec((None, D_B, MOBA_BLOCK), lambda bi, n: (bi, 0, n)),
                   pl.BlockSpec((None, None, 1, D_B), lambda bi, n: (bi, n, 0, 0)), hbm, hbm),
        scratch_shapes=[pltpu.VMEM((2, MOBA_BLOCK, D_B), F32), pltpu.VMEM((2, MOBA_BLOCK, D_B), F32),
                        pltpu.SemaphoreType.DMA((2,))],
        input_output_aliases={3 + c: 3 + c for c in range(len(carried))},
        compiler_params=_params("arbitrary", "arbitrary"),
        name="moba_kv",
    )(z, z, g_kb, *carried)
    return kb, vt, km.reshape(b, nblk, D_B), (k_new, v_new)


def _topk_mask_t(gate_t, n_valid, topk):
    n, r = gate_t.shape
    rowi = lax.broadcasted_iota(jnp.int32, (n, r), 0)
    gm = jnp.where(rowi < n_valid, gate_t, -jnp.inf)
    sel = jnp.zeros((n, r), F32)
    for j in range(n):
        gj = gm[j:j + 1]
        beats = jnp.where(gm > gj, 1.0, jnp.where((gm == gj) & (rowi < j), 1.0, 0.0))
        rank = jnp.sum(beats, axis=0, keepdims=True)
        sel = jnp.where((rowi == j) & (rank < topk) & (rowi < n_valid), 1.0, sel)
    return sel


def _moba_prompt_kernel(slopes_ref, q_ref, gq_ref, k_ref, vt_ref, km_ref, o_ref, s_ref, qb_ref, sel_ref):
    hg = pl.program_id(1)
    i = pl.program_id(2)
    blk = MOBA_BLOCK
    scale = DH_B ** -0.5
    heads = [slice(hh * DH_B, (hh + 1) * DH_B) for hh in range(MOBA_HEADS_PER_STEP)]

    @pl.when(i == 0)
    def _():
        for hh, sl in enumerate(heads):
            qn = _rms(q_ref[:, sl], gq_ref[...])
            qb_ref[:, sl] = qn.astype(BF16)
            gate_t = lax.dot_general(km_ref[:, sl], qn, _NT, precision=lax.Precision.HIGHEST,
                                     preferred_element_type=F32)
            q_block = lax.broadcasted_iota(jnp.int32, gate_t.shape, 1) // blk
            sel_ref[hh] = _topk_mask_t(gate_t, q_block, MOBA_TOPK)

    d0 = (lax.broadcasted_iota(jnp.int32, (blk, blk), 1)
          - lax.broadcasted_iota(jnp.int32, (blk, blk), 0)).astype(F32)
    slopes = [slopes_ref[hg * MOBA_HEADS_PER_STEP + hh] for hh in range(MOBA_HEADS_PER_STEP)]
    bias0 = [-slope * d0 for slope in slopes]

    def attend(own):
        cols = slice(own * blk, (own + 1) * blk)
        m = [None] * len(heads)
        for j in range(own + 1):
            rows = slice(j * blk, (j + 1) * blk)
            for hh, sl in enumerate(heads):
                s = lax.dot_general(k_ref[rows, sl], qb_ref[cols, sl], _NT, preferred_element_type=F32) * scale
                s = s + (bias0[hh] - slopes[hh] * float((own - j) * blk))
                keep = (d0 >= 0.0) if j == own else (sel_ref[hh, j:j + 1, cols] > 0.0)
                s = jnp.where(keep, s, -jnp.inf)
                s_ref[hh, rows, :] = s
                m_j = jnp.max(s, axis=0, keepdims=True)
                m[hh] = m_j if m[hh] is None else jnp.maximum(m[hh], m_j)
        l = [jnp.zeros((1, blk), F32) for _ in heads]
        acc = [jnp.zeros((DH_B, blk), F32) for _ in heads]
        for j in range(own + 1):
            rows = slice(j * blk, (j + 1) * blk)
            for hh, sl in enumerate(heads):
                p = jnp.exp(s_ref[hh, rows, :] - m[hh])
                l[hh] = l[hh] + jnp.sum(p, axis=0, keepdims=True)
                acc[hh] = acc[hh] + _dot(vt_ref[sl, rows], p.astype(BF16))
        for hh, sl in enumerate(heads):
            o_ref[:, sl] = (acc[hh] / l[hh]).T.astype(o_ref.dtype)

    for own in range(km_ref.shape[0]):
        pl.when(i == own)(functools.partial(attend, own))


def _moba_prompt(z, g_qb, kb, vt, kmean, slopes, layer):
    b, l, _ = z.shape
    nblk = l // MOBA_BLOCK
    width = MOBA_HEADS_PER_STEP * DH_B
    return pl.pallas_call(
        _moba_prompt_kernel,
        out_shape=jax.ShapeDtypeStruct((b, l, D_B), BF16),
        grid=(b, H_B // MOBA_HEADS_PER_STEP, nblk),
        in_specs=[pl.BlockSpec(memory_space=pltpu.SMEM),
                  pl.BlockSpec((None, l, width), lambda bi, h, i: (bi, 0, OFF_QB // width + h)),
                  pl.BlockSpec((None, 1, DH_B), lambda bi, h, i: (layer, 0, 0)),
                  pl.BlockSpec((None, l, width), lambda bi, h, i: (bi, 0, h)),
                  pl.BlockSpec((None, width, l), lambda bi, h, i: (bi, h, 0)),
                  pl.BlockSpec((None, nblk, width), lambda bi, h, i: (bi, 0, h))],
        out_specs=pl.BlockSpec((None, MOBA_BLOCK, width), lambda bi, h, i: (bi, i, h)),
        scratch_shapes=[pltpu.VMEM((MOBA_HEADS_PER_STEP, l, MOBA_BLOCK), F32), pltpu.VMEM((l, width), BF16),
                        pltpu.VMEM((MOBA_HEADS_PER_STEP, nblk, l), F32)],
        compiler_params=_params("parallel", "parallel", "arbitrary"),
        name="moba_prompt",
    )(slopes, z, g_qb, kb, vt, kmean)


def _page_mean_kernel(pt_ref, *refs):
    del pt_ref
    page_refs, o_ref = refs[:-1], refs[-1]
    for n in range(PAGE_MEAN_BLOCKS):
        pages = page_refs[n * PAGES_PER_BLOCK:(n + 1) * PAGES_PER_BLOCK]
        total = jnp.sum(pages[0][...], axis=0)
        for r in pages[1:]:
            total = total + jnp.sum(r[...], axis=0)
        o_ref[n] = total * (1.0 / MOBA_BLOCK)


def _page_means(cache_k, page_table_flat, layer, n_seq, n_pages):
    n_blocks = n_pages // PAGES_PER_BLOCK
    assert n_blocks % PAGE_MEAN_BLOCKS == 0
    pages_per_step = PAGE_MEAN_BLOCKS * PAGES_PER_BLOCK

    def page_spec(p):
        return pl.BlockSpec(
            (None, None, PAGE_SIZE, H_B, DH_B),
            lambda bi, n, pt: (layer, pt[bi * n_pages + n * pages_per_step + p], 0, 0, 0))

    return pl.pallas_call(
        _page_mean_kernel,
        out_shape=jax.ShapeDtypeStruct((n_seq, n_blocks, H_B, DH_B), F32),
        grid_spec=pltpu.PrefetchScalarGridSpec(
            num_scalar_prefetch=1,
            grid=(n_seq, n_blocks // PAGE_MEAN_BLOCKS),
            in_specs=[page_spec(p) for p in range(pages_per_step)],
            out_specs=pl.BlockSpec((None, PAGE_MEAN_BLOCKS, H_B, DH_B), lambda bi, n, pt: (bi, n, 0, 0))),
        compiler_params=_params("parallel", "parallel"),
        name="moba_page_means",
    )(page_table_flat, *([cache_k] * pages_per_step))


def _moba_select_kernel(q_ref, gq_ref, km_ref, qn_ref, sel_ref):
    rows = q_ref.shape[0]
    n_blocks = km_ref.shape[0]
    g = gq_ref[...]
    col = lax.broadcasted_iota(jnp.int32, (rows, n_blocks), 1)
    lane = lax.broadcasted_iota(jnp.int32, (rows, sel_ref.shape[-1]), 1)
    for h in range(H_B):
        sl = slice(h * DH_B, (h + 1) * DH_B)
        qn = _rms(q_ref[:, sl], g)
        qn_ref[:, sl] = qn
        gate = lax.dot_general(qn, km_ref[:, sl], _NT, precision=lax.Precision.HIGHEST,
                               preferred_element_type=F32)
        picks = jnp.zeros(lane.shape, jnp.int32)
        for r in range(MOBA_TOPK):
            best = jnp.max(gate, axis=-1, keepdims=True)
            idx = jnp.min(jnp.where(gate == best, col, n_blocks), axis=-1, keepdims=True)
            picks = jnp.where(lane == r, idx, picks)
            gate = jnp.where(col == idx, -jnp.inf, gate)
        sel_ref[h] = picks


def _moba_select(z, g_qb, kmean, layer):
    b, rows, _ = z.shape
    n_blocks = kmean.shape[1]
    return pl.pallas_call(
        _moba_select_kernel,
        out_shape=(jax.ShapeDtypeStruct((b, rows, D_B), F32),
                   jax.ShapeDtypeStruct((b, H_B, rows, 128), jnp.int32)),
        grid=(b,),
        in_specs=[pl.BlockSpec((None, rows, D_B), lambda bi: (bi, 0, OFF_QB // D_B)),
                  pl.BlockSpec((None, 1, DH_B), lambda bi: (layer, 0, 0)),
                  pl.BlockSpec((None, n_blocks, D_B), lambda bi: (bi, 0, 0))],
        out_specs=(pl.BlockSpec((None, rows, D_B), lambda bi: (bi, 0, 0)),
                   pl.BlockSpec((None, H_B, rows, 128), lambda bi: (bi, 0, 0, 0))),
        compiler_params=_params("parallel"),
        name="moba_select",
    )(z, g_qb, kmean)


def _moba_sample_kernel(sel_ref, pt_ref, slopes_ref, q_ref, kown_ref, vown_ref, ck_ref, cv_ref, o_ref,
                        kbuf, vbuf, sem, *, layer, n_tok, n_pages):
    step = pl.program_id(0)
    n_steps = pl.num_programs(0)
    tiles_per_tok = MOBA_TOPK * PAGES_PER_BLOCK
    past_len = n_pages * PAGE_SIZE

    def picked_block(st, t, s):
        return sel_ref[(st * n_tok + t) * MOBA_TOPK + s]

    def tile_copies(st, slot):
        bi, h = st // H_B, st % H_B
        copies = []
        for t in range(n_tok):
            for s in range(MOBA_TOPK):
                block = picked_block(st, t, s)
                for p in range(PAGES_PER_BLOCK):
                    page = pt_ref[bi * n_pages + block * PAGES_PER_BLOCK + p]
                    j = t * tiles_per_tok + s * PAGES_PER_BLOCK + p
                    copies.append(pltpu.make_async_copy(
                        ck_ref.at[layer, page, :, h, :], kbuf.at[slot, j], sem.at[slot]))
                    copies.append(pltpu.make_async_copy(
                        cv_ref.at[layer, page, :, h, :], vbuf.at[slot, j], sem.at[slot]))
        return copies

    slot = step % 2

    @pl.when(step == 0)
    def _():
        for c in tile_copies(step, slot):
            c.start()

    @pl.when(step + 1 < n_steps)
    def _():
        for c in tile_copies(step + 1, 1 - slot):
            c.start()

    for c in tile_copies(step, slot):
        c.wait()

    slope = slopes_ref[step % H_B]
    scale = DH_B ** -0.5
    rows = kown_ref.shape[0]
    k_own, v_own = kown_ref[...], vown_ref[...]
    own_pos = past_len + lax.broadcasted_iota(jnp.int32, (rows, 1), 0)
    page_row = lax.broadcasted_iota(jnp.int32, (PAGE_SIZE, 1), 0)
    out_row = lax.broadcasted_iota(jnp.int32, (rows, 1), 0)
    out = jnp.zeros((rows, DH_B), F32)
    for t in range(n_tok):
        q = q_ref[t:t + 1, :]
        t_pos = past_len + t

        def score(keys, k_pos):
            return (jnp.sum(keys * q, axis=-1, keepdims=True) * scale
                    - slope * (t_pos - k_pos).astype(F32))

        scores = [jnp.where(own_pos <= t_pos, score(k_own, own_pos), -jnp.inf)]
        values = [v_own]
        for s in range(MOBA_TOPK):
            block = picked_block(step, t, s)
            for p in range(PAGES_PER_BLOCK):
                j = t * tiles_per_tok + s * PAGES_PER_BLOCK + p
                scores.append(score(kbuf[slot, j], block * MOBA_BLOCK + p * PAGE_SIZE + page_row))
                values.append(vbuf[slot, j])
        m = functools.reduce(jnp.maximum, [jnp.max(s, axis=0, keepdims=True) for s in scores])
        probs = [jnp.exp(s - m) for s in scores]
        l = functools.reduce(jnp.add, [jnp.sum(p, axis=0, keepdims=True) for p in probs])
        acc = functools.reduce(jnp.add, [jnp.sum(p * v, axis=0, keepdims=True) for p, v in zip(probs, values)])
        out = jnp.where(out_row == t, acc / l, out)
    o_ref[...] = out.astype(o_ref.dtype)


def _moba_sample(qn, kn, z, cache_k, cache_v, picks_flat, page_table_flat, slopes, layer, *, n_tok, n_pages):
    b, rows, _ = qn.shape
    n_tiles = n_tok * MOBA_TOPK * PAGES_PER_BLOCK
    own_spec = pl.BlockSpec((None, rows, DH_B), lambda st, sel, pt: (st // H_B, 0, st % H_B))
    return pl.pallas_call(
        functools.partial(_moba_sample_kernel, layer=layer, n_tok=n_tok, n_pages=n_pages),
        out_shape=jax.ShapeDtypeStruct((b, rows, D_B), BF16),
        grid_spec=pltpu.PrefetchScalarGridSpec(
            num_scalar_prefetch=2,
            grid=(b * H_B,),
            in_specs=[pl.BlockSpec(memory_space=pltpu.SMEM), own_spec, own_spec,
                      pl.BlockSpec((None, rows, DH_B),
                                   lambda st, sel, pt: (st // H_B, 0, OFF_VB // DH_B + st % H_B)),
                      pl.BlockSpec(memory_space=pl.ANY), pl.BlockSpec(memory_space=pl.ANY)],
            out_specs=pl.BlockSpec((None, rows, DH_B), lambda st, sel, pt: (st // H_B, 0, st % H_B)),
            scratch_shapes=[pltpu.VMEM((2, n_tiles, PAGE_SIZE, DH_B), F32),
                            pltpu.VMEM((2, n_tiles, PAGE_SIZE, DH_B), F32),
                            pltpu.SemaphoreType.DMA((2,))]),
        compiler_params=_params("arbitrary"),
        name="moba_sample",
    )(picks_flat, page_table_flat, slopes, qn, kn, z, cache_k, cache_v)


def _mem_attn_kernel(q_ref, gq_ref, k_ref, v_ref, o_ref):
    qn = _rms(q_ref[...], gq_ref[...])
    s = lax.dot_general(qn.astype(BF16), k_ref[...].astype(BF16), _NT, preferred_element_type=F32)
    s = s * (DH_C ** -0.5)
    e = jnp.exp(s - jnp.max(s, axis=-1, keepdims=True))
    p = e / jnp.sum(e, axis=-1, keepdims=True)
    o_ref[...] = _dot(p.astype(BF16), v_ref[...].astype(BF16)).astype(o_ref.dtype)


def _mem_attn(z, g_qc, mem_k, kv, layer, *, tq):
    b, l, _ = z.shape
    tq = _tile(l, tq)
    return pl.pallas_call(
        _mem_attn_kernel,
        out_shape=jax.ShapeDtypeStruct((b, l, D_C), BF16),
        grid=(b, H_C, l // tq),
        in_specs=[pl.BlockSpec((None, tq, DH_C), lambda bi, h, i: (bi, i, OFF_QC // DH_C + h)),
                  pl.BlockSpec((None, 1, DH_C), lambda bi, h, i: (layer, 0, 0)),
                  pl.BlockSpec((None, N_MEM, DH_C), lambda bi, h, i: (bi, 0, h)),
                  pl.BlockSpec((None, N_MEM, DH_C), lambda bi, h, i: (bi, 0, D_C // DH_C + h))],
        out_specs=pl.BlockSpec((None, tq, DH_C), lambda bi, h, i: (bi, i, h)),
        compiler_params=_params("parallel", "parallel", "arbitrary"),
        name="mem_attn",
    )(z, g_qc, mem_k, kv)


def _mem_attn_cached_kernel(q_ref, gq_ref, k_ref, v_ref, o_ref):
    g = gq_ref[...]
    heads = [slice(h * DH_C, (h + 1) * DH_C) for h in range(H_C)]
    scores = [lax.dot_general(_rms(q_ref[:, sl], g).astype(BF16), k_ref[:, h, :].astype(BF16), _NT,
                              preferred_element_type=F32) * (DH_C ** -0.5) for h, sl in enumerate(heads)]
    probs = []
    for s in scores:
        e = jnp.exp(s - jnp.max(s, axis=-1, keepdims=True))
        probs.append((e / jnp.sum(e, axis=-1, keepdims=True)).astype(BF16))
    for h, sl in enumerate(heads):
        o_ref[:, sl] = _dot(probs[h], v_ref[:, h, :].astype(BF16)).astype(o_ref.dtype)


def _mem_attn_cached(z, g_qc, cache_mem_k, cache_mem_v, layer):
    b, rows, _ = z.shape
    cache_spec = pl.BlockSpec((None, None, N_MEM, H_C, DH_C), lambda bi: (layer, bi, 0, 0, 0))
    return pl.pallas_call(
        _mem_attn_cached_kernel,
        out_shape=jax.ShapeDtypeStruct((b, rows, D_C), BF16),
        grid=(b,),
        in_specs=[pl.BlockSpec((None, rows, D_C), lambda bi: (bi, 0, OFF_QC // D_C)),
                  pl.BlockSpec((None, 1, DH_C), lambda bi: (layer, 0, 0)),
                  cache_spec, cache_spec],
        out_specs=pl.BlockSpec((None, rows, D_C), lambda bi: (bi, 0, 0)),
        compiler_params=_params("parallel"),
        name="mem_attn_cached",
    )(z, g_qc, cache_mem_k, cache_mem_v)


def _mixer_and_ffn(xs, layer, w, branches, next_page_means=None):
    shapes = [x.shape[:2] for x in xs]
    x2 = [x.reshape(-1, D_MODEL) for x in xs]
    h = [_rmsnorm(x, w["norm_mix"], layer) for x in x2]
    z2 = _mm_groups([[a] for a in h], [w["w_in"]], layer, epilogue=_ep_plain, out_dtype=F32,
                    tm=1024, tn=1024, name="mm_in")
    outs = [fn(z.reshape(*s, D_IN)) for fn, z, s in zip(branches, z2, shapes)]
    gates = [(z2, OFF_GATE + k * D_MODEL) for k in range(N_BRANCH)]
    merged = _mm_groups([list(o[:N_BRANCH]) for o in outs], [w["w_br_a"], w["w_br_b"], w["w_br_c"]], layer,
                        epilogue=_ep_gated_sum, extras=gates, out_dtype=BF16, tm=1024, tn=512, name="mm_merge")
    x2 = _mm_groups([[a] for a in merged], [w["w_out"]], layer, epilogue=_ep_residual, extras=[(x2, 0)],
                    out_dtype=F32, tm=1024, tn=1024, name="mm_out")
    h = [_rmsnorm(x, w["norm_ffn"], layer) for x in x2]
    act = _mm_groups([[a] for a in h], [w["w_gate"], w["w_up"]], layer, act_of_w=(0, 0), epilogue=_ep_swiglu,
                     out_dtype=BF16, tm=1024, tn=512, name="mm_swiglu", page_means=next_page_means)
    act, kmean_next = (act[:-1], act[-1]) if next_page_means is not None else (act, None)
    x2 = _mm_groups([[a] for a in act], [w["w_down"]], layer, epilogue=_ep_residual, extras=[(x2, 0)],
                    out_dtype=F32, tm=512, tn=512, name="mm_down")
    return ([x.reshape(*s, D_MODEL) for x, s in zip(x2, shapes)],
            [z.reshape(*s, D_IN) for z, s in zip(z2, shapes)], [o[N_BRANCH] for o in outs], kmean_next)


def kernel(x_prompt, x_sample, state_hgrn, cache_k, cache_v, cache_mem_k, cache_mem_v, page_table,
           mem_prompt, lb_logits, norm_mix, w_in, norm_o_a, norm_q_b, norm_k_b, norm_q_c, norm_k_c,
           norm_mem, w_mem_kv, w_br_a, w_br_b, w_br_c, w_out, norm_ffn, w_gate, w_up, w_down):
    depth = w_in.shape[0]
    n_prompt, seq, _ = x_prompt.shape
    n_seq, n_tok, _ = x_sample.shape
    n_pages = page_table.shape[1]
    assert seq % MOBA_BLOCK == 0 and n_pages % PAGES_PER_BLOCK == 0
    assert n_pages // PAGES_PER_BLOCK >= MOBA_TOPK and n_tok <= SAMPLE_ROWS <= MOBA_BLOCK

    row = lambda g: g.reshape(depth, 1, g.shape[-1])
    w = {"norm_mix": row(norm_mix), "norm_ffn": row(norm_ffn), "w_in": w_in, "w_br_a": w_br_a, "w_br_b": w_br_b,
         "w_br_c": w_br_c, "w_out": w_out, "w_gate": w_gate, "w_up": w_up, "w_down": w_down}
    g_oa, g_qb, g_kb, g_qc, g_kc, g_mem = (row(g) for g in (norm_o_a, norm_q_b, norm_k_b, norm_q_c,
                                                           norm_k_c, norm_mem))
    lower = _lower_bounds(lb_logits).reshape(depth, 1, D_A)
    slopes = 2.0 ** (-8.0 * jnp.arange(1, H_B + 1, dtype=F32) / H_B)

    pt_flat = page_table.reshape(-1)
    s0_prompt = jnp.zeros((1, n_prompt, H_A, DK_A, DV_A), F32)
    mem2 = mem_prompt.reshape(n_prompt * N_MEM, D_MODEL)

    xp = x_prompt
    xs = jnp.pad(x_sample, ((0, 0), (0, SAMPLE_ROWS - n_tok), (0, 0)))
    outs = {k: [] for k in ("hp", "mkp", "mvp", "hs", "ks", "vs")}
    kv_prompt = tuple(jnp.zeros((depth, n_prompt, seq, H_B, DH_B), F32) for _ in range(2))

    kmean_layer = _page_means(cache_k, pt_flat, 0, n_seq, n_pages)
    for layer in range(depth):
        (kv,) = _mm_groups([[_rmsnorm(mem2, g_mem, layer)]], [w_mem_kv], layer, epilogue=_ep_plain,
                           out_dtype=F32, tm=1024, tn=1024, name="mm_mem")
        kv = kv.reshape(n_prompt, N_MEM, 2 * D_C)
        mem_k = _head_norm(kv, g_kc, layer, col_block=0, heads=H_C, hd=DH_C)
        mem_v = kv[:, :, D_C:]

        def prompt_branches(z):
            o_a, s_new = _hgrn2(z, lower, g_oa, s0_prompt, layer, 0, chunk=64)
            kb, vt, kmean, kv_stacked = _moba_kv(z, g_kb, layer, depth, kv_prompt)
            o_b = _moba_prompt(z, g_qb, kb, vt, kmean, slopes, layer)
            o_c = _mem_attn(z, g_qc, mem_k, kv, layer, tq=2048)
            m = n_prompt * seq
            return o_a.reshape(m, D_A), o_b.reshape(m, D_B), o_c.reshape(m, D_C), (s_new, kv_stacked)

        def sample_branches(z):
            o_a, s_new = _hgrn2(z, lower, g_oa, state_hgrn, layer, layer, chunk=SAMPLE_ROWS, valid_len=n_tok)
            kn = _head_norm(z, g_kb, layer, col_block=OFF_KB // D_B, heads=H_B, hd=DH_B)
            kmean = kmean_layer.reshape(n_seq, -1, D_B)
            qn, picks = _moba_select(z, g_qb, kmean, layer)
            picks_flat = picks[:, :, :n_tok, :MOBA_TOPK].reshape(-1)
            o_b = _moba_sample(qn, kn, z, cache_k, cache_v, picks_flat, pt_flat, slopes, layer,
                               n_tok=n_tok, n_pages=n_pages)
            o_c = _mem_attn_cached(z, g_qc, cache_mem_k, cache_mem_v, layer)
            m = n_seq * SAMPLE_ROWS
            return o_a.reshape(m, D_A), o_b.reshape(m, D_B), o_c.reshape(m, D_C), (s_new, kn)

        next_means = (cache_k, pt_flat, layer + 1, n_seq, n_pages) if layer + 1 < depth else None
        (xp, xs), (_, zs), ((s_p, kv_prompt), (s_s, k_s)), kmean_layer = _mixer_and_ffn(
            [xp, xs], layer, w, [prompt_branches, sample_branches], next_means)

        outs["hp"].append(s_p)
        outs["mkp"].append(mem_k.reshape(n_prompt, N_MEM, H_C, DH_C))
        outs["mvp"].append(mem_v.reshape(n_prompt, N_MEM, H_C, DH_C))
        outs["hs"].append(s_s)
        outs["ks"].append(k_s[:, :n_tok].reshape(n_seq, n_tok, H_B, DH_B))
        outs["vs"].append(zs[:, :n_tok, OFF_VB:OFF_VB + D_B].reshape(n_seq, n_tok, H_B, DH_B))

    stack = lambda k: jnp.stack(outs[k])
    return (xp, xs[:, :n_tok], stack("hp"), kv_prompt[0], kv_prompt[1], stack("mkp"), stack("mvp"),
            stack("hs"), stack("ks"), stack("vs"))
```

```python
import functools

import jax
import jax.numpy as jnp
from jax import lax
from jax.experimental import pallas as pl
from jax.experimental.pallas import tpu as pltpu

D_MODEL = 2048
H_A, DK_A, DV_A = 8, 128, 128
D_A = H_A * DK_A
H_B, DH_B = 8, 128
D_B = H_B * DH_B
MOBA_BLOCK = 256
MOBA_TOPK = 3
PAGE_SIZE = 128
PAGES_PER_BLOCK = MOBA_BLOCK // PAGE_SIZE
N_MEM = 256
H_C, DH_C = 4, 256
D_C = H_C * DH_C
N_BRANCH = 3
D_IN = 4 * D_A + 3 * D_B + D_C + N_BRANCH * D_MODEL
EPS = 1e-6

OFF_QA, OFF_FA, OFF_IA, OFF_GA = 0, D_A, 2 * D_A, 3 * D_A
OFF_QB = 4 * D_A
OFF_KB = OFF_QB + D_B
OFF_VB = OFF_KB + D_B
OFF_QC = OFF_VB + D_B
OFF_GATE = OFF_QC + D_C

GLA_SUB = 16
GLA_SAFE_DROP = 80.0
SAMPLE_ROWS = 16
MOBA_HEADS_PER_STEP = 4
V7X_VMEM_LIMIT = 56 * 1024 * 1024

F32 = jnp.float32
BF16 = jnp.bfloat16
_NT = (((1,), (1,)), ((), ()))
_TN = (((0,), (0,)), ((), ()))


def _params(*semantics):
    return pltpu.CompilerParams(dimension_semantics=semantics, vmem_limit_bytes=V7X_VMEM_LIMIT)


def _tile(n, pref):
    t = min(n, pref)
    assert n % t == 0, (n, pref)
    return t


def _rms(x, gain):
    return x * lax.rsqrt(jnp.mean(x * x, axis=-1, keepdims=True) + EPS) * gain


def _rmsnorm_kernel(x_ref, g_ref, o_ref):
    o_ref[...] = _rms(x_ref[...], g_ref[...]).astype(o_ref.dtype)


def _rmsnorm(x, gains, layer):
    m, d = x.shape
    tm = _tile(m, 512)
    return pl.pallas_call(
        _rmsnorm_kernel,
        out_shape=jax.ShapeDtypeStruct((m, d), BF16),
        grid=(m // tm,),
        in_specs=[pl.BlockSpec((tm, d), lambda i: (i, 0)),
                  pl.BlockSpec((None, 1, d), lambda i: (layer, 0, 0))],
        out_specs=pl.BlockSpec((tm, d), lambda i: (i, 0)),
        compiler_params=_params("parallel"),
        name="rmsnorm",
    )(x, gains)


def _dot(a, b):
    return jnp.dot(a, b, preferred_element_type=F32)


def _mm_groups_kernel(*refs, n_groups, n_w, act_of_w, n_extra, epilogue, side=None):
    n_act = len(set(act_of_w))
    pos = 0
    acts = [refs[pos + g * n_act:pos + (g + 1) * n_act] for g in range(n_groups)]
    pos += n_groups * n_act
    weights = refs[pos:pos + n_w]
    pos += n_w
    extras = [refs[pos + g * n_extra:pos + (g + 1) * n_extra] for g in range(n_groups)]
    pos += n_groups * n_extra
    outs = refs[pos:pos + n_groups]
    wb = refs[pos + n_groups:]

    def compute(g):
        dots = [_dot(acts[g][act_of_w[k]][...], wb[k][...]) for k in range(n_w)]
        outs[g][...] = epilogue(dots, [e[...] for e in extras[g]]).astype(outs[g].dtype)

    if side is not None:
        side[0]()

    @pl.when(pl.program_id(1) == 0)
    def _():
        for k in range(n_w):
            wb[k][...] = weights[k][...].astype(BF16)
        for g in range(1, n_groups):
            compute(g)

    compute(0)
    if side is not None:
        side[1]()


def _page_mean_rider(pt_ref, cache_ref, km_ref, buf, sem, *, layer, n_seq, n_pages, per_step):
    blocks_per_seq = n_pages // PAGES_PER_BLOCK
    last_block = n_seq * blocks_per_seq - 1
    step = pl.program_id(0) * pl.num_programs(1) + pl.program_id(1)
    n_steps = pl.num_programs(0) * pl.num_programs(1)
    slot = step % 2

    def block_of(st, r):
        g = jnp.minimum(st * per_step + r, last_block)
        return g // blocks_per_seq, g % blocks_per_seq

    def page_copies(st, into):
        copies = []
        for r in range(per_step):
            seq, blk = block_of(st, r)
            for p in range(PAGES_PER_BLOCK):
                page = pt_ref[seq * n_pages + blk * PAGES_PER_BLOCK + p]
                copies.append(pltpu.make_async_copy(cache_ref.at[layer, page],
                                                    buf.at[into, r * PAGES_PER_BLOCK + p], sem.at[into]))
        return copies

    def before():
        @pl.when(step == 0)
        def _():
            for c in page_copies(step, slot):
                c.start()

        @pl.when(step + 1 < n_steps)
        def _():
            for c in page_copies(step + 1, 1 - slot):
                c.start()

        for c in page_copies(step, slot):
            c.wait()

    def beside():
        for r in range(per_step):
            seq, blk = block_of(step, r)
            total = jnp.sum(buf[slot, r * PAGES_PER_BLOCK], axis=0)
            for p in range(1, PAGES_PER_BLOCK):
                total = total + jnp.sum(buf[slot, r * PAGES_PER_BLOCK + p], axis=0)
            km_ref[seq, blk] = total * (1.0 / MOBA_BLOCK)

    return before, beside


def _mm_groups_rider_kernel(pt_ref, *refs, n_in, n_out, n_w, mm_kwargs, rider_kwargs):
    ins, cache_ref = refs[:n_in], refs[n_in]
    outs, km_ref = refs[n_in + 1:n_in + 1 + n_out], refs[n_in + 1 + n_out]
    wb = refs[n_in + 2 + n_out:n_in + 2 + n_out + n_w]
    buf, sem = refs[n_in + 2 + n_out + n_w:]
    side = _page_mean_rider(pt_ref, cache_ref, km_ref, buf, sem, **rider_kwargs)
    _mm_groups_kernel(*ins, *outs, *wb, side=side, **mm_kwargs)


def _mm_groups(acts, weights, layer, *, epilogue, out_dtype, tm, tn, act_of_w=None, extras=(), name,
               page_means=None):
    n_groups, n_w = len(acts), len(weights)
    act_of_w = tuple(range(n_w)) if act_of_w is None else tuple(act_of_w)
    k, n = weights[0].shape[1:]
    m = [a[0].shape[0] for a in acts]
    tm, tn = _tile(m[0], tm), _tile(n, tn)
    rows = [tm] + m[1:]
    grid = (n // tn, m[0] // tm)

    def row_block(g):
        return (lambda j, i, *_: (i, 0)) if g == 0 else (lambda j, i, *_: (0, 0))

    def out_block(g, first=0):
        return (lambda j, i, *_: (i, first + j)) if g == 0 else (lambda j, i, *_: (0, first + j))

    in_specs, args = [], []
    for g in range(n_groups):
        in_specs += [pl.BlockSpec((rows[g], a.shape[1]), row_block(g)) for a in acts[g]]
        args += list(acts[g])
    in_specs += [pl.BlockSpec((None, k, tn), lambda j, i, *_: (layer, 0, j))] * n_w
    args += list(weights)
    for g in range(n_groups):
        for arrays, col0 in extras:
            in_specs.append(pl.BlockSpec((rows[g], tn), out_block(g, col0 // tn)))
            args.append(arrays[g])
    out_shape = [jax.ShapeDtypeStruct((m[g], n), out_dtype) for g in range(n_groups)]
    out_specs = [pl.BlockSpec((rows[g], tn), out_block(g)) for g in range(n_groups)]
    scratch = [pltpu.VMEM((k, tn), BF16)] * n_w
    mm_kwargs = dict(n_groups=n_groups, n_w=n_w, act_of_w=act_of_w, n_extra=len(extras), epilogue=epilogue)
    if page_means is None:
        return pl.pallas_call(
            functools.partial(_mm_groups_kernel, **mm_kwargs),
            out_shape=tuple(out_shape), grid=grid, in_specs=in_specs, out_specs=tuple(out_specs),
            scratch_shapes=scratch, compiler_params=_params("arbitrary", "arbitrary"), name=name,
        )(*args)

    cache_k, pt_flat, cache_layer, n_seq, n_pages = page_means
    n_blocks = n_pages // PAGES_PER_BLOCK
    per_step = -(-n_seq * n_blocks // (grid[0] * grid[1]))
    km_shape = (n_seq, n_blocks, H_B, DH_B)
    return pl.pallas_call(
        functools.partial(_mm_groups_rider_kernel, n_in=len(args), n_out=n_groups, n_w=n_w, mm_kwargs=mm_kwargs,
                          rider_kwargs=dict(layer=cache_layer, n_seq=n_seq, n_pages=n_pages, per_step=per_step)),
        out_shape=tuple(out_shape) + (jax.ShapeDtypeStruct(km_shape, F32),),
        grid_spec=pltpu.PrefetchScalarGridSpec(
            num_scalar_prefetch=1, grid=grid,
            in_specs=in_specs + [pl.BlockSpec(memory_space=pl.ANY)],
            out_specs=tuple(out_specs) + (pl.BlockSpec(km_shape, lambda j, i, *_: (0, 0, 0, 0)),),
            scratch_shapes=scratch + [pltpu.VMEM((2, per_step * PAGES_PER_BLOCK, PAGE_SIZE, H_B, DH_B), F32),
                                      pltpu.SemaphoreType.DMA((2,))]),
        compiler_params=_params("arbitrary", "arbitrary"), name=name,
    )(pt_flat, *args, cache_k)


def _ep_plain(dots, extras):
    return dots[0]


def _ep_residual(dots, extras):
    return extras[0] + dots[0]


def _ep_swiglu(dots, extras):
    return dots[0] * jax.nn.sigmoid(dots[0]) * dots[1]


def _ep_gated_sum(dots, extras):
    return functools.reduce(jnp.add, [jax.nn.sigmoid(g) * d for g, d in zip(extras, dots)])


def _lower_bounds_kernel(x_ref, o_ref):
    x = x_ref[...]
    e = jnp.exp(x - jnp.max(x, axis=0, keepdims=True))
    p = e / jnp.sum(e, axis=0, keepdims=True)
    rows = [jnp.zeros_like(p[0:1])]
    for r in range(1, x.shape[0]):
        rows.append(rows[-1] + p[r:r + 1])
    o_ref[...] = jnp.concatenate(rows, axis=0)


def _lower_bounds(lb_logits):
    return pl.pallas_call(
        _lower_bounds_kernel,
        out_shape=jax.ShapeDtypeStruct(lb_logits.shape, F32),
        name="hgrn2_lower_bounds",
    )(lb_logits)


def _hgrn2_pair_weights_factored(q, kk, G, g_mid):
    qt = (q * jnp.exp(G - g_mid)).astype(BF16)
    kt = (kk * jnp.exp(g_mid - G)).astype(BF16)
    a = lax.dot_general(qt, kt, _NT, preferred_element_type=F32)
    causal = lax.broadcasted_iota(jnp.int32, a.shape, 1) <= lax.broadcasted_iota(jnp.int32, a.shape, 0)
    return jnp.where(causal, a, 0.0).astype(BF16)


def _hgrn2_intra_exact(q, kk, v, G, rmod, chunk):
    pieces = [jnp.zeros((GLA_SUB, DV_A), F32)]
    for i in range(1, chunk // GLA_SUB):
        lo = i * GLA_SUB
        r = G[lo - 1:lo]
        qi = (q[lo:lo + GLA_SUB] * jnp.exp(G[lo:lo + GLA_SUB] - r)).astype(BF16)
        kj = (kk[:lo] * jnp.exp(r - G[:lo])).astype(BF16)
        a = lax.dot_general(qi, kj, _NT, preferred_element_type=F32)
        pieces.append(_dot(a.astype(BF16), v[:lo].astype(BF16)))
    o = jnp.concatenate(pieces, axis=0)
    for d in range(GLA_SUB):
        k_d = kk if d == 0 else pltpu.roll(kk, d, axis=0)
        g_d = G if d == 0 else pltpu.roll(G, d, axis=0)
        v_d = v if d == 0 else pltpu.roll(v, d, axis=0)
        decay = jnp.exp(jnp.where(rmod >= d, G - g_d, -jnp.inf))
        a_d = jnp.sum(q * k_d * decay, axis=-1, keepdims=True)
        o = o + a_d * v_d
    return o


def _hgrn2_kernel(q_ref, f_ref, i_ref, g_ref, lb_ref, gn_ref, s0_ref, o_ref, sout_ref,
                  st_ref, k_ref, qs_ref, G_ref, oi_ref, *, chunk, valid_len, side=None):
    c = pl.program_id(1)
    if side is not None:
        side[0]()

    @pl.when(c == 0)
    def _():
        for h in range(H_A):
            st_ref[h] = s0_ref[h].T

    heads = [slice(h * DK_A, (h + 1) * DK_A) for h in range(H_A)]
    row = lax.broadcasted_iota(jnp.int32, (chunk, 1), 0)
    rmod = row % GLA_SUB
    tri = (lax.broadcasted_iota(jnp.int32, (chunk, chunk), 0)
           >= lax.broadcasted_iota(jnp.int32, (chunk, chunk), 1)).astype(F32)

    for sl in heads:
        zf = f_ref[:, sl]
        lb = lb_ref[:, sl]
        e = jnp.exp(-jnp.abs(zf))
        log_sig = jnp.minimum(zf, 0.0) - jnp.log1p(e)
        la = jnp.log(lb)
        lc = jnp.log1p(-lb) + log_sig
        log_f = jnp.maximum(la, lc) + jnp.log1p(jnp.exp(-jnp.abs(la - lc)))
        kk = (1.0 - lb) * (jnp.where(zf >= 0.0, e, 1.0) / (1.0 + e))
        if valid_len is not None:
            live = (c * chunk + row) < valid_len
            log_f = jnp.where(live, log_f, 0.0)
            kk = jnp.where(live, kk, 0.0)
        qr = q_ref[:, sl]
        k_ref[:, sl] = kk
        qs_ref[:, sl] = qr * jax.nn.sigmoid(qr) * (DK_A ** -0.5)
        G_ref[:, sl] = jnp.dot(tri, log_f, precision=lax.Precision.HIGHEST, preferred_element_type=F32)

    mid = chunk // 2
    g_mid = G_ref[mid - 1:mid, :]
    g_last = G_ref[chunk - 1:chunk, :]
    safe = jnp.max(jnp.maximum(-g_mid, g_mid - g_last)) < GLA_SAFE_DROP

    @pl.when(safe)
    def _():
        weights = [_hgrn2_pair_weights_factored(qs_ref[:, sl], k_ref[:, sl], G_ref[:, sl], g_mid[:, sl])
                   for sl in heads]
        for sl, a in zip(heads, weights):
            oi_ref[:, sl] = _dot(a, i_ref[:, sl].astype(BF16))

    @pl.when(jnp.logical_not(safe))
    def _():
        for sl in heads:
            oi_ref[:, sl] = _hgrn2_intra_exact(qs_ref[:, sl], k_ref[:, sl], i_ref[:, sl], G_ref[:, sl], rmod, chunk)

    carried = [lax.dot_general((qs_ref[:, sl] * jnp.exp(G_ref[:, sl])).astype(BF16), st_ref[h].astype(BF16),
                               _NT, preferred_element_type=F32) for h, sl in enumerate(heads)]
    added = [lax.dot_general(i_ref[:, sl].astype(BF16),
                             (k_ref[:, sl] * jnp.exp(g_last[:, sl] - G_ref[:, sl])).astype(BF16),
                             _TN, preferred_element_type=F32) for sl in heads]
    gn = gn_ref[...]
    for h, sl in enumerate(heads):
        st_ref[h] = st_ref[h] * jnp.exp(g_last[:, sl]) + added[h]
        gr = g_ref[:, sl]
        o = oi_ref[:, sl] + carried[h]
        o_ref[:, sl] = (_rms(o, gn) * (gr * jax.nn.sigmoid(gr))).astype(o_ref.dtype)
    if side is not None:
        side[1]()

    @pl.when(c == pl.num_programs(1) - 1)
    def _():
        for h in range(H_A):
            sout_ref[h] = st_ref[h].T


def _hgrn2_rider_kernel(pt_ref, *refs, n_in, chunk, valid_len, rider_kwargs):
    ins, cache_ref = refs[:n_in], refs[n_in]
    o_ref, sout_ref, km_ref = refs[n_in + 1:n_in + 4]
    scratch, (buf, sem) = refs[n_in + 4:-2], refs[-2:]
    side = _page_mean_rider(pt_ref, cache_ref, km_ref, buf, sem, **rider_kwargs)
    _hgrn2_kernel(*ins, o_ref, sout_ref, *scratch, chunk=chunk, valid_len=valid_len, side=side)


def _hgrn2(z, lower, g_onorm, s0, layer, s0_layer, *, chunk, valid_len=None, page_means=None):
    b, l, _ = z.shape
    chunk = _tile(l, chunk)
    grid = (b, l // chunk)

    def col_spec(off):
        return pl.BlockSpec((None, chunk, D_A), lambda bi, c, *_: (bi, c, off // D_A))

    in_specs = [col_spec(OFF_QA), col_spec(OFF_FA), col_spec(OFF_IA), col_spec(OFF_GA),
                pl.BlockSpec((None, 1, D_A), lambda bi, c, *_: (layer, 0, 0)),
                pl.BlockSpec((None, 1, DV_A), lambda bi, c, *_: (layer, 0, 0)),
                pl.BlockSpec((None, None, H_A, DK_A, DV_A), lambda bi, c, *_: (s0_layer, bi, 0, 0, 0))]
    out_shape = (jax.ShapeDtypeStruct((b, l, D_A), BF16), jax.ShapeDtypeStruct((b, H_A, DK_A, DV_A), F32))
    out_specs = (pl.BlockSpec((None, chunk, D_A), lambda bi, c, *_: (bi, c, 0)),
                 pl.BlockSpec((None, H_A, DK_A, DV_A), lambda bi, c, *_: (bi, 0, 0, 0)))
    scratch = [pltpu.VMEM((H_A, DV_A, DK_A), F32)] + [pltpu.VMEM((chunk, D_A), F32)] * 4
    args = (z, z, z, z, lower, g_onorm, s0)
    if page_means is None:
        return pl.pallas_call(
            functools.partial(_hgrn2_kernel, chunk=chunk, valid_len=valid_len),
            out_shape=out_shape, grid=grid, in_specs=in_specs, out_specs=out_specs, scratch_shapes=scratch,
            compiler_params=_params("parallel", "arbitrary"), name="hgrn2",
        )(*args)

    cache_k, pt_flat, cache_layer, n_seq, n_pages = page_means
    n_blocks = n_pages // PAGES_PER_BLOCK
    per_step = -(-n_seq * n_blocks // (grid[0] * grid[1]))
    km_shape = (n_seq, n_blocks, H_B, DH_B)
    return pl.pallas_call(
        functools.partial(_hgrn2_rider_kernel, n_in=len(args), chunk=chunk, valid_len=valid_len,
                          rider_kwargs=dict(layer=cache_layer, n_seq=n_seq, n_pages=n_pages, per_step=per_step)),
        out_shape=out_shape + (jax.ShapeDtypeStruct(km_shape, F32),),
        grid_spec=pltpu.PrefetchScalarGridSpec(
            num_scalar_prefetch=1, grid=grid,
            in_specs=in_specs + [pl.BlockSpec(memory_space=pl.ANY)],
            out_specs=out_specs + (pl.BlockSpec(km_shape, lambda bi, c, *_: (0, 0, 0, 0)),),
            scratch_shapes=scratch + [pltpu.VMEM((2, per_step * PAGES_PER_BLOCK, PAGE_SIZE, H_B, DH_B), F32),
                                      pltpu.SemaphoreType.DMA((2,))]),
        compiler_params=_params("arbitrary", "arbitrary"), name="hgrn2",
    )(pt_flat, *args, cache_k)


def _head_norm_kernel(x_ref, g_ref, o_ref, *, heads, hd):
    g = g_ref[...]
    for h in range(heads):
        sl = slice(h * hd, (h + 1) * hd)
        o_ref[:, sl] = _rms(x_ref[:, sl], g)


def _head_norm(x, gains, layer, *, col_block, heads, hd):
    b, r, _ = x.shape
    width = heads * hd
    return pl.pallas_call(
        functools.partial(_head_norm_kernel, heads=heads, hd=hd),
        out_shape=jax.ShapeDtypeStruct((b, r, width), F32),
        grid=(b,),
        in_specs=[pl.BlockSpec((None, r, width), lambda bi: (bi, 0, col_block)),
                  pl.BlockSpec((None, 1, hd), lambda bi: (layer, 0, 0))],
        out_specs=pl.BlockSpec((None, r, width), lambda bi: (bi, 0, 0)),
        compiler_params=_params("parallel"),
        name="head_norm",
    )(x, gains)


def _moba_kv_kernel(k_ref, v_ref, g_ref, *refs, layer, n_carried):
    kb_ref, vt_ref, km_ref, knew_ref, vnew_ref, ks_ref, vs_ref, sem = refs[n_carried:]
    n_blk = pl.num_programs(1)
    step = pl.program_id(0) * n_blk + pl.program_id(1)
    n_steps = pl.num_programs(0) * n_blk
    slot = step % 2

    def out_copies(st, half):
        bi, n = st // n_blk, st % n_blk
        rows = pl.ds(pl.multiple_of(n * MOBA_BLOCK, MOBA_BLOCK), MOBA_BLOCK)
        copies = []
        for h in range(H_B):
            cols = slice(h * DH_B, (h + 1) * DH_B)
            copies.append(pltpu.make_async_copy(ks_ref.at[half, :, cols], knew_ref.at[layer, bi, rows, h, :], sem.at[half]))
            copies.append(pltpu.make_async_copy(vs_ref.at[half, :, cols], vnew_ref.at[layer, bi, rows, h, :], sem.at[half]))
        return copies

    @pl.when(step >= 2)
    def _():
        for c in out_copies(step - 2, slot):
            c.wait()

    g = g_ref[...]
    for h in range(H_B):
        sl = slice(h * DH_B, (h + 1) * DH_B)
        kn = _rms(k_ref[:, sl], g)
        ks_ref[slot, :, sl] = kn
        kb_ref[:, sl] = kn.astype(BF16)
        km_ref[:, sl] = jnp.mean(kn, axis=0, keepdims=True)
        vt_ref[sl, :] = v_ref[:, sl].T.astype(BF16)
    vs_ref[slot] = v_ref[...]
    for c in out_copies(step, slot):
        c.start()

    @pl.when((step == n_steps - 1) & (step >= 1))
    def _():
        for c in out_copies(step - 1, 1 - slot):
            c.wait()

    @pl.when(step == n_steps - 1)
    def _():
        for c in out_copies(step, slot):
            c.wait()


def _moba_kv(z, g_kb, layer, depth, stacked):
    b, l, _ = z.shape
    nblk = l // MOBA_BLOCK
    heads_out = jax.ShapeDtypeStruct((depth, b, l, H_B, DH_B), F32)
    hbm = pl.BlockSpec(memory_space=pl.ANY)
    carried = tuple(stacked)
    kb, vt, km, k_new, v_new = pl.pallas_call(
        functools.partial(_moba_kv_kernel, layer=layer, n_carried=len(carried)),
        out_shape=(jax.ShapeDtypeStruct((b, l, D_B), BF16), jax.ShapeDtypeStruct((b, D_B, l), BF16),
                   jax.ShapeDtypeStruct((b, nblk, 1, D_B), F32), heads_out, heads_out),
        grid=(b, nblk),
        in_specs=[pl.BlockSpec((None, MOBA_BLOCK, D_B), lambda bi, n: (bi, n, OFF_KB // D_B)),
                  pl.BlockSpec((None, MOBA_BLOCK, D_B), lambda bi, n: (bi, n, OFF_VB // D_B)),
                  pl.BlockSpec((None, 1, DH_B), lambda bi, n: (layer, 0, 0))] + [hbm] * len(carried),
        out_specs=(pl.BlockSpec((None, MOBA_BLOCK, D_B), lambda bi, n: (bi, n, 0)),
                   pl.BlockSpec((None, D_B, MOBA_BLOCK), lambda bi, n: (bi, 0, n)),
                   pl.BlockSpec((None, None, 1, D_B), lambda bi, n: (bi, n, 0, 0)), hbm, hbm),
        scratch_shapes=[pltpu.VMEM((2, MOBA_BLOCK, D_B), F32), pltpu.VMEM((2, MOBA_BLOCK, D_B), F32),
                        pltpu.SemaphoreType.DMA((2,))],
        input_output_aliases={3 + c: 3 + c for c in range(len(carried))},
        compiler_params=_params("arbitrary", "arbitrary"),
        name="moba_kv",
    )(z, z, g_kb, *carried)
    return kb, vt, km.reshape(b, nblk, D_B), (k_new, v_new)


def _topk_mask_t(gate_t, n_valid, topk):
    n, r = gate_t.shape
    rowi = lax.broadcasted_iota(jnp.int32, (n, r), 0)
    gm = jnp.where(rowi < n_valid, gate_t, -jnp.inf)
    sel = jnp.zeros((n, r), F32)
    for j in range(n):
        gj = gm[j:j + 1]
        beats = jnp.where(gm > gj, 1.0, jnp.where((gm == gj) & (rowi < j), 1.0, 0.0))
        rank = jnp.sum(beats, axis=0, keepdims=True)
        sel = jnp.where((rowi == j) & (rank < topk) & (rowi < n_valid), 1.0, sel)
    return sel


def _moba_prompt_kernel(slopes_ref, q_ref, gq_ref, k_ref, vt_ref, km_ref, o_ref, s_ref, qb_ref, sel_ref):
    hg = pl.program_id(1)
    i = pl.program_id(2)
    blk = MOBA_BLOCK
    scale = DH_B ** -0.5
    heads = [slice(hh * DH_B, (hh + 1) * DH_B) for hh in range(MOBA_HEADS_PER_STEP)]

    @pl.when(i == 0)
    def _():
        for hh, sl in enumerate(heads):
            qn = _rms(q_ref[:, sl], gq_ref[...])
            qb_ref[:, sl] = qn.astype(BF16)
            gate_t = lax.dot_general(km_ref[:, sl], qn, _NT, precision=lax.Precision.HIGHEST,
                                     preferred_element_type=F32)
            q_block = lax.broadcasted_iota(jnp.int32, gate_t.shape, 1) // blk
            sel_ref[hh] = _topk_mask_t(gate_t, q_block, MOBA_TOPK)

    d0 = (lax.broadcasted_iota(jnp.int32, (blk, blk), 1)
          - lax.broadcasted_iota(jnp.int32, (blk, blk), 0)).astype(F32)
    slopes = [slopes_ref[hg * MOBA_HEADS_PER_STEP + hh] for hh in range(MOBA_HEADS_PER_STEP)]
    bias0 = [-slope * d0 for slope in slopes]

    def attend(own):
        cols = slice(own * blk, (own + 1) * blk)
        m = [None] * len(heads)
        for j in range(own + 1):
            rows = slice(j * blk, (j + 1) * blk)
            for hh, sl in enumerate(heads):
                s = lax.dot_general(k_ref[rows, sl], qb_ref[cols, sl], _NT, preferred_element_type=F32) * scale
                s = s + (bias0[hh] - slopes[hh] * float((own - j) * blk))
                keep = (d0 >= 0.0) if j == own else (sel_ref[hh, j:j + 1, cols] > 0.0)
                s = jnp.where(keep, s, -jnp.inf)
                s_ref[hh, rows, :] = s
                m_j = jnp.max(s, axis=0, keepdims=True)
                m[hh] = m_j if m[hh] is None else jnp.maximum(m[hh], m_j)
        l = [jnp.zeros((1, blk), F32) for _ in heads]
        acc = [jnp.zeros((DH_B, blk), F32) for _ in heads]
        for j in range(own + 1):
            rows = slice(j * blk, (j + 1) * blk)
            for hh, sl in enumerate(heads):
                p = jnp.exp(s_ref[hh, rows, :] - m[hh])
                l[hh] = l[hh] + jnp.sum(p, axis=0, keepdims=True)
                acc[hh] = acc[hh] + _dot(vt_ref[sl, rows], p.astype(BF16))
        for hh, sl in enumerate(heads):
            o_ref[:, sl] = (acc[hh] / l[hh]).T.astype(o_ref.dtype)

    for own in range(km_ref.shape[0]):
        pl.when(i == own)(functools.partial(attend, own))


def _moba_prompt(z, g_qb, kb, vt, kmean, slopes, layer):
    b, l, _ = z.shape
    nblk = l // MOBA_BLOCK
    width = MOBA_HEADS_PER_STEP * DH_B
    return pl.pallas_call(
        _moba_prompt_kernel,
        out_shape=jax.ShapeDtypeStruct((b, l, D_B), BF16),
        grid=(b, H_B // MOBA_HEADS_PER_STEP, nblk),
        in_specs=[pl.BlockSpec(memory_space=pltpu.SMEM),
                  pl.BlockSpec((None, l, width), lambda bi, h, i: (bi, 0, OFF_QB // width + h)),
                  pl.BlockSpec((None, 1, DH_B), lambda bi, h, i: (layer, 0, 0)),
                  pl.BlockSpec((None, l, width), lambda bi, h, i: (bi, 0, h)),
                  pl.BlockSpec((None, width, l), lambda bi, h, i: (bi, h, 0)),
                  pl.BlockSpec((None, nblk, width), lambda bi, h, i: (bi, 0, h))],
        out_specs=pl.BlockSpec((None, MOBA_BLOCK, width), lambda bi, h, i: (bi, i, h)),
        scratch_shapes=[pltpu.VMEM((MOBA_HEADS_PER_STEP, l, MOBA_BLOCK), F32), pltpu.VMEM((l, width), BF16),
                        pltpu.VMEM((MOBA_HEADS_PER_STEP, nblk, l), F32)],
        compiler_params=_params("parallel", "parallel", "arbitrary"),
        name="moba_prompt",
    )(slopes, z, g_qb, kb, vt, kmean)


def _moba_select_kernel(q_ref, gq_ref, km_ref, qn_ref, sel_ref):
    rows = q_ref.shape[0]
    n_blocks = km_ref.shape[0]
    g = gq_ref[...]
    col = lax.broadcasted_iota(jnp.int32, (rows, n_blocks), 1)
    lane = lax.broadcasted_iota(jnp.int32, (rows, sel_ref.shape[-1]), 1)
    for h in range(H_B):
        sl = slice(h * DH_B, (h + 1) * DH_B)
        qn = _rms(q_ref[:, sl], g)
        qn_ref[:, sl] = qn
        gate = lax.dot_general(qn, km_ref[:, sl], _NT, precision=lax.Precision.HIGHEST,
                               preferred_element_type=F32)
        picks = jnp.zeros(lane.shape, jnp.int32)
        for r in range(MOBA_TOPK):
            best = jnp.max(gate, axis=-1, keepdims=True)
            idx = jnp.min(jnp.where(gate == best, col, n_blocks), axis=-1, keepdims=True)
            picks = jnp.where(lane == r, idx, picks)
            gate = jnp.where(col == idx, -jnp.inf, gate)
        sel_ref[h] = picks


def _moba_select(z, g_qb, kmean, layer):
    b, rows, _ = z.shape
    n_blocks = kmean.shape[1]
    return pl.pallas_call(
        _moba_select_kernel,
        out_shape=(jax.ShapeDtypeStruct((b, rows, D_B), F32),
                   jax.ShapeDtypeStruct((b, H_B, rows, 128), jnp.int32)),
        grid=(b,),
        in_specs=[pl.BlockSpec((None, rows, D_B), lambda bi: (bi, 0, OFF_QB // D_B)),
                  pl.BlockSpec((None, 1, DH_B), lambda bi: (layer, 0, 0)),
                  pl.BlockSpec((None, n_blocks, D_B), lambda bi: (bi, 0, 0))],
        out_specs=(pl.BlockSpec((None, rows, D_B), lambda bi: (bi, 0, 0)),
                   pl.BlockSpec((None, H_B, rows, 128), lambda bi: (bi, 0, 0, 0))),
        compiler_params=_params("parallel"),
        name="moba_select",
    )(z, g_qb, kmean)


def _moba_sample_kernel(sel_ref, pt_ref, slopes_ref, q_ref, kown_ref, vown_ref, ck_ref, cv_ref, o_ref,
                        kbuf, vbuf, sem, *, layer, n_tok, n_pages):
    step = pl.program_id(0)
    n_steps = pl.num_programs(0)
    tiles_per_tok = MOBA_TOPK * PAGES_PER_BLOCK
    past_len = n_pages * PAGE_SIZE

    def picked_block(st, t, s):
        return sel_ref[(st * n_tok + t) * MOBA_TOPK + s]

    def tile_copies(st, slot):
        bi, h = st // H_B, st % H_B
        copies = []
        for t in range(n_tok):
            for s in range(MOBA_TOPK):
                block = picked_block(st, t, s)
                for p in range(PAGES_PER_BLOCK):
                    page = pt_ref[bi * n_pages + block * PAGES_PER_BLOCK + p]
                    j = t * tiles_per_tok + s * PAGES_PER_BLOCK + p
                    copies.append(pltpu.make_async_copy(
                        ck_ref.at[layer, page, :, h, :], kbuf.at[slot, j], sem.at[slot]))
                    copies.append(pltpu.make_async_copy(
                        cv_ref.at[layer, page, :, h, :], vbuf.at[slot, j], sem.at[slot]))
        return copies

    slot = step % 2

    @pl.when(step == 0)
    def _():
        for c in tile_copies(step, slot):
            c.start()

    @pl.when(step + 1 < n_steps)
    def _():
        for c in tile_copies(step + 1, 1 - slot):
            c.start()

    for c in tile_copies(step, slot):
        c.wait()

    slope = slopes_ref[step % H_B]
    scale = DH_B ** -0.5
    rows = kown_ref.shape[0]
    k_own, v_own = kown_ref[...], vown_ref[...]
    own_pos = past_len + lax.broadcasted_iota(jnp.int32, (rows, 1), 0)
    page_row = lax.broadcasted_iota(jnp.int32, (PAGE_SIZE, 1), 0)
    out_row = lax.broadcasted_iota(jnp.int32, (rows, 1), 0)
    out = jnp.zeros((rows, DH_B), F32)
    for t in range(n_tok):
        q = q_ref[t:t + 1, :]
        t_pos = past_len + t

        def score(keys, k_pos):
            return (jnp.sum(keys * q, axis=-1, keepdims=True) * scale
                    - slope * (t_pos - k_pos).astype(F32))

        scores = [jnp.where(own_pos <= t_pos, score(k_own, own_pos), -jnp.inf)]
        values = [v_own]
        for s in range(MOBA_TOPK):
            block = picked_block(step, t, s)
            for p in range(PAGES_PER_BLOCK):
                j = t * tiles_per_tok + s * PAGES_PER_BLOCK + p
                scores.append(score(kbuf[slot, j], block * MOBA_BLOCK + p * PAGE_SIZE + page_row))
                values.append(vbuf[slot, j])
        m = functools.reduce(jnp.maximum, [jnp.max(s, axis=0, keepdims=True) for s in scores])
        probs = [jnp.exp(s - m) for s in scores]
        l = functools.reduce(jnp.add, [jnp.sum(p, axis=0, keepdims=True) for p in probs])
        acc = functools.reduce(jnp.add, [jnp.sum(p * v, axis=0, keepdims=True) for p, v in zip(probs, values)])
        out = jnp.where(out_row == t, acc / l, out)
    o_ref[...] = out.astype(o_ref.dtype)


def _moba_sample(qn, kn, z, cache_k, cache_v, picks_flat, page_table_flat, slopes, layer, *, n_tok, n_pages):
    b, rows, _ = qn.shape
    n_tiles = n_tok * MOBA_TOPK * PAGES_PER_BLOCK
    own_spec = pl.BlockSpec((None, rows, DH_B), lambda st, sel, pt: (st // H_B, 0, st % H_B))
    return pl.pallas_call(
        functools.partial(_moba_sample_kernel, layer=layer, n_tok=n_tok, n_pages=n_pages),
        out_shape=jax.ShapeDtypeStruct((b, rows, D_B), BF16),
        grid_spec=pltpu.PrefetchScalarGridSpec(
            num_scalar_prefetch=2,
            grid=(b * H_B,),
            in_specs=[pl.BlockSpec(memory_space=pltpu.SMEM), own_spec, own_spec,
                      pl.BlockSpec((None, rows, DH_B),
                                   lambda st, sel, pt: (st // H_B, 0, OFF_VB // DH_B + st % H_B)),
                      pl.BlockSpec(memory_space=pl.ANY), pl.BlockSpec(memory_space=pl.ANY)],
            out_specs=pl.BlockSpec((None, rows, DH_B), lambda st, sel, pt: (st // H_B, 0, st % H_B)),
            scratch_shapes=[pltpu.VMEM((2, n_tiles, PAGE_SIZE, DH_B), F32),
                            pltpu.VMEM((2, n_tiles, PAGE_SIZE, DH_B), F32),
                            pltpu.SemaphoreType.DMA((2,))]),
        compiler_params=_params("arbitrary"),
        name="moba_sample",
    )(picks_flat, page_table_flat, slopes, qn, kn, z, cache_k, cache_v)


def _mem_attn_kernel(q_ref, gq_ref, k_ref, v_ref, o_ref):
    qn = _rms(q_ref[...], gq_ref[...])
    s = lax.dot_general(qn.astype(BF16), k_ref[...].astype(BF16), _NT, preferred_element_type=F32)
    s = s * (DH_C ** -0.5)
    e = jnp.exp(s - jnp.max(s, axis=-1, keepdims=True))
    p = e / jnp.sum(e, axis=-1, keepdims=True)
    o_ref[...] = _dot(p.astype(BF16), v_ref[...].astype(BF16)).astype(o_ref.dtype)


def _mem_attn(z, g_qc, mem_k, kv, layer, *, tq):
    b, l, _ = z.shape
    tq = _tile(l, tq)
    return pl.pallas_call(
        _mem_attn_kernel,
        out_shape=jax.ShapeDtypeStruct((b, l, D_C), BF16),
        grid=(b, H_C, l // tq),
        in_specs=[pl.BlockSpec((None, tq, DH_C), lambda bi, h, i: (bi, i, OFF_QC // DH_C + h)),
                  pl.BlockSpec((None, 1, DH_C), lambda bi, h, i: (layer, 0, 0)),
                  pl.BlockSpec((None, N_MEM, DH_C), lambda bi, h, i: (bi, 0, h)),
                  pl.BlockSpec((None, N_MEM, DH_C), lambda bi, h, i: (bi, 0, D_C // DH_C + h))],
        out_specs=pl.BlockSpec((None, tq, DH_C), lambda bi, h, i: (bi, i, h)),
        compiler_params=_params("parallel", "parallel", "arbitrary"),
        name="mem_attn",
    )(z, g_qc, mem_k, kv)


def _mem_attn_cached_kernel(q_ref, gq_ref, k_ref, v_ref, o_ref):
    g = gq_ref[...]
    heads = [slice(h * DH_C, (h + 1) * DH_C) for h in range(H_C)]
    scores = [lax.dot_general(_rms(q_ref[:, sl], g).astype(BF16), k_ref[:, h, :].astype(BF16), _NT,
                              preferred_element_type=F32) * (DH_C ** -0.5) for h, sl in enumerate(heads)]
    probs = []
    for s in scores:
        e = jnp.exp(s - jnp.max(s, axis=-1, keepdims=True))
        probs.append((e / jnp.sum(e, axis=-1, keepdims=True)).astype(BF16))
    for h, sl in enumerate(heads):
        o_ref[:, sl] = _dot(probs[h], v_ref[:, h, :].astype(BF16)).astype(o_ref.dtype)


def _mem_attn_cached(z, g_qc, cache_mem_k, cache_mem_v, layer):
    b, rows, _ = z.shape
    cache_spec = pl.BlockSpec((None, None, N_MEM, H_C, DH_C), lambda bi: (layer, bi, 0, 0, 0))
    return pl.pallas_call(
        _mem_attn_cached_kernel,
        out_shape=jax.ShapeDtypeStruct((b, rows, D_C), BF16),
        grid=(b,),
        in_specs=[pl.BlockSpec((None, rows, D_C), lambda bi: (bi, 0, OFF_QC // D_C)),
                  pl.BlockSpec((None, 1, DH_C), lambda bi: (layer, 0, 0)),
                  cache_spec, cache_spec],
        out_specs=pl.BlockSpec((None, rows, D_C), lambda bi: (bi, 0, 0)),
        compiler_params=_params("parallel"),
        name="mem_attn_cached",
    )(z, g_qc, cache_mem_k, cache_mem_v)


def _mixer_and_ffn(xs, layer, w, branches, next_page_means=None):
    shapes = [x.shape[:2] for x in xs]
    x2 = [x.reshape(-1, D_MODEL) for x in xs]
    h = [_rmsnorm(x, w["norm_mix"], layer) for x in x2]
    z2 = _mm_groups([[a] for a in h], [w["w_in"]], layer, epilogue=_ep_plain, out_dtype=F32,
                    tm=1024, tn=1024, name="mm_in")
    outs = [fn(z.reshape(*s, D_IN)) for fn, z, s in zip(branches, z2, shapes)]
    gates = [(z2, OFF_GATE + k * D_MODEL) for k in range(N_BRANCH)]
    merged = _mm_groups([list(o[:N_BRANCH]) for o in outs], [w["w_br_a"], w["w_br_b"], w["w_br_c"]], layer,
                        epilogue=_ep_gated_sum, extras=gates, out_dtype=BF16, tm=1024, tn=512, name="mm_merge")
    x2 = _mm_groups([[a] for a in merged], [w["w_out"]], layer, epilogue=_ep_residual, extras=[(x2, 0)],
                    out_dtype=F32, tm=1024, tn=1024, name="mm_out")
    h = [_rmsnorm(x, w["norm_ffn"], layer) for x in x2]
    act = _mm_groups([[a] for a in h], [w["w_gate"], w["w_up"]], layer, act_of_w=(0, 0), epilogue=_ep_swiglu,
                     out_dtype=BF16, tm=1024, tn=512, name="mm_swiglu", page_means=next_page_means)
    act, kmean_next = (act[:-1], act[-1]) if next_page_means is not None else (act, None)
    x2 = _mm_groups([[a] for a in act], [w["w_down"]], layer, epilogue=_ep_residual, extras=[(x2, 0)],
                    out_dtype=F32, tm=512, tn=512, name="mm_down")
    return ([x.reshape(*s, D_MODEL) for x, s in zip(x2, shapes)],
            [z.reshape(*s, D_IN) for z, s in zip(z2, shapes)], [o[N_BRANCH] for o in outs], kmean_next)


def kernel(x_prompt, x_sample, state_hgrn, cache_k, cache_v, cache_mem_k, cache_mem_v, page_table,
           mem_prompt, lb_logits, norm_mix, w_in, norm_o_a, norm_q_b, norm_k_b, norm_q_c, norm_k_c,
           norm_mem, w_mem_kv, w_br_a, w_br_b, w_br_c, w_out, norm_ffn, w_gate, w_up, w_down):
    depth = w_in.shape[0]
    n_prompt, seq, _ = x_prompt.shape
    n_seq, n_tok, _ = x_sample.shape
    n_pages = page_table.shape[1]
    assert seq % MOBA_BLOCK == 0 and n_pages % PAGES_PER_BLOCK == 0
    assert n_pages // PAGES_PER_BLOCK >= MOBA_TOPK and n_tok <= SAMPLE_ROWS <= MOBA_BLOCK

    row = lambda g: g.reshape(depth, 1, g.shape[-1])
    w = {"norm_mix": row(norm_mix), "norm_ffn": row(norm_ffn), "w_in": w_in, "w_br_a": w_br_a, "w_br_b": w_br_b,
         "w_br_c": w_br_c, "w_out": w_out, "w_gate": w_gate, "w_up": w_up, "w_down": w_down}
    g_oa, g_qb, g_kb, g_qc, g_kc, g_mem = (row(g) for g in (norm_o_a, norm_q_b, norm_k_b, norm_q_c,
                                                           norm_k_c, norm_mem))
    lower = _lower_bounds(lb_logits).reshape(depth, 1, D_A)
    slopes = 2.0 ** (-8.0 * jnp.arange(1, H_B + 1, dtype=F32) / H_B)

    pt_flat = page_table.reshape(-1)
    s0_prompt = jnp.zeros((1, n_prompt, H_A, DK_A, DV_A), F32)
    mem2 = mem_prompt.reshape(n_prompt * N_MEM, D_MODEL)

    xp = x_prompt
    xs = jnp.pad(x_sample, ((0, 0), (0, SAMPLE_ROWS - n_tok), (0, 0)))
    outs = {k: [] for k in ("hp", "mkp", "mvp", "hs", "ks", "vs")}
    kv_prompt = tuple(jnp.zeros((depth, n_prompt, seq, H_B, DH_B), F32) for _ in range(2))

    page_means = {}
    for layer in range(depth):
        (kv,) = _mm_groups([[_rmsnorm(mem2, g_mem, layer)]], [w_mem_kv], layer, epilogue=_ep_plain,
                           out_dtype=F32, tm=1024, tn=1024, name="mm_mem")
        kv = kv.reshape(n_prompt, N_MEM, 2 * D_C)
        mem_k = _head_norm(kv, g_kc, layer, col_block=0, heads=H_C, hd=DH_C)
        mem_v = kv[:, :, D_C:]

        def prompt_branches(z):
            if layer in page_means:
                o_a, s_new = _hgrn2(z, lower, g_oa, s0_prompt, layer, 0, chunk=64)
            else:
                o_a, s_new, page_means[layer] = _hgrn2(z, lower, g_oa, s0_prompt, layer, 0, chunk=64,
                                                       page_means=(cache_k, pt_flat, layer, n_seq, n_pages))
            kb, vt, kmean, kv_stacked = _moba_kv(z, g_kb, layer, depth, kv_prompt)
            o_b = _moba_prompt(z, g_qb, kb, vt, kmean, slopes, layer)
            o_c = _mem_attn(z, g_qc, mem_k, kv, layer, tq=2048)
            m = n_prompt * seq
            return o_a.reshape(m, D_A), o_b.reshape(m, D_B), o_c.reshape(m, D_C), (s_new, kv_stacked)

        def sample_branches(z):
            o_a, s_new = _hgrn2(z, lower, g_oa, state_hgrn, layer, layer, chunk=SAMPLE_ROWS, valid_len=n_tok)
            kn = _head_norm(z, g_kb, layer, col_block=OFF_KB // D_B, heads=H_B, hd=DH_B)
            kmean = page_means[layer].reshape(n_seq, -1, D_B)
            qn, picks = _moba_select(z, g_qb, kmean, layer)
            picks_flat = picks[:, :, :n_tok, :MOBA_TOPK].reshape(-1)
            o_b = _moba_sample(qn, kn, z, cache_k, cache_v, picks_flat, pt_flat, slopes, layer,
                               n_tok=n_tok, n_pages=n_pages)
            o_c = _mem_attn_cached(z, g_qc, cache_mem_k, cache_mem_v, layer)
            m = n_seq * SAMPLE_ROWS
            return o_a.reshape(m, D_A), o_b.reshape(m, D_B), o_c.reshape(m, D_C), (s_new, kn)

        next_means = (cache_k, pt_flat, layer + 1, n_seq, n_pages) if layer + 1 < depth else None
        (xp, xs), (_, zs), ((s_p, kv_prompt), (s_s, k_s)), kmean_next = _mixer_and_ffn(
            [xp, xs], layer, w, [prompt_branches, sample_branches], next_means)
        if next_means is not None:
            page_means[layer + 1] = kmean_next

        outs["hp"].append(s_p)
        outs["mkp"].append(mem_k.reshape(n_prompt, N_MEM, H_C, DH_C))
        outs["mvp"].append(mem_v.reshape(n_prompt, N_MEM, H_C, DH_C))
        outs["hs"].append(s_s)
        outs["ks"].append(k_s[:, :n_tok].reshape(n_seq, n_tok, H_B, DH_B))
        outs["vs"].append(zs[:, :n_tok, OFF_VB:OFF_VB + D_B].reshape(n_seq, n_tok, H_B, DH_B))

    stack = lambda k: jnp.stack(outs[k])
    return (xp, xs[:, :n_tok], stack("hp"), kv_prompt[0], kv_prompt[1], stack("mkp"), stack("mvp"),
            stack("hs"), stack("ks"), stack("vs"))
```

```python
import functools

import jax
import jax.numpy as jnp
from jax import lax
from jax.experimental import pallas as pl
from jax.experimental.pallas import tpu as pltpu

D_MODEL = 2048
H_A, DK_A, DV_A = 8, 128, 128
D_A = H_A * DK_A
H_B, DH_B = 8, 128
D_B = H_B * DH_B
MOBA_BLOCK = 256
MOBA_TOPK = 3
PAGE_SIZE = 128
PAGES_PER_BLOCK = MOBA_BLOCK // PAGE_SIZE
N_MEM = 256
H_C, DH_C = 4, 256
D_C = H_C * DH_C
N_BRANCH = 3
D_IN = 4 * D_A + 3 * D_B + D_C + N_BRANCH * D_MODEL
EPS = 1e-6

OFF_QA, OFF_FA, OFF_IA, OFF_GA = 0, D_A, 2 * D_A, 3 * D_A
OFF_QB = 4 * D_A
OFF_KB = OFF_QB + D_B
OFF_VB = OFF_KB + D_B
OFF_QC = OFF_VB + D_B
OFF_GATE = OFF_QC + D_C

GLA_SUB = 16
GLA_SAFE_DROP = 80.0
SAMPLE_ROWS = 16
MOBA_HEADS_PER_STEP = 4
V7X_VMEM_LIMIT = 56 * 1024 * 1024

F32 = jnp.float32
BF16 = jnp.bfloat16
_NT = (((1,), (1,)), ((), ()))
_TN = (((0,), (0,)), ((), ()))


def _params(*semantics):
    return pltpu.CompilerParams(dimension_semantics=semantics, vmem_limit_bytes=V7X_VMEM_LIMIT)


def _tile(n, pref):
    t = min(n, pref)
    assert n % t == 0, (n, pref)
    return t


def _rms(x, gain):
    return x * lax.rsqrt(jnp.mean(x * x, axis=-1, keepdims=True) + EPS) * gain


def _rmsnorm_kernel(x_ref, g_ref, o_ref):
    o_ref[...] = _rms(x_ref[...], g_ref[...]).astype(o_ref.dtype)


def _rmsnorm(x, gains, layer):
    m, d = x.shape
    tm = _tile(m, 512)
    return pl.pallas_call(
        _rmsnorm_kernel,
        out_shape=jax.ShapeDtypeStruct((m, d), BF16),
        grid=(m // tm,),
        in_specs=[pl.BlockSpec((tm, d), lambda i: (i, 0)),
                  pl.BlockSpec((None, 1, d), lambda i: (layer, 0, 0))],
        out_specs=pl.BlockSpec((tm, d), lambda i: (i, 0)),
        compiler_params=_params("parallel"),
        name="rmsnorm",
    )(x, gains)


def _dot(a, b):
    return jnp.dot(a, b, preferred_element_type=F32)


def _mm_groups_kernel(*refs, n_groups, n_w, act_of_w, n_extra, epilogue, side=None):
    n_act = len(set(act_of_w))
    pos = 0
    acts = [refs[pos + g * n_act:pos + (g + 1) * n_act] for g in range(n_groups)]
    pos += n_groups * n_act
    weights = refs[pos:pos + n_w]
    pos += n_w
    extras = [refs[pos + g * n_extra:pos + (g + 1) * n_extra] for g in range(n_groups)]
    pos += n_groups * n_extra
    outs = refs[pos:pos + n_groups]
    wb = refs[pos + n_groups:]

    def compute(g):
        dots = [_dot(acts[g][act_of_w[k]][...], wb[k][...]) for k in range(n_w)]
        outs[g][...] = epilogue(dots, [e[...] for e in extras[g]]).astype(outs[g].dtype)

    if side is not None:
        side[0]()

    @pl.when(pl.program_id(1) == 0)
    def _():
        for k in range(n_w):
            wb[k][...] = weights[k][...].astype(BF16)
        for g in range(1, n_groups):
            compute(g)

    compute(0)
    if side is not None:
        side[1]()


def _page_mean_rider(pt_ref, cache_ref, km_ref, buf, sem, *, layer, n_seq, n_pages, per_step):
    blocks_per_seq = n_pages // PAGES_PER_BLOCK
    last_block = n_seq * blocks_per_seq - 1
    step = pl.program_id(0) * pl.num_programs(1) + pl.program_id(1)
    n_steps = pl.num_programs(0) * pl.num_programs(1)
    slot = step % 2

    def block_of(st, r):
        g = jnp.minimum(st * per_step + r, last_block)
        return g // blocks_per_seq, g % blocks_per_seq

    def page_copies(st, into):
        copies = []
        for r in range(per_step):
            seq, blk = block_of(st, r)
            for p in range(PAGES_PER_BLOCK):
                page = pt_ref[seq * n_pages + blk * PAGES_PER_BLOCK + p]
                copies.append(pltpu.make_async_copy(cache_ref.at[layer, page],
                                                    buf.at[into, r * PAGES_PER_BLOCK + p], sem.at[into]))
        return copies

    def before():
        @pl.when(step == 0)
        def _():
            for c in page_copies(step, slot):
                c.start()

        @pl.when(step + 1 < n_steps)
        def _():
            for c in page_copies(step + 1, 1 - slot):
                c.start()

        for c in page_copies(step, slot):
            c.wait()

    def beside():
        for r in range(per_step):
            seq, blk = block_of(step, r)
            total = jnp.sum(buf[slot, r * PAGES_PER_BLOCK], axis=0)
            for p in range(1, PAGES_PER_BLOCK):
                total = total + jnp.sum(buf[slot, r * PAGES_PER_BLOCK + p], axis=0)
            km_ref[seq, blk] = total * (1.0 / MOBA_BLOCK)

    return before, beside


def _mm_groups_rider_kernel(pt_ref, *refs, n_in, n_out, n_w, mm_kwargs, rider_kwargs):
    ins, cache_ref = refs[:n_in], refs[n_in]
    outs, km_ref = refs[n_in + 1:n_in + 1 + n_out], refs[n_in + 1 + n_out]
    wb = refs[n_in + 2 + n_out:n_in + 2 + n_out + n_w]
    buf, sem = refs[n_in + 2 + n_out + n_w:]
    side = _page_mean_rider(pt_ref, cache_ref, km_ref, buf, sem, **rider_kwargs)
    _mm_groups_kernel(*ins, *outs, *wb, side=side, **mm_kwargs)


def _mm_groups(acts, weights, layer, *, epilogue, out_dtype, tm, tn, act_of_w=None, extras=(), name,
               page_means=None):
    n_groups, n_w = len(acts), len(weights)
    act_of_w = tuple(range(n_w)) if act_of_w is None else tuple(act_of_w)
    k, n = weights[0].shape[1:]
    m = [a[0].shape[0] for a in acts]
    tm, tn = _tile(m[0], tm), _tile(n, tn)
    rows = [tm] + m[1:]
    grid = (n // tn, m[0] // tm)

    def row_block(g):
        return (lambda j, i, *_: (i, 0)) if g == 0 else (lambda j, i, *_: (0, 0))

    def out_block(g, first=0):
        return (lambda j, i, *_: (i, first + j)) if g == 0 else (lambda j, i, *_: (0, first + j))

    in_specs, args = [], []
    for g in range(n_groups):
        in_specs += [pl.BlockSpec((rows[g], a.shape[1]), row_block(g)) for a in acts[g]]
        args += list(acts[g])
    in_specs += [pl.BlockSpec((None, k, tn), lambda j, i, *_: (layer, 0, j))] * n_w
    args += list(weights)
    for g in range(n_groups):
        for arrays, col0 in extras:
            in_specs.append(pl.BlockSpec((rows[g], tn), out_block(g, col0 // tn)))
            args.append(arrays[g])
    out_shape = [jax.ShapeDtypeStruct((m[g], n), out_dtype) for g in range(n_groups)]
    out_specs = [pl.BlockSpec((rows[g], tn), out_block(g)) for g in range(n_groups)]
    scratch = [pltpu.VMEM((k, tn), BF16)] * n_w
    mm_kwargs = dict(n_groups=n_groups, n_w=n_w, act_of_w=act_of_w, n_extra=len(extras), epilogue=epilogue)
    if page_means is None:
        return pl.pallas_call(
            functools.partial(_mm_groups_kernel, **mm_kwargs),
            out_shape=tuple(out_shape), grid=grid, in_specs=in_specs, out_specs=tuple(out_specs),
            scratch_shapes=scratch, compiler_params=_params("arbitrary", "arbitrary"), name=name,
        )(*args)

    cache_k, pt_flat, cache_layer, n_seq, n_pages = page_means
    n_blocks = n_pages // PAGES_PER_BLOCK
    per_step = -(-n_seq * n_blocks // (grid[0] * grid[1]))
    km_shape = (n_seq, n_blocks, H_B, DH_B)
    return pl.pallas_call(
        functools.partial(_mm_groups_rider_kernel, n_in=len(args), n_out=n_groups, n_w=n_w, mm_kwargs=mm_kwargs,
                          rider_kwargs=dict(layer=cache_layer, n_seq=n_seq, n_pages=n_pages, per_step=per_step)),
        out_shape=tuple(out_shape) + (jax.ShapeDtypeStruct(km_shape, F32),),
        grid_spec=pltpu.PrefetchScalarGridSpec(
            num_scalar_prefetch=1, grid=grid,
            in_specs=in_specs + [pl.BlockSpec(memory_space=pl.ANY)],
            out_specs=tuple(out_specs) + (pl.BlockSpec(km_shape, lambda j, i, *_: (0, 0, 0, 0)),),
            scratch_shapes=scratch + [pltpu.VMEM((2, per_step * PAGES_PER_BLOCK, PAGE_SIZE, H_B, DH_B), F32),
                                      pltpu.SemaphoreType.DMA((2,))]),
        compiler_params=_params("arbitrary", "arbitrary"), name=name,
    )(pt_flat, *args, cache_k)


def _ep_plain(dots, extras):
    return dots[0]


def _ep_residual(dots, extras):
    return extras[0] + dots[0]


def _ep_swiglu(dots, extras):
    return dots[0] * jax.nn.sigmoid(dots[0]) * dots[1]


def _ep_gated_sum(dots, extras):
    return functools.reduce(jnp.add, [jax.nn.sigmoid(g) * d for g, d in zip(extras, dots)])


def _lower_bounds_kernel(x_ref, o_ref):
    x = x_ref[...]
    e = jnp.exp(x - jnp.max(x, axis=0, keepdims=True))
    p = e / jnp.sum(e, axis=0, keepdims=True)
    rows = [jnp.zeros_like(p[0:1])]
    for r in range(1, x.shape[0]):
        rows.append(rows[-1] + p[r:r + 1])
    o_ref[...] = jnp.concatenate(rows, axis=0)


def _lower_bounds(lb_logits):
    return pl.pallas_call(
        _lower_bounds_kernel,
        out_shape=jax.ShapeDtypeStruct(lb_logits.shape, F32),
        name="hgrn2_lower_bounds",
    )(lb_logits)


def _hgrn2_pair_weights_factored(q, kk, G, g_mid):
    qt = (q * jnp.exp(G - g_mid)).astype(BF16)
    kt = (kk * jnp.exp(g_mid - G)).astype(BF16)
    a = lax.dot_general(qt, kt, _NT, preferred_element_type=F32)
    causal = lax.broadcasted_iota(jnp.int32, a.shape, 1) <= lax.broadcasted_iota(jnp.int32, a.shape, 0)
    return jnp.where(causal, a, 0.0).astype(BF16)


def _hgrn2_intra_exact(q, kk, v, G, rmod, chunk):
    pieces = [jnp.zeros((GLA_SUB, DV_A), F32)]
    for i in range(1, chunk // GLA_SUB):
        lo = i * GLA_SUB
        r = G[lo - 1:lo]
        qi = (q[lo:lo + GLA_SUB] * jnp.exp(G[lo:lo + GLA_SUB] - r)).astype(BF16)
        kj = (kk[:lo] * jnp.exp(r - G[:lo])).astype(BF16)
        a = lax.dot_general(qi, kj, _NT, preferred_element_type=F32)
        pieces.append(_dot(a.astype(BF16), v[:lo].astype(BF16)))
    o = jnp.concatenate(pieces, axis=0)
    for d in range(GLA_SUB):
        k_d = kk if d == 0 else pltpu.roll(kk, d, axis=0)
        g_d = G if d == 0 else pltpu.roll(G, d, axis=0)
        v_d = v if d == 0 else pltpu.roll(v, d, axis=0)
        decay = jnp.exp(jnp.where(rmod >= d, G - g_d, -jnp.inf))
        a_d = jnp.sum(q * k_d * decay, axis=-1, keepdims=True)
        o = o + a_d * v_d
    return o


def _hgrn2_kernel(q_ref, f_ref, i_ref, g_ref, lb_ref, gn_ref, s0_ref, o_ref, sout_ref,
                  st_ref, k_ref, qs_ref, G_ref, oi_ref, *, chunk, valid_len, side=None):
    c = pl.program_id(1)
    if side is not None:
        side[0]()

    @pl.when(c == 0)
    def _():
        for h in range(H_A):
            st_ref[h] = s0_ref[h].T

    heads = [slice(h * DK_A, (h + 1) * DK_A) for h in range(H_A)]
    row = lax.broadcasted_iota(jnp.int32, (chunk, 1), 0)
    rmod = row % GLA_SUB
    tri = (lax.broadcasted_iota(jnp.int32, (chunk, chunk), 0)
           >= lax.broadcasted_iota(jnp.int32, (chunk, chunk), 1)).astype(F32)

    for sl in heads:
        zf = f_ref[:, sl]
        lb = lb_ref[:, sl]
        e = jnp.exp(-jnp.abs(zf))
        log_sig = jnp.minimum(zf, 0.0) - jnp.log1p(e)
        la = jnp.log(lb)
        lc = jnp.log1p(-lb) + log_sig
        log_f = jnp.maximum(la, lc) + jnp.log1p(jnp.exp(-jnp.abs(la - lc)))
        kk = (1.0 - lb) * (jnp.where(zf >= 0.0, e, 1.0) / (1.0 + e))
        if valid_len is not None:
            live = (c * chunk + row) < valid_len
            log_f = jnp.where(live, log_f, 0.0)
            kk = jnp.where(live, kk, 0.0)
        qr = q_ref[:, sl]
        k_ref[:, sl] = kk
        qs_ref[:, sl] = qr * jax.nn.sigmoid(qr) * (DK_A ** -0.5)
        G_ref[:, sl] = jnp.dot(tri, log_f, precision=lax.Precision.HIGHEST, preferred_element_type=F32)

    mid = chunk // 2
    g_mid = G_ref[mid - 1:mid, :]
    g_last = G_ref[chunk - 1:chunk, :]
    safe = jnp.max(jnp.maximum(-g_mid, g_mid - g_last)) < GLA_SAFE_DROP

    @pl.when(safe)
    def _():
        weights = [_hgrn2_pair_weights_factored(qs_ref[:, sl], k_ref[:, sl], G_ref[:, sl], g_mid[:, sl])
                   for sl in heads]
        for sl, a in zip(heads, weights):
            oi_ref[:, sl] = _dot(a, i_ref[:, sl].astype(BF16))

    @pl.when(jnp.logical_not(safe))
    def _():
        for sl in heads:
            oi_ref[:, sl] = _hgrn2_intra_exact(qs_ref[:, sl], k_ref[:, sl], i_ref[:, sl], G_ref[:, sl], rmod, chunk)

    carried = [lax.dot_general((qs_ref[:, sl] * jnp.exp(G_ref[:, sl])).astype(BF16), st_ref[h].astype(BF16),
                               _NT, preferred_element_type=F32) for h, sl in enumerate(heads)]
    added = [lax.dot_general(i_ref[:, sl].astype(BF16),
                             (k_ref[:, sl] * jnp.exp(g_last[:, sl] - G_ref[:, sl])).astype(BF16),
                             _TN, preferred_element_type=F32) for sl in heads]
    gn = gn_ref[...]
    for h, sl in enumerate(heads):
        st_ref[h] = st_ref[h] * jnp.exp(g_last[:, sl]) + added[h]
        gr = g_ref[:, sl]
        o = oi_ref[:, sl] + carried[h]
        o_ref[:, sl] = (_rms(o, gn) * (gr * jax.nn.sigmoid(gr))).astype(o_ref.dtype)
    if side is not None:
        side[1]()

    @pl.when(c == pl.num_programs(1) - 1)
    def _():
        for h in range(H_A):
            sout_ref[h] = st_ref[h].T


def _hgrn2_rider_kernel(pt_ref, *refs, n_in, chunk, valid_len, rider_kwargs):
    ins, cache_ref = refs[:n_in], refs[n_in]
    o_ref, sout_ref, km_ref = refs[n_in + 1:n_in + 4]
    scratch, (buf, sem) = refs[n_in + 4:-2], refs[-2:]
    side = _page_mean_rider(pt_ref, cache_ref, km_ref, buf, sem, **rider_kwargs)
    _hgrn2_kernel(*ins, o_ref, sout_ref, *scratch, chunk=chunk, valid_len=valid_len, side=side)


def _hgrn2(z, lower, g_onorm, s0, layer, s0_layer, *, chunk, valid_len=None, page_means=None):
    b, l, _ = z.shape
    chunk = _tile(l, chunk)
    grid = (b, l // chunk)

    def col_spec(off):
        return pl.BlockSpec((None, chunk, D_A), lambda bi, c, *_: (bi, c, off // D_A))

    in_specs = [col_spec(OFF_QA), col_spec(OFF_FA), col_spec(OFF_IA), col_spec(OFF_GA),
                pl.BlockSpec((None, 1, D_A), lambda bi, c, *_: (layer, 0, 0)),
                pl.BlockSpec((None, 1, DV_A), lambda bi, c, *_: (layer, 0, 0)),
                pl.BlockSpec((None, None, H_A, DK_A, DV_A), lambda bi, c, *_: (s0_layer, bi, 0, 0, 0))]
    out_shape = (jax.ShapeDtypeStruct((b, l, D_A), BF16), jax.ShapeDtypeStruct((b, H_A, DK_A, DV_A), F32))
    out_specs = (pl.BlockSpec((None, chunk, D_A), lambda bi, c, *_: (bi, c, 0)),
                 pl.BlockSpec((None, H_A, DK_A, DV_A), lambda bi, c, *_: (bi, 0, 0, 0)))
    scratch = [pltpu.VMEM((H_A, DV_A, DK_A), F32)] + [pltpu.VMEM((chunk, D_A), F32)] * 4
    args = (z, z, z, z, lower, g_onorm, s0)
    if page_means is None:
        return pl.pallas_call(
            functools.partial(_hgrn2_kernel, chunk=chunk, valid_len=valid_len),
            out_shape=out_shape, grid=grid, in_specs=in_specs, out_specs=out_specs, scratch_shapes=scratch,
            compiler_params=_params("parallel", "arbitrary"), name="hgrn2",
        )(*args)

    cache_k, pt_flat, cache_layer, n_seq, n_pages = page_means
    n_blocks = n_pages // PAGES_PER_BLOCK
    per_step = -(-n_seq * n_blocks // (grid[0] * grid[1]))
    km_shape = (n_seq, n_blocks, H_B, DH_B)
    return pl.pallas_call(
        functools.partial(_hgrn2_rider_kernel, n_in=len(args), chunk=chunk, valid_len=valid_len,
                          rider_kwargs=dict(layer=cache_layer, n_seq=n_seq, n_pages=n_pages, per_step=per_step)),
        out_shape=out_shape + (jax.ShapeDtypeStruct(km_shape, F32),),
        grid_spec=pltpu.PrefetchScalarGridSpec(
            num_scalar_prefetch=1, grid=grid,
            in_specs=in_specs + [pl.BlockSpec(memory_space=pl.ANY)],
            out_specs=out_specs + (pl.BlockSpec(km_shape, lambda bi, c, *_: (0, 0, 0, 0)),),
            scratch_shapes=scratch + [pltpu.VMEM((2, per_step * PAGES_PER_BLOCK, PAGE_SIZE, H_B, DH_B), F32),
                                      pltpu.SemaphoreType.DMA((2,))]),
        compiler_params=_params("arbitrary", "arbitrary"), name="hgrn2",
    )(pt_flat, *args, cache_k)


def _head_norm_kernel(x_ref, g_ref, o_ref, *, heads, hd):
    g = g_ref[...]
    for h in range(heads):
        sl = slice(h * hd, (h + 1) * hd)
        o_ref[:, sl] = _rms(x_ref[:, sl], g)


def _head_norm(x, gains, layer, *, col_block, heads, hd):
    b, r, _ = x.shape
    width = heads * hd
    return pl.pallas_call(
        functools.partial(_head_norm_kernel, heads=heads, hd=hd),
        out_shape=jax.ShapeDtypeStruct((b, r, width), F32),
        grid=(b,),
        in_specs=[pl.BlockSpec((None, r, width), lambda bi: (bi, 0, col_block)),
                  pl.BlockSpec((None, 1, hd), lambda bi: (layer, 0, 0))],
        out_specs=pl.BlockSpec((None, r, width), lambda bi: (bi, 0, 0)),
        compiler_params=_params("parallel"),
        name="head_norm",
    )(x, gains)


def _moba_kv_kernel(k_ref, v_ref, g_ref, *refs, layer, n_carried):
    kb_ref, vt_ref, km_ref, knew_ref, vnew_ref, ks_ref, vs_ref, sem = refs[n_carried:]
    n_blk = pl.num_programs(1)
    step = pl.program_id(0) * n_blk + pl.program_id(1)
    n_steps = pl.num_programs(0) * n_blk
    slot = step % 2

    def out_copies(st, half):
        bi, n = st // n_blk, st % n_blk
        rows = pl.ds(pl.multiple_of(n * MOBA_BLOCK, MOBA_BLOCK), MOBA_BLOCK)
        copies = []
        for h in range(H_B):
            cols = slice(h * DH_B, (h + 1) * DH_B)
            copies.append(pltpu.make_async_copy(ks_ref.at[half, :, cols], knew_ref.at[layer, bi, rows, h, :], sem.at[half]))
            copies.append(pltpu.make_async_copy(vs_ref.at[half, :, cols], vnew_ref.at[layer, bi, rows, h, :], sem.at[half]))
        return copies

    @pl.when(step >= 2)
    def _():
        for c in out_copies(step - 2, slot):
            c.wait()

    g = g_ref[...]
    for h in range(H_B):
        sl = slice(h * DH_B, (h + 1) * DH_B)
        kn = _rms(k_ref[:, sl], g)
        ks_ref[slot, :, sl] = kn
        kb_ref[:, sl] = kn.astype(BF16)
        km_ref[:, sl] = jnp.mean(kn, axis=0, keepdims=True)
        vt_ref[sl, :] = v_ref[:, sl].T.astype(BF16)
    vs_ref[slot] = v_ref[...]
    for c in out_copies(step, slot):
        c.start()

    @pl.when((step == n_steps - 1) & (step >= 1))
    def _():
        for c in out_copies(step - 1, 1 - slot):
            c.wait()

    @pl.when(step == n_steps - 1)
    def _():
        for c in out_copies(step, slot):
            c.wait()


def _moba_kv(z, g_kb, layer, depth, stacked):
    b, l, _ = z.shape
    nblk = l // MOBA_BLOCK
    heads_out = jax.ShapeDtypeStruct((depth, b, l, H_B, DH_B), F32)
    hbm = pl.BlockSpec(memory_space=pl.ANY)
    carried = tuple(stacked)
    kb, vt, km, k_new, v_new = pl.pallas_call(
        functools.partial(_moba_kv_kernel, layer=layer, n_carried=len(carried)),
        out_shape=(jax.ShapeDtypeStruct((b, l, D_B), BF16), jax.ShapeDtypeStruct((b, D_B, l), BF16),
                   jax.ShapeDtypeStruct((b, nblk, 1, D_B), F32), heads_out, heads_out),
        grid=(b, nblk),
        in_specs=[pl.BlockSpec((None, MOBA_BLOCK, D_B), lambda bi, n: (bi, n, OFF_KB // D_B)),
                  pl.BlockSpec((None, MOBA_BLOCK, D_B), lambda bi, n: (bi, n, OFF_VB // D_B)),
                  pl.BlockSpec((None, 1, DH_B), lambda bi, n: (layer, 0, 0))] + [hbm] * len(carried),
        out_specs=(pl.BlockSpec((None, MOBA_BLOCK, D_B), lambda bi, n: (bi, n, 0)),
                   pl.BlockSpec((None, D_B, MOBA_BLOCK), lambda bi, n: (bi, 0, n)),
                   pl.BlockSpec((None, None, 1, D_B), lambda bi, n: (bi, n, 0, 0)), hbm, hbm),
        scratch_shapes=[pltpu.VMEM((2, MOBA_BLOCK, D_B), F32), pltpu.VMEM((2, MOBA_BLOCK, D_B), F32),
                        pltpu.SemaphoreType.DMA((2,))],
        input_output_aliases={3 + c: 3 + c for c in range(len(carried))},
        compiler_params=_params("arbitrary", "arbitrary"),
        name="moba_kv",
    )(z, z, g_kb, *carried)
    return kb, vt, km.reshape(b, nblk, D_B), (k_new, v_new)


def _topk_mask_t(gate_t, n_valid, topk):
    n, r = gate_t.shape
    rowi = lax.broadcasted_iota(jnp.int32, (n, r), 0)
    gm = jnp.where(rowi < n_valid, gate_t, -jnp.inf)
    sel = jnp.zeros((n, r), F32)
    for j in range(n):
        gj = gm[j:j + 1]
        beats = jnp.where(gm > gj, 1.0, jnp.where((gm == gj) & (rowi < j), 1.0, 0.0))
        rank = jnp.sum(beats, axis=0, keepdims=True)
        sel = jnp.where((rowi == j) & (rank < topk) & (rowi < n_valid), 1.0, sel)
    return sel


def _moba_prompt_kernel(slopes_ref, q_ref, gq_ref, k_ref, vt_ref, km_ref, o_ref, s_ref, qb_ref, sel_ref):
    hg = pl.program_id(1)
    i = pl.program_id(2)
    blk = MOBA_BLOCK
    scale = DH_B ** -0.5
    heads = [slice(hh * DH_B, (hh + 1) * DH_B) for hh in range(MOBA_HEADS_PER_STEP)]

    @pl.when(i == 0)
    def _():
        for hh, sl in enumerate(heads):
            qn = _rms(q_ref[:, sl], gq_ref[...])
            qb_ref[:, sl] = qn.astype(BF16)
            gate_t = lax.dot_general(km_ref[:, sl], qn, _NT, precision=lax.Precision.HIGHEST,
                                     preferred_element_type=F32)
            q_block = lax.broadcasted_iota(jnp.int32, gate_t.shape, 1) // blk
            sel_ref[hh] = _topk_mask_t(gate_t, q_block, MOBA_TOPK)

    d0 = (lax.broadcasted_iota(jnp.int32, (blk, blk), 1)
          - lax.broadcasted_iota(jnp.int32, (blk, blk), 0)).astype(F32)
    slopes = [slopes_ref[hg * MOBA_HEADS_PER_STEP + hh] for hh in range(MOBA_HEADS_PER_STEP)]
    bias0 = [-slope * d0 for slope in slopes]

    def attend(own):
        cols = slice(own * blk, (own + 1) * blk)
        m = [None] * len(heads)
        for j in range(own + 1):
            rows = slice(j * blk, (j + 1) * blk)
            for hh, sl in enumerate(heads):
                s = lax.dot_general(k_ref[rows, sl], qb_ref[cols, sl], _NT, preferred_element_type=F32) * scale
                s = s + (bias0[hh] - slopes[hh] * float((own - j) * blk))
                keep = (d0 >= 0.0) if j == own else (sel_ref[hh, j:j + 1, cols] > 0.0)
                s = jnp.where(keep, s, -jnp.inf)
                s_ref[hh, rows, :] = s
                m_j = jnp.max(s, axis=0, keepdims=True)
                m[hh] = m_j if m[hh] is None else jnp.maximum(m[hh], m_j)
        l = [jnp.zeros((1, blk), F32) for _ in heads]
        acc = [jnp.zeros((DH_B, blk), F32) for _ in heads]
        for j in range(own + 1):
            rows = slice(j * blk, (j + 1) * blk)
            for hh, sl in enumerate(heads):
                p = jnp.exp(s_ref[hh, rows, :] - m[hh])
                l[hh] = l[hh] + jnp.sum(p, axis=0, keepdims=True)
                acc[hh] = acc[hh] + _dot(vt_ref[sl, rows], p.astype(BF16))
        for hh, sl in enumerate(heads):
            o_ref[:, sl] = (acc[hh] / l[hh]).T.astype(o_ref.dtype)

    for own in range(km_ref.shape[0]):
        pl.when(i == own)(functools.partial(attend, own))


def _moba_prompt(z, g_qb, kb, vt, kmean, slopes, layer):
    b, l, _ = z.shape
    nblk = l // MOBA_BLOCK
    width = MOBA_HEADS_PER_STEP * DH_B
    return pl.pallas_call(
        _moba_prompt_kernel,
        out_shape=jax.ShapeDtypeStruct((b, l, D_B), BF16),
        grid=(b, H_B // MOBA_HEADS_PER_STEP, nblk),
        in_specs=[pl.BlockSpec(memory_space=pltpu.SMEM),
                  pl.BlockSpec((None, l, width), lambda bi, h, i: (bi, 0, OFF_QB // width + h)),
                  pl.BlockSpec((None, 1, DH_B), lambda bi, h, i: (layer, 0, 0)),
                  pl.BlockSpec((None, l, width), lambda bi, h, i: (bi, 0, h)),
                  pl.BlockSpec((None, width, l), lambda bi, h, i: (bi, h, 0)),
                  pl.BlockSpec((None, nblk, width), lambda bi, h, i: (bi, 0, h))],
        out_specs=pl.BlockSpec((None, MOBA_BLOCK, width), lambda bi, h, i: (bi, i, h)),
        scratch_shapes=[pltpu.VMEM((MOBA_HEADS_PER_STEP, l, MOBA_BLOCK), F32), pltpu.VMEM((l, width), BF16),
                        pltpu.VMEM((MOBA_HEADS_PER_STEP, nblk, l), F32)],
        compiler_params=_params("parallel", "parallel", "arbitrary"),
        name="moba_prompt",
    )(slopes, z, g_qb, kb, vt, kmean)


def _moba_select_kernel(q_ref, gq_ref, km_ref, qn_ref, sel_ref):
    rows = q_ref.shape[0]
    n_blocks = km_ref.shape[0]
    g = gq_ref[...]
    col = lax.broadcasted_iota(jnp.int32, (rows, n_blocks), 1)
    lane = lax.broadcasted_iota(jnp.int32, (rows, sel_ref.shape[-1]), 1)
    for h in range(H_B):
        sl = slice(h * DH_B, (h + 1) * DH_B)
        qn = _rms(q_ref[:, sl], g)
        qn_ref[:, sl] = qn
        gate = lax.dot_general(qn, km_ref[:, sl], _NT, precision=lax.Precision.HIGHEST,
                               preferred_element_type=F32)
        picks = jnp.zeros(lane.shape, jnp.int32)
        for r in range(MOBA_TOPK):
            best = jnp.max(gate, axis=-1, keepdims=True)
            idx = jnp.min(jnp.where(gate == best, col, n_blocks), axis=-1, keepdims=True)
            picks = jnp.where(lane == r, idx, picks)
            gate = jnp.where(col == idx, -jnp.inf, gate)
        sel_ref[h] = picks


def _moba_select(z, g_qb, kmean, layer):
    b, rows, _ = z.shape
    n_blocks = kmean.shape[1]
    return pl.pallas_call(
        _moba_select_kernel,
        out_shape=(jax.ShapeDtypeStruct((b, rows, D_B), F32),
                   jax.ShapeDtypeStruct((b, H_B, rows, 128), jnp.int32)),
        grid=(b,),
        in_specs=[pl.BlockSpec((None, rows, D_B), lambda bi: (bi, 0, OFF_QB // D_B)),
                  pl.BlockSpec((None, 1, DH_B), lambda bi: (layer, 0, 0)),
                  pl.BlockSpec((None, n_blocks, D_B), lambda bi: (bi, 0, 0))],
        out_specs=(pl.BlockSpec((None, rows, D_B), lambda bi: (bi, 0, 0)),
                   pl.BlockSpec((None, H_B, rows, 128), lambda bi: (bi, 0, 0, 0))),
        compiler_params=_params("parallel"),
        name="moba_select",
    )(z, g_qb, kmean)


def _moba_sample_kernel(sel_ref, pt_ref, slopes_ref, q_ref, kown_ref, vown_ref, ck_ref, cv_ref, o_ref,
                        kbuf, vbuf, sem, *, layer, n_tok, n_pages):
    step = pl.program_id(0)
    n_steps = pl.num_programs(0)
    tiles_per_tok = MOBA_TOPK * PAGES_PER_BLOCK
    past_len = n_pages * PAGE_SIZE

    def picked_block(st, t, s):
        return sel_ref[(st * n_tok + t) * MOBA_TOPK + s]

    def tile_copies(st, slot):
        bi, h = st // H_B, st % H_B
        copies = []
        for t in range(n_tok):
            for s in range(MOBA_TOPK):
                block = picked_block(st, t, s)
                for p in range(PAGES_PER_BLOCK):
                    page = pt_ref[bi * n_pages + block * PAGES_PER_BLOCK + p]
                    j = t * tiles_per_tok + s * PAGES_PER_BLOCK + p
                    copies.append(pltpu.make_async_copy(
                        ck_ref.at[layer, page, :, h, :], kbuf.at[slot, j], sem.at[slot]))
                    copies.append(pltpu.make_async_copy(
                        cv_ref.at[layer, page, :, h, :], vbuf.at[slot, j], sem.at[slot]))
        return copies

    slot = step % 2

    @pl.when(step == 0)
    def _():
        for n, c in enumerate(tile_copies(step, slot)):
            c.start(priority=n % 2)

    @pl.when(step + 1 < n_steps)
    def _():
        for n, c in enumerate(tile_copies(step + 1, 1 - slot)):
            c.start(priority=n % 2)

    for c in tile_copies(step, slot):
        c.wait()

    slope = slopes_ref[step % H_B]
    scale = DH_B ** -0.5
    rows = kown_ref.shape[0]
    k_own, v_own = kown_ref[...], vown_ref[...]
    own_pos = past_len + lax.broadcasted_iota(jnp.int32, (rows, 1), 0)
    page_row = lax.broadcasted_iota(jnp.int32, (PAGE_SIZE, 1), 0)
    out_row = lax.broadcasted_iota(jnp.int32, (rows, 1), 0)
    out = jnp.zeros((rows, DH_B), F32)
    for t in range(n_tok):
        q = q_ref[t:t + 1, :]
        t_pos = past_len + t

        def score(keys, k_pos):
            return (jnp.sum(keys * q, axis=-1, keepdims=True) * scale
                    - slope * (t_pos - k_pos).astype(F32))

        scores = [jnp.where(own_pos <= t_pos, score(k_own, own_pos), -jnp.inf)]
        values = [v_own]
        for s in range(MOBA_TOPK):
            block = picked_block(step, t, s)
            for p in range(PAGES_PER_BLOCK):
                j = t * tiles_per_tok + s * PAGES_PER_BLOCK + p
                scores.append(score(kbuf[slot, j], block * MOBA_BLOCK + p * PAGE_SIZE + page_row))
                values.append(vbuf[slot, j])
        m = functools.reduce(jnp.maximum, [jnp.max(s, axis=0, keepdims=True) for s in scores])
        probs = [jnp.exp(s - m) for s in scores]
        l = functools.reduce(jnp.add, [jnp.sum(p, axis=0, keepdims=True) for p in probs])
        acc = functools.reduce(jnp.add, [jnp.sum(p * v, axis=0, keepdims=True) for p, v in zip(probs, values)])
        out = jnp.where(out_row == t, acc / l, out)
    o_ref[...] = out.astype(o_ref.dtype)


def _moba_sample(qn, kn, z, cache_k, cache_v, picks_flat, page_table_flat, slopes, layer, *, n_tok, n_pages):
    b, rows, _ = qn.shape
    n_tiles = n_tok * MOBA_TOPK * PAGES_PER_BLOCK
    own_spec = pl.BlockSpec((None, rows, DH_B), lambda st, sel, pt: (st // H_B, 0, st % H_B))
    return pl.pallas_call(
        functools.partial(_moba_sample_kernel, layer=layer, n_tok=n_tok, n_pages=n_pages),
        out_shape=jax.ShapeDtypeStruct((b, rows, D_B), BF16),
        grid_spec=pltpu.PrefetchScalarGridSpec(
            num_scalar_prefetch=2,
            grid=(b * H_B,),
            in_specs=[pl.BlockSpec(memory_space=pltpu.SMEM), own_spec, own_spec,
                      pl.BlockSpec((None, rows, DH_B),
                                   lambda st, sel, pt: (st // H_B, 0, OFF_VB // DH_B + st % H_B)),
                      pl.BlockSpec(memory_space=pl.ANY), pl.BlockSpec(memory_space=pl.ANY)],
            out_specs=pl.BlockSpec((None, rows, DH_B), lambda st, sel, pt: (st // H_B, 0, st % H_B)),
            scratch_shapes=[pltpu.VMEM((2, n_tiles, PAGE_SIZE, DH_B), F32),
                            pltpu.VMEM((2, n_tiles, PAGE_SIZE, DH_B), F32),
                            pltpu.SemaphoreType.DMA((2,))]),
        compiler_params=_params("arbitrary"),
        name="moba_sample",
    )(picks_flat, page_table_flat, slopes, qn, kn, z, cache_k, cache_v)


def _mem_attn_kernel(q_ref, gq_ref, k_ref, v_ref, o_ref):
    qn = _rms(q_ref[...], gq_ref[...])
    s = lax.dot_general(qn.astype(BF16), k_ref[...].astype(BF16), _NT, preferred_element_type=F32)
    s = s * (DH_C ** -0.5)
    e = jnp.exp(s - jnp.max(s, axis=-1, keepdims=True))
    p = e / jnp.sum(e, axis=-1, keepdims=True)
    o_ref[...] = _dot(p.astype(BF16), v_ref[...].astype(BF16)).astype(o_ref.dtype)


def _mem_attn(z, g_qc, mem_k, kv, layer, *, tq):
    b, l, _ = z.shape
    tq = _tile(l, tq)
    return pl.pallas_call(
        _mem_attn_kernel,
        out_shape=jax.ShapeDtypeStruct((b, l, D_C), BF16),
        grid=(b, H_C, l // tq),
        in_specs=[pl.BlockSpec((None, tq, DH_C), lambda bi, h, i: (bi, i, OFF_QC // DH_C + h)),
                  pl.BlockSpec((None, 1, DH_C), lambda bi, h, i: (layer, 0, 0)),
                  pl.BlockSpec((None, N_MEM, DH_C), lambda bi, h, i: (bi, 0, h)),
                  pl.BlockSpec((None, N_MEM, DH_C), lambda bi, h, i: (bi, 0, D_C // DH_C + h))],
        out_specs=pl.BlockSpec((None, tq, DH_C), lambda bi, h, i: (bi, i, h)),
        compiler_params=_params("parallel", "parallel", "arbitrary"),
        name="mem_attn",
    )(z, g_qc, mem_k, kv)


def _mem_attn_cached_kernel(q_ref, gq_ref, k_ref, v_ref, o_ref):
    g = gq_ref[...]
    heads = [slice(h * DH_C, (h + 1) * DH_C) for h in range(H_C)]
    scores = [lax.dot_general(_rms(q_ref[:, sl], g).astype(BF16), k_ref[:, h, :].astype(BF16), _NT,
                              preferred_element_type=F32) * (DH_C ** -0.5) for h, sl in enumerate(heads)]
    probs = []
    for s in scores:
        e = jnp.exp(s - jnp.max(s, axis=-1, keepdims=True))
        probs.append((e / jnp.sum(e, axis=-1, keepdims=True)).astype(BF16))
    for h, sl in enumerate(heads):
        o_ref[:, sl] = _dot(probs[h], v_ref[:, h, :].astype(BF16)).astype(o_ref.dtype)


def _mem_attn_cached(z, g_qc, cache_mem_k, cache_mem_v, layer):
    b, rows, _ = z.shape
    cache_spec = pl.BlockSpec((None, None, N_MEM, H_C, DH_C), lambda bi: (layer, bi, 0, 0, 0))
    return pl.pallas_call(
        _mem_attn_cached_kernel,
        out_shape=jax.ShapeDtypeStruct((b, rows, D_C), BF16),
        grid=(b,),
        in_specs=[pl.BlockSpec((None, rows, D_C), lambda bi: (bi, 0, OFF_QC // D_C)),
                  pl.BlockSpec((None, 1, DH_C), lambda bi: (layer, 0, 0)),
                  cache_spec, cache_spec],
        out_specs=pl.BlockSpec((None, rows, D_C), lambda bi: (bi, 0, 0)),
        compiler_params=_params("parallel"),
        name="mem_attn_cached",
    )(z, g_qc, cache_mem_k, cache_mem_v)


def _mixer_and_ffn(xs, layer, w, branches, next_page_means=None):
    shapes = [x.shape[:2] for x in xs]
    x2 = [x.reshape(-1, D_MODEL) for x in xs]
    h = [_rmsnorm(x, w["norm_mix"], layer) for x in x2]
    z2 = _mm_groups([[a] for a in h], [w["w_in"]], layer, epilogue=_ep_plain, out_dtype=F32,
                    tm=1024, tn=1024, name="mm_in")
    outs = [fn(z.reshape(*s, D_IN)) for fn, z, s in zip(branches, z2, shapes)]
    gates = [(z2, OFF_GATE + k * D_MODEL) for k in range(N_BRANCH)]
    merged = _mm_groups([list(o[:N_BRANCH]) for o in outs], [w["w_br_a"], w["w_br_b"], w["w_br_c"]], layer,
                        epilogue=_ep_gated_sum, extras=gates, out_dtype=BF16, tm=1024, tn=512, name="mm_merge")
    x2 = _mm_groups([[a] for a in merged], [w["w_out"]], layer, epilogue=_ep_residual, extras=[(x2, 0)],
                    out_dtype=F32, tm=1024, tn=1024, name="mm_out")
    h = [_rmsnorm(x, w["norm_ffn"], layer) for x in x2]
    act = _mm_groups([[a] for a in h], [w["w_gate"], w["w_up"]], layer, act_of_w=(0, 0), epilogue=_ep_swiglu,
                     out_dtype=BF16, tm=1024, tn=512, name="mm_swiglu", page_means=next_page_means)
    act, kmean_next = (act[:-1], act[-1]) if next_page_means is not None else (act, None)
    x2 = _mm_groups([[a] for a in act], [w["w_down"]], layer, epilogue=_ep_residual, extras=[(x2, 0)],
                    out_dtype=F32, tm=512, tn=512, name="mm_down")
    return ([x.reshape(*s, D_MODEL) for x, s in zip(x2, shapes)],
            [z.reshape(*s, D_IN) for z, s in zip(z2, shapes)], [o[N_BRANCH] for o in outs], kmean_next)


def kernel(x_prompt, x_sample, state_hgrn, cache_k, cache_v, cache_mem_k, cache_mem_v, page_table,
           mem_prompt, lb_logits, norm_mix, w_in, norm_o_a, norm_q_b, norm_k_b, norm_q_c, norm_k_c,
           norm_mem, w_mem_kv, w_br_a, w_br_b, w_br_c, w_out, norm_ffn, w_gate, w_up, w_down):
    depth = w_in.shape[0]
    n_prompt, seq, _ = x_prompt.shape
    n_seq, n_tok, _ = x_sample.shape
    n_pages = page_table.shape[1]
    assert seq % MOBA_BLOCK == 0 and n_pages % PAGES_PER_BLOCK == 0
    assert n_pages // PAGES_PER_BLOCK >= MOBA_TOPK and n_tok <= SAMPLE_ROWS <= MOBA_BLOCK

    row = lambda g: g.reshape(depth, 1, g.shape[-1])
    w = {"norm_mix": row(norm_mix), "norm_ffn": row(norm_ffn), "w_in": w_in, "w_br_a": w_br_a, "w_br_b": w_br_b,
         "w_br_c": w_br_c, "w_out": w_out, "w_gate": w_gate, "w_up": w_up, "w_down": w_down}
    g_oa, g_qb, g_kb, g_qc, g_kc, g_mem = (row(g) for g in (norm_o_a, norm_q_b, norm_k_b, norm_q_c,
                                                           norm_k_c, norm_mem))
    lower = _lower_bounds(lb_logits).reshape(depth, 1, D_A)
    slopes = 2.0 ** (-8.0 * jnp.arange(1, H_B + 1, dtype=F32) / H_B)

    pt_flat = page_table.reshape(-1)
    s0_prompt = jnp.zeros((1, n_prompt, H_A, DK_A, DV_A), F32)
    mem2 = mem_prompt.reshape(n_prompt * N_MEM, D_MODEL)

    xp = x_prompt
    xs = jnp.pad(x_sample, ((0, 0), (0, SAMPLE_ROWS - n_tok), (0, 0)))
    outs = {k: [] for k in ("hp", "mkp", "mvp", "hs", "ks", "vs")}
    kv_prompt = tuple(jnp.zeros((depth, n_prompt, seq, H_B, DH_B), F32) for _ in range(2))

    page_means = {}
    for layer in range(depth):
        (kv,) = _mm_groups([[_rmsnorm(mem2, g_mem, layer)]], [w_mem_kv], layer, epilogue=_ep_plain,
                           out_dtype=F32, tm=1024, tn=1024, name="mm_mem")
        kv = kv.reshape(n_prompt, N_MEM, 2 * D_C)
        mem_k = _head_norm(kv, g_kc, layer, col_block=0, heads=H_C, hd=DH_C)
        mem_v = kv[:, :, D_C:]

        def prompt_branches(z):
            if layer in page_means:
                o_a, s_new = _hgrn2(z, lower, g_oa, s0_prompt, layer, 0, chunk=64)
            else:
                o_a, s_new, page_means[layer] = _hgrn2(z, lower, g_oa, s0_prompt, layer, 0, chunk=64,
                                                       page_means=(cache_k, pt_flat, layer, n_seq, n_pages))
            kb, vt, kmean, kv_stacked = _moba_kv(z, g_kb, layer, depth, kv_prompt)
            o_b = _moba_prompt(z, g_qb, kb, vt, kmean, slopes, layer)
            o_c = _mem_attn(z, g_qc, mem_k, kv, layer, tq=2048)
            m = n_prompt * seq
            return o_a.reshape(m, D_A), o_b.reshape(m, D_B), o_c.reshape(m, D_C), (s_new, kv_stacked)

        def sample_branches(z):
            o_a, s_new = _hgrn2(z, lower, g_oa, state_hgrn, layer, layer, chunk=SAMPLE_ROWS, valid_len=n_tok)
            kn = _head_norm(z, g_kb, layer, col_block=OFF_KB // D_B, heads=H_B, hd=DH_B)
            kmean = page_means[layer].reshape(n_seq, -1, D_B)
            qn, picks = _moba_select(z, g_qb, kmean, layer)
            picks_flat = picks[:, :, :n_tok, :MOBA_TOPK].reshape(-1)
            o_b = _moba_sample(qn, kn, z, cache_k, cache_v, picks_flat, pt_flat, slopes, layer,
                               n_tok=n_tok, n_pages=n_pages)
            o_c = _mem_attn_cached(z, g_qc, cache_mem_k, cache_mem_v, layer)
            m = n_seq * SAMPLE_ROWS
            return o_a.reshape(m, D_A), o_b.reshape(m, D_B), o_c.reshape(m, D_C), (s_new, kn)

        next_means = (cache_k, pt_flat, layer + 1, n_seq, n_pages) if layer + 1 < depth else None
        (xp, xs), (_, zs), ((s_p, kv_prompt), (s_s, k_s)), kmean_next = _mixer_and_ffn(
            [xp, xs], layer, w, [prompt_branches, sample_branches], next_means)
        if next_means is not None:
            page_means[layer + 1] = kmean_next

        outs["hp"].append(s_p)
        outs["mkp"].append(mem_k.reshape(n_prompt, N_MEM, H_C, DH_C))
        outs["mvp"].append(mem_v.reshape(n_prompt, N_MEM, H_C, DH_C))
        outs["hs"].append(s_s)
        outs["ks"].append(k_s[:, :n_tok].reshape(n_seq, n_tok, H_B, DH_B))
        outs["vs"].append(zs[:, :n_tok, OFF_VB:OFF_VB + D_B].reshape(n_seq, n_tok, H_B, DH_B))

    stack = lambda k: jnp.stack(outs[k])
    return (xp, xs[:, :n_tok], stack("hp"), kv_prompt[0], kv_prompt[1], stack("mkp"), stack("mvp"),
            stack("hs"), stack("ks"), stack("vs"))
```
